```python
import math
import jax, jax.numpy as jnp
from jax import lax
import numpy as np

D_MODEL = 2048
BATCH = 1
SEQ = 8192
DEPTH = 2
DEC_BATCH = 128
DEC_SEQ = 1
PAST_LEN = 8192
PAGE_SIZE = 128

N_MIXERS = 2
N_POOL_LAYERS = (DEPTH + N_MIXERS - 1) // N_MIXERS
N_ATTN_LAYERS = DEPTH // N_MIXERS
POOL_WINDOWS = (2, 4, 8, 16)
POOL_GROUPS = len(POOL_WINDOWS)
POOL_GC = D_MODEL // POOL_GROUPS
POOL_MAXW = max(POOL_WINDOWS)
POOL_BUF = POOL_MAXW - 1
HEAD_DIM = 64
N_HEADS = D_MODEL // HEAD_DIM
N_KV_HEADS = N_HEADS // 4
GQA_GROUP = N_HEADS // N_KV_HEADS
WINDOW = 128
SWA_BLOCK = WINDOW
ROPE_THETA = 10000.0
N_MEM = 256
X_HEADS = 4
X_HEAD_DIM = 128
D_FF = 5632
RMS_EPS = 1e-6

kernel_name = "hybrid_pool_swa_sink_macaron_memxattn_step"


def rms_norm(x, g):
    xf = x.astype(jnp.float32)
    y = xf * lax.rsqrt(jnp.mean(xf * xf, axis=-1, keepdims=True) + RMS_EPS)
    return (y * g.astype(jnp.float32)).astype(x.dtype)


def swiglu(h, w_gu, w_dn):
    a, b = jnp.split(h @ w_gu, 2, axis=-1)
    return (jax.nn.silu(a) * b) @ w_dn


def rope(x, pos):
    half = x.shape[-1] // 2
    inv = ROPE_THETA ** (-jnp.arange(half, dtype=jnp.float32) / half)
    ang = pos.astype(jnp.float32)[:, None] * inv[None, :]
    cos = jnp.cos(ang)[None, :, None, :]
    sin = jnp.sin(ang)[None, :, None, :]
    xf = x.astype(jnp.float32)
    x1, x2 = xf[..., :half], xf[..., half:]
    return jnp.concatenate([x1 * cos - x2 * sin, x2 * cos + x1 * sin], axis=-1).astype(x.dtype)


def sink_softmax(s, sink):
    m = jnp.maximum(jnp.max(s, axis=-1, keepdims=True), sink)
    p = jnp.exp(s - m)
    return p / (jnp.sum(p, axis=-1, keepdims=True) + jnp.exp(sink - m))


def split_qkv(qkv):
    b, t, _ = qkv.shape
    nq, nk = N_HEADS * HEAD_DIM, N_KV_HEADS * HEAD_DIM
    q = qkv[..., :nq].reshape(b, t, N_HEADS, HEAD_DIM)
    k = qkv[..., nq:nq + nk].reshape(b, t, N_KV_HEADS, HEAD_DIM)
    v = qkv[..., nq + nk:].reshape(b, t, N_KV_HEADS, HEAD_DIM)
    return q, k, v


def pool_mix(h_ext, pos_ext, n_out, w_pool, scale):
    b, l, _ = h_ext.shape
    hf = h_ext.astype(jnp.float32)
    csp = jnp.concatenate([jnp.zeros((b, POOL_MAXW, D_MODEL), jnp.float32), jnp.cumsum(hf, axis=1)], axis=1)
    end = csp[:, POOL_MAXW + l - n_out:]
    pos1 = pos_ext[l - n_out:].astype(jnp.float32) + 1.0
    means = []
    for g, w in enumerate(POOL_WINDOWS):
        c0, c1 = g * POOL_GC, (g + 1) * POOL_GC
        start = csp[:, POOL_MAXW - w + l - n_out:POOL_MAXW - w + l, c0:c1]
        cnt = jnp.minimum(jnp.float32(w), pos1)[None, :, None]
        means.append((end[..., c0:c1] - start) / cnt)
    pooled = (jnp.concatenate(means, axis=-1) - hf[:, l - n_out:]).astype(h_ext.dtype)
    y = jnp.einsum('btgc,gcd->btgd', pooled.reshape(b, n_out, POOL_GROUPS, POOL_GC), w_pool)
    return y.reshape(b, n_out, D_MODEL) * scale


def swa_prompt(h, w_qkv, w_o, sinks):
    b, t, _ = h.shape
    q, k, v = split_qkv(h @ w_qkv)
    pos = jnp.arange(t)
    q, k = rope(q, pos), rope(k, pos)
    nb = t // SWA_BLOCK
    qb = q.reshape(b, nb, SWA_BLOCK, N_KV_HEADS, GQA_GROUP, HEAD_DIM)
    kb = k.reshape(b, nb, SWA_BLOCK, N_KV_HEADS, HEAD_DIM)
    vb = v.reshape(b, nb, SWA_BLOCK, N_KV_HEADS, HEAD_DIM)
    zero = jnp.zeros_like(kb[:, :1])
    kp = jnp.concatenate([zero, kb], axis=1)
    vp = jnp.concatenate([zero, vb], axis=1)
    kc = jnp.concatenate([kp[:, :-1], kp[:, 1:]], axis=2)
    vc = jnp.concatenate([vp[:, :-1], vp[:, 1:]], axis=2)
    s = jnp.einsum('bnqkgd,bnskd->bnkgqs', qb, kc).astype(jnp.float32) / math.sqrt(HEAD_DIM)
    blk = jnp.arange(nb)[:, None] * SWA_BLOCK
    qpos = blk + jnp.arange(SWA_BLOCK)[None, :]
    kpos = blk - SWA_BLOCK + jnp.arange(2 * SWA_BLOCK)[None, :]
    diff = qpos[:, :, None] - kpos[:, None, :]
    mask = (diff >= 0) & (diff < WINDOW) & (kpos[:, None, :] >= 0)
    s = jnp.where(mask[None, :, None, None], s, -jnp.inf)
    p = sink_softmax(s, sinks.astype(jnp.float32).reshape(1, 1, N_KV_HEADS, GQA_GROUP, 1, 1))
    o = jnp.einsum('bnkgqs,bnskd->bnqkgd', p.astype(vc.dtype), vc).reshape(b, t, N_HEADS * HEAD_DIM)
    keep = min(WINDOW, t)
    return o @ w_o, k[:, t - keep:], v[:, t - keep:]


def swa_sample(h, ck, cv, w_qkv, w_o, sinks):
    b, s_len, _ = h.shape
    buf = ck.shape[1]
    q, k, v = split_qkv(h @ w_qkv)
    qpos = PAST_LEN + jnp.arange(s_len)
    q, k = rope(q, qpos), rope(k, qpos)
    k_all = jnp.concatenate([ck, k], axis=1)
    v_all = jnp.concatenate([cv, v], axis=1)
    kpos = jnp.concatenate([PAST_LEN - buf + jnp.arange(buf), qpos])
    diff = qpos[:, None] - kpos[None, :]
    mask = (diff >= 0) & (diff < WINDOW)
    qg = q.reshape(b, s_len, N_KV_HEADS, GQA_GROUP, HEAD_DIM)
    s = jnp.einsum('bqkgd,bskd->bkgqs', qg, k_all).astype(jnp.float32) / math.sqrt(HEAD_DIM)
    s = jnp.where(mask, s, -jnp.inf)
    p = sink_softmax(s, sinks.astype(jnp.float32).reshape(1, N_KV_HEADS, GQA_GROUP, 1, 1))
    o = jnp.einsum('bkgqs,bskd->bqkgd', p.astype(v_all.dtype), v_all).reshape(b, s_len, N_HEADS * HEAD_DIM)
    return o @ w_o, k_all[:, -buf:], v_all[:, -buf:]


def mem_kv(mem, g, w_kv):
    b, m, _ = mem.shape
    k, v = jnp.split(rms_norm(mem, g) @ w_kv, 2, axis=-1)
    return k.reshape(b, m, X_HEADS, X_HEAD_DIM), v.reshape(b, m, X_HEADS, X_HEAD_DIM)


def cross_attn(h, mk, mv, w_q, w_o):
    b, t, _ = h.shape
    q = (h @ w_q).reshape(b, t, X_HEADS, X_HEAD_DIM)
    s = jnp.einsum('bthd,bmhd->bhtm', q, mk).astype(jnp.float32) / math.sqrt(X_HEAD_DIM)
    p = jax.nn.softmax(s, axis=-1)
    o = jnp.einsum('bhtm,bmhd->bthd', p.astype(mv.dtype), mv).reshape(b, t, X_HEADS * X_HEAD_DIM)
    return o @ w_o


def setup_inputs(seed: int = 0) -> dict:
    key = jax.random.key(seed)
    ks = iter(jax.random.split(key, 40))
    f32 = jnp.float32

    def nrm(shape, scale=1.0):
        return jax.random.normal(next(ks), shape, f32) * scale

    def gain(shape):
        return 1.0 + 0.05 * jax.random.normal(next(ks), shape, f32)

    swa_buf = min(WINDOW, PAST_LEN)
    qkv_w = (N_HEADS + 2 * N_KV_HEADS) * HEAD_DIM
    xw = X_HEADS * X_HEAD_DIM
    return {
        'x_prompt': nrm((BATCH, SEQ, D_MODEL)),
        'x_sample': nrm((DEC_BATCH, DEC_SEQ, D_MODEL)),
        'state_pool': nrm((N_POOL_LAYERS, DEC_BATCH, POOL_BUF, D_MODEL)),
        'cache_swa_k': nrm((N_ATTN_LAYERS, DEC_BATCH, swa_buf, N_KV_HEADS, HEAD_DIM)),
        'cache_swa_v': nrm((N_ATTN_LAYERS, DEC_BATCH, swa_buf, N_KV_HEADS, HEAD_DIM)),
        'cache_mem_k': nrm((DEPTH, DEC_BATCH, N_MEM, X_HEADS, X_HEAD_DIM)),
        'cache_mem_v': nrm((DEPTH, DEC_BATCH, N_MEM, X_HEADS, X_HEAD_DIM)),
        'mem_prompt': nrm((BATCH, N_MEM, D_MODEL)),
        'g_ffn1': gain((DEPTH, D_MODEL)),
        'w_ffn1_gu': nrm((DEPTH, D_MODEL, 2 * D_FF), D_MODEL ** -0.5),
        'w_ffn1_dn': nrm((DEPTH, D_FF, D_MODEL), D_FF ** -0.5),
        'g_mix': gain((DEPTH, D_MODEL)),
        'w_pool': nrm((N_POOL_LAYERS, POOL_GROUPS, POOL_GC, POOL_GC), POOL_GC ** -0.5),
        'pool_scale': 1.0 + 0.1 * nrm((N_POOL_LAYERS, D_MODEL)),
        'w_qkv': nrm((N_ATTN_LAYERS, D_MODEL, qkv_w), D_MODEL ** -0.5),
        'w_o': nrm((N_ATTN_LAYERS, N_HEADS * HEAD_DIM, D_MODEL), (N_HEADS * HEAD_DIM) ** -0.5),
        'sinks': nrm((N_ATTN_LAYERS, N_HEADS)),
        'g_xq': gain((DEPTH, D_MODEL)),
        'g_mem': gain((DEPTH, D_MODEL)),
        'w_xq': nrm((DEPTH, D_MODEL, xw), D_MODEL ** -0.5),
        'w_xkv': nrm((DEPTH, D_MODEL, 2 * xw), D_MODEL ** -0.5),
        'w_xo': nrm((DEPTH, xw, D_MODEL), xw ** -0.5),
        'g_ffn2': gain((DEPTH, D_MODEL)),
        'w_ffn2_gu': nrm((DEPTH, D_MODEL, 2 * D_FF), D_MODEL ** -0.5),
        'w_ffn2_dn': nrm((DEPTH, D_FF, D_MODEL), D_FF ** -0.5),
        'g_final': gain((D_MODEL,)),
    }


def reference(x_prompt, x_sample, state_pool, cache_swa_k, cache_swa_v, cache_mem_k, cache_mem_v, mem_prompt,
              g_ffn1, w_ffn1_gu, w_ffn1_dn, g_mix, w_pool, pool_scale, w_qkv, w_o, sinks,
              g_xq, g_mem, w_xq, w_xkv, w_xo, g_ffn2, w_ffn2_gu, w_ffn2_dn, g_final):
    xp, xs = x_prompt, x_sample
    t_p = xp.shape[1]
    pool_p, pool_s = [], []
    swa_kp, swa_vp, swa_ks, swa_vs = [], [], [], []
    mem_kp, mem_vp = [], []
    for layer in range(DEPTH):
        xp = xp + 0.5 * swiglu(rms_norm(xp, g_ffn1[layer]), w_ffn1_gu[layer], w_ffn1_dn[layer])
        xs = xs + 0.5 * swiglu(rms_norm(xs, g_ffn1[layer]), w_ffn1_gu[layer], w_ffn1_dn[layer])
        hp = rms_norm(xp, g_mix[layer])
        hs = rms_norm(xs, g_mix[layer])
        i = layer // N_MIXERS
        if layer % N_MIXERS == 0:
            xp = xp + pool_mix(hp, jnp.arange(t_p), t_p, w_pool[i], pool_scale[i])
            hs_ext = jnp.concatenate([state_pool[i], hs], axis=1)
            xs = xs + pool_mix(hs_ext, PAST_LEN - POOL_BUF + jnp.arange(hs_ext.shape[1]), hs.shape[1],
                               w_pool[i], pool_scale[i])
            pool_p.append(hp[:, t_p - POOL_BUF:])
            pool_s.append(hs_ext[:, -POOL_BUF:])
        else:
            yp, kp_new, vp_new = swa_prompt(hp, w_qkv[i], w_o[i], sinks[i])
            ys, ks_new, vs_new = swa_sample(hs, cache_swa_k[i], cache_swa_v[i], w_qkv[i], w_o[i], sinks[i])
            xp = xp + yp
            xs = xs + ys
            swa_kp.append(kp_new)
            swa_vp.append(vp_new)
            swa_ks.append(ks_new)
            swa_vs.append(vs_new)
        mk, mv = mem_kv(mem_prompt, g_mem[layer], w_xkv[layer])
        mem_kp.append(mk)
        mem_vp.append(mv)
        xp = xp + cross_attn(rms_norm(xp, g_xq[layer]), mk, mv, w_xq[layer], w_xo[layer])
        xs = xs + cross_attn(rms_norm(xs, g_xq[layer]), cache_mem_k[layer], cache_mem_v[layer],
                             w_xq[layer], w_xo[layer])
        xp = xp + 0.5 * swiglu(rms_norm(xp, g_ffn2[layer]), w_ffn2_gu[layer], w_ffn2_dn[layer])
        xs = xs + 0.5 * swiglu(rms_norm(xs, g_ffn2[layer]), w_ffn2_gu[layer], w_ffn2_dn[layer])
    y_prompt = rms_norm(xp, g_final)
    y_sample = rms_norm(xs, g_final)
    return (y_prompt, y_sample, jnp.stack(pool_p), jnp.stack(pool_s), jnp.stack(swa_kp), jnp.stack(swa_vp),
            jnp.stack(swa_ks), jnp.stack(swa_vs), jnp.stack(mem_kp), jnp.stack(mem_vp))
```

```python
import functools
import math

import jax
import jax.numpy as jnp
from jax import lax
from jax.experimental import pallas as pl
from jax.experimental.pallas import tpu as pltpu

F32 = jnp.float32
BF16 = jnp.bfloat16

RMS_EPS = 1e-6
PAST_LEN = 8192
POOL_WINDOWS = (2, 4, 8, 16)
POOL_MAXW = max(POOL_WINDOWS)
POOL_BUF = POOL_MAXW - 1
HEAD_DIM = 64
GQA_GROUP = 4
WINDOW = 128
ROPE_THETA = 10000.0
X_HEADS = 4
X_HEAD_DIM = 128
LANES = 128
V7X_VMEM_LIMIT = 60 * 1024 * 1024


def _params(sem):
    return pltpu.CompilerParams(dimension_semantics=sem, vmem_limit_bytes=V7X_VMEM_LIMIT)


def _pick(n, candidates):
    for c in candidates:
        if n % c == 0:
            return c
    raise ValueError(f"no block size in {candidates} divides {n}")


def _rms(x, g):
    return x * lax.rsqrt(jnp.mean(x * x, axis=-1, keepdims=True) + RMS_EPS) * g


def _norm_body(x_ref, g_ref, o_ref):
    o_ref[...] = _rms(x_ref[...], g_ref[...]).astype(o_ref.dtype)


def rms_norm_rows(x, g, out_dtype):
    m, d = x.shape
    bm = _pick(m, (832, 640, 512, 256, 128, 64, 32, 16, 8))
    return pl.pallas_call(
        _norm_body,
        grid=(m // bm,),
        in_specs=[pl.BlockSpec((bm, d), lambda i: (i, 0)), pl.BlockSpec((1, d), lambda i: (0, 0))],
        out_specs=pl.BlockSpec((bm, d), lambda i: (i, 0)),
        out_shape=jax.ShapeDtypeStruct((m, d), out_dtype),
        compiler_params=_params(("arbitrary",)),
        name="rms_norm_rows",
    )(x, g.reshape(1, d))


def _gate_up_body(h_ref, wg_ref, wu_ref, o_ref, wb_ref, *, bf):
    @pl.when(pl.program_id(1) == 0)
    def _():
        wb_ref[:, :bf] = wg_ref[...].astype(BF16)
        wb_ref[:, bf:] = wu_ref[...].astype(BF16)

    r = jnp.dot(h_ref[...], wb_ref[...], preferred_element_type=F32)
    a = r[:, :bf]
    b = r[:, bf:]
    o_ref[...] = (a / (1.0 + jnp.exp(-a)) * b * 0.5).astype(o_ref.dtype)


def gate_up(h, w_gu):
    m, d = h.shape
    f = w_gu.shape[1] // 2
    bf = _pick(f, (512, 256, 128))
    bm = _pick(m, (1664, 1024, 512, 256, 128, 64, 32, 16))
    nf = f // bf
    return pl.pallas_call(
        functools.partial(_gate_up_body, bf=bf),
        grid=(nf, m // bm),
        in_specs=[
            pl.BlockSpec((bm, d), lambda j, i: (i, 0)),
            pl.BlockSpec((d, bf), lambda j, i: (0, j)),
            pl.BlockSpec((d, bf), lambda j, i: (0, j + nf)),
        ],
        out_specs=pl.BlockSpec((bm, bf), lambda j, i: (i, j)),
        out_shape=jax.ShapeDtypeStruct((m, f), BF16),
        scratch_shapes=[pltpu.VMEM((d, 2 * bf), BF16)],
        compiler_params=_params(("arbitrary", "arbitrary")),
        name="gate_up",
    )(h, w_gu, w_gu)


def _mm_res_body(a_ref, w_ref, x_ref, o_ref, wb_ref):
    @pl.when(pl.program_id(1) == 0)
    def _():
        wb_ref[...] = w_ref[...].astype(BF16)

    o_ref[...] = x_ref[...] + jnp.dot(a_ref[...], wb_ref[...], preferred_element_type=F32)


def matmul_residual(a, w, x):
    m, k = a.shape
    n = w.shape[1]
    bn = _pick(n, (512, 256, 128))
    bm = _pick(m, (832, 512, 256, 128, 64, 32, 16))
    return pl.pallas_call(
        _mm_res_body,
        grid=(n // bn, m // bm),
        in_specs=[
            pl.BlockSpec((bm, k), lambda j, i: (i, 0)),
            pl.BlockSpec((k, bn), lambda j, i: (0, j), pipeline_mode=pl.Buffered(1)),
            pl.BlockSpec((bm, bn), lambda j, i: (i, j)),
        ],
        out_specs=pl.BlockSpec((bm, bn), lambda j, i: (i, j)),
        out_shape=jax.ShapeDtypeStruct((m, n), F32),
        scratch_shapes=[pltpu.VMEM((k, bn), BF16)],
        input_output_aliases={2: 0},
        compiler_params=_params(("arbitrary", "arbitrary")),
        name="matmul_residual",
    )(a, w, x)


def _qkv_body(h_ref, w_ref, c_ref, s_ref, o_ref, wb_ref, *, n_rope_tiles):
    @pl.when(pl.program_id(1) == 0)
    def _():
        wb_ref[...] = w_ref[...].astype(BF16)

    r = jnp.dot(h_ref[...], wb_ref[...], preferred_element_type=F32)
    bm, bn = r.shape

    @pl.when(pl.program_id(0) < n_rope_tiles)
    def _():
        c = c_ref[...]
        s = s_ref[...]
        lane = lax.broadcasted_iota(jnp.int32, (bm, LANES), 1)
        first_half = (lane % HEAD_DIM) < (HEAD_DIM // 2)
        for ci in range(bn // LANES):
            blk = r[:, ci * LANES:(ci + 1) * LANES]
            partner = jnp.where(first_half,
                                pltpu.roll(blk, LANES - HEAD_DIM // 2, 1),
                                pltpu.roll(blk, HEAD_DIM // 2, 1))
            o_ref[:, ci * LANES:(ci + 1) * LANES] = blk * c + partner * s

    @pl.when(pl.program_id(0) >= n_rope_tiles)
    def _():
        o_ref[...] = r


def qkv_rope(h, w_qkv, cos_t, sin_t, n_rope_cols, bn):
    m, d = h.shape
    n = w_qkv.shape[1]
    assert n % bn == 0 and n_rope_cols % bn == 0 and bn % LANES == 0
    bm = _pick(m, (832, 512, 256, 128, 64, 32, 16))
    return pl.pallas_call(
        functools.partial(_qkv_body, n_rope_tiles=n_rope_cols // bn),
        grid=(n // bn, m // bm),
        in_specs=[
            pl.BlockSpec((bm, d), lambda j, i: (i, 0)),
            pl.BlockSpec((d, bn), lambda j, i: (0, j)),
            pl.BlockSpec((bm, LANES), lambda j, i: (i, 0)),
            pl.BlockSpec((bm, LANES), lambda j, i: (i, 0)),
        ],
        out_specs=pl.BlockSpec((bm, bn), lambda j, i: (i, j)),
        out_shape=jax.ShapeDtypeStruct((m, n), F32),
        scratch_shapes=[pltpu.VMEM((d, bn), BF16)],
        compiler_params=_params(("arbitrary", "arbitrary")),
        name="qkv_rope",
    )(h, w_qkv, cos_t, sin_t)


def rope_tables(t_prompt, n_sample):
    half = HEAD_DIM // 2
    inv = ROPE_THETA ** (-jnp.arange(half, dtype=F32) / half)
    pos = jnp.concatenate([jnp.arange(t_prompt), jnp.full((n_sample,), PAST_LEN)]).astype(F32)
    ang = pos[:, None] * inv[None, :]
    cos, sin = jnp.cos(ang), jnp.sin(ang)
    reps = LANES // HEAD_DIM
    return (jnp.tile(jnp.concatenate([cos, cos], axis=1), (1, reps)),
            jnp.tile(jnp.concatenate([-sin, sin], axis=1), (1, reps)))


def _swa_prompt_body(sink_ref, q_ref, kp_ref, kc_ref, vp_ref, vc_ref, o_ref, *, n_kv):
    n = pl.program_id(0)
    blk = WINDOW
    rows = lax.broadcasted_iota(jnp.int32, (GQA_GROUP * blk, 2 * blk), 0) % blk
    cols = lax.broadcasted_iota(jnp.int32, (GQA_GROUP * blk, 2 * blk), 1)
    diff = rows + blk - cols
    mask = (diff >= 0) & (diff < WINDOW) & ((cols >= blk) | (n > 0))
    head_of_row = lax.broadcasted_iota(jnp.int32, (GQA_GROUP * blk, 1), 0) // blk
    for kv in range(n_kv):
        ks = slice(kv * HEAD_DIM, (kv + 1) * HEAD_DIM)
        k2 = jnp.concatenate([kp_ref[:, ks], kc_ref[:, ks]], axis=0).astype(BF16)
        v2 = jnp.concatenate([vp_ref[:, ks], vc_ref[:, ks]], axis=0).astype(BF16)
        qs = jnp.concatenate(
            [q_ref[:, (kv * GQA_GROUP + g) * HEAD_DIM:(kv * GQA_GROUP + g + 1) * HEAD_DIM] for g in range(GQA_GROUP)],
            axis=0).astype(BF16)
        s = lax.dot_general(qs, k2, (((1,), (1,)), ((), ())), preferred_element_type=F32) * (1.0 / math.sqrt(HEAD_DIM))
        s = jnp.where(mask, s, -jnp.inf)
        sink = jnp.zeros((GQA_GROUP * blk, 1), F32)
        for g in range(GQA_GROUP):
            sink = jnp.where(head_of_row == g, sink_ref[kv * GQA_GROUP + g], sink)
        mx = jnp.maximum(jnp.max(s, axis=-1, keepdims=True), sink)
        p = jnp.exp(s - mx)
        denom = jnp.sum(p, axis=-1, keepdims=True) + jnp.exp(sink - mx)
        o = jnp.dot(p.astype(BF16), v2, preferred_element_type=F32) / denom
        for g in range(GQA_GROUP):
            c0 = (kv * GQA_GROUP + g) * HEAD_DIM
            o_ref[:, c0:c0 + HEAD_DIM] = o[g * blk:(g + 1) * blk].astype(o_ref.dtype)


def swa_prompt(qkv, sinks, t, d):
    n_kv = d // HEAD_DIM // GQA_GROUP
    kvw = n_kv * HEAD_DIM
    assert d % kvw == 0 and t % WINDOW == 0
    kblk = d // kvw
    vblk = kblk + 1
    prev = lambda n: jnp.maximum(n - 1, 0)
    return pl.pallas_call(
        functools.partial(_swa_prompt_body, n_kv=n_kv),
        grid=(t // WINDOW,),
        in_specs=[
            pl.BlockSpec(memory_space=pltpu.SMEM),
            pl.BlockSpec((WINDOW, d), lambda n: (n, 0)),
            pl.BlockSpec((WINDOW, kvw), lambda n: (prev(n), kblk)),
            pl.BlockSpec((WINDOW, kvw), lambda n: (n, kblk)),
            pl.BlockSpec((WINDOW, kvw), lambda n: (prev(n), vblk)),
            pl.BlockSpec((WINDOW, kvw), lambda n: (n, vblk)),
        ],
        out_specs=pl.BlockSpec((WINDOW, d), lambda n: (n, 0)),
        out_shape=jax.ShapeDtypeStruct((t, d), BF16),
        compiler_params=_params(("arbitrary",)),
        name="swa_prompt",
    )(sinks, qkv, qkv, qkv, qkv, qkv)


def _swa_sample_body(q_ref, k_ref, v_ref, sink_ref, e_ref, et_ref, o_ref, *, n_kv):
    bt, nk, kvw = k_ref.shape
    q = q_ref[...]
    k = k_ref[...]
    s = jnp.zeros((bt * nk, LANES), F32)
    for g in range(GQA_GROUP):
        qg = jnp.concatenate(
            [q[:, (kv * GQA_GROUP + g) * HEAD_DIM:(kv * GQA_GROUP + g + 1) * HEAD_DIM] for kv in range(n_kv)], axis=1)
        prod = (k * qg[:, None, :]).reshape(bt * nk, kvw).astype(BF16)
        s = s + jnp.dot(prod, e_ref[g], preferred_element_type=F32)
    s = (s * (1.0 / math.sqrt(HEAD_DIM))).reshape(bt, nk, LANES)
    sink = sink_ref[...].reshape(1, 1, LANES)
    mx = jnp.maximum(jnp.max(s, axis=1, keepdims=True), sink)
    p = jnp.exp(s - mx)
    denom = jnp.sum(p, axis=1, keepdims=True) + jnp.exp(sink - mx)
    pn = (p / denom).astype(BF16).reshape(bt * nk, LANES)
    v = v_ref[...]
    for g in range(GQA_GROUP):
        pe = jnp.dot(pn, et_ref[g], preferred_element_type=F32).reshape(bt, nk, kvw)
        og = jnp.sum(pe * v, axis=1)
        for kv in range(n_kv):
            c0 = (kv * GQA_GROUP + g) * HEAD_DIM
            o_ref[:, c0:c0 + HEAD_DIM] = og[:, kv * HEAD_DIM:(kv + 1) * HEAD_DIM].astype(o_ref.dtype)


def swa_sample(q, k_all, v_all, sinks):
    b, d = q.shape
    _, nk, kvw = k_all.shape
    n_kv = kvw // HEAD_DIM
    bt = _pick(b, (16, 8, 4, 2, 1))
    lane_kv = jnp.arange(kvw) // HEAD_DIM
    col = jnp.arange(LANES)
    e = jnp.stack([(lane_kv[:, None] + g * n_kv == col[None, :]) for g in range(GQA_GROUP)]).astype(BF16)
    et = jnp.swapaxes(e, 1, 2)
    sink_l = jnp.zeros((LANES,), F32).at[:GQA_GROUP * n_kv].set(
        sinks.astype(F32).reshape(n_kv, GQA_GROUP).T.reshape(-1)).reshape(1, LANES)
    return pl.pallas_call(
        functools.partial(_swa_sample_body, n_kv=n_kv),
        grid=(b // bt,),
        in_specs=[
            pl.BlockSpec((bt, d), lambda i: (i, 0)),
            pl.BlockSpec((bt, nk, kvw), lambda i: (i, 0, 0)),
            pl.BlockSpec((bt, nk, kvw), lambda i: (i, 0, 0)),
            pl.BlockSpec((1, LANES), lambda i: (0, 0)),
            pl.BlockSpec((GQA_GROUP, kvw, LANES), lambda i: (0, 0, 0)),
            pl.BlockSpec((GQA_GROUP, LANES, kvw), lambda i: (0, 0, 0)),
        ],
        out_specs=pl.BlockSpec((bt, d), lambda i: (i, 0)),
        out_shape=jax.ShapeDtypeStruct((b, d), BF16),
        compiler_params=_params(("arbitrary",)),
        name="swa_sample",
    )(q, k_all, v_all, sink_l, e, et)


def _pool_prompt_body(x_ref, g_ref, w_ref, sc_ref, o_ref, hl_ref, wb_ref, *, gc):
    i = pl.program_id(0)

    @pl.when(i == 0)
    def _():
        wb_ref[...] = w_ref[...].astype(BF16)
        hl_ref[...] = jnp.zeros_like(hl_ref)

    bm = x_ref.shape[0]
    x = x_ref[...]
    h = _rms(x, g_ref[...])
    hprev = hl_ref[...]
    hl_ref[...] = h[bm - POOL_MAXW:, :]
    pos1 = (lax.broadcasted_iota(jnp.int32, (bm, 1), 0) + i * bm + 1).astype(F32)
    for gi, w in enumerate(POOL_WINDOWS):
        cs = slice(gi * gc, (gi + 1) * gc)
        hg = h[:, cs]
        acc = jnp.concatenate([hprev[:, cs], hg], axis=0)
        step = 1
        while step < w:
            acc = acc + pltpu.roll(acc, step, 0)
            step *= 2
        cnt = jnp.minimum(jnp.float32(w), pos1)
        pooled = (acc[POOL_MAXW:, :] / cnt - hg).astype(BF16)
        y = jnp.dot(pooled, wb_ref[gi], preferred_element_type=F32)
        o_ref[:, cs] = x[:, cs] + y * sc_ref[:, cs]


def pool_prompt(x_all, g, w_pool, scale, t):
    d = x_all.shape[1]
    ng, gc, _ = w_pool.shape
    bm = _pick(t, (512, 256, 128, 64, 32, 16))
    return pl.pallas_call(
        functools.partial(_pool_prompt_body, gc=gc),
        grid=(t // bm,),
        in_specs=[
            pl.BlockSpec((bm, d), lambda i: (i, 0)),
            pl.BlockSpec((1, d), lambda i: (0, 0)),
            pl.BlockSpec((ng, gc, gc), lambda i: (0, 0, 0)),
            pl.BlockSpec((1, d), lambda i: (0, 0)),
        ],
        out_specs=[
            pl.BlockSpec((bm, d), lambda i: (i, 0)),
            pl.BlockSpec((POOL_MAXW, d), lambda i: (0, 0)),
        ],
        out_shape=[jax.ShapeDtypeStruct(x_all.shape, F32), jax.ShapeDtypeStruct((POOL_MAXW, d), F32)],
        scratch_shapes=[pltpu.VMEM((ng, gc, gc), BF16)],
        input_output_aliases={0: 0},
        compiler_params=_params(("arbitrary",)),
        name="pool_prompt",
    )(x_all, g.reshape(1, d), w_pool, scale.reshape(1, d))


def _pool_sample_body(x_ref, st_ref, g_ref, w_ref, sc_ref, o_ref, h_ref, *, gc):
    x = x_ref[...]
    h = _rms(x, g_ref[...])
    h_ref[...] = h
    for gi, w in enumerate(POOL_WINDOWS):
        cs = slice(gi * gc, (gi + 1) * gc)
        hg = h[:, cs]
        tot = hg + jnp.sum(st_ref[:, POOL_BUF - (w - 1):, cs], axis=1)
        cnt = float(min(w, PAST_LEN + 1))
        pooled = (tot / cnt - hg).astype(BF16)
        y = jnp.dot(pooled, w_ref[gi].astype(BF16), preferred_element_type=F32)
        o_ref[:, cs] = x[:, cs] + y * sc_ref[:, cs]


def pool_sample(x_all, state, g, w_pool, scale, t):
    d = x_all.shape[1]
    b = state.shape[0]
    ng, gc, _ = w_pool.shape
    bt = _pick(b, (32, 16, 8))
    assert t % bt == 0
    off = t // bt
    return pl.pallas_call(
        functools.partial(_pool_sample_body, gc=gc),
        grid=(b // bt,),
        in_specs=[
            pl.BlockSpec((bt, d), lambda i: (off + i, 0)),
            pl.BlockSpec((bt, POOL_BUF, d), lambda i: (i, 0, 0)),
            pl.BlockSpec((1, d), lambda i: (0, 0)),
            pl.BlockSpec((ng, gc, gc), lambda i: (0, 0, 0)),
            pl.BlockSpec((1, d), lambda i: (0, 0)),
        ],
        out_specs=[
            pl.BlockSpec((bt, d), lambda i: (off + i, 0)),
            pl.BlockSpec((bt, d), lambda i: (i, 0)),
        ],
        out_shape=[jax.ShapeDtypeStruct(x_all.shape, F32), jax.ShapeDtypeStruct((b, d), F32)],
        input_output_aliases={0: 0},
        compiler_params=_params(("arbitrary",)),
        name="pool_sample",
    )(x_all, state, g.reshape(1, d), w_pool, scale.reshape(1, d))


def _mem_kv_body(m_ref, g_ref, w_ref, o_ref):
    h = _rms(m_ref[...], g_ref[...]).astype(BF16)
    o_ref[...] = jnp.dot(h, w_ref[...].astype(BF16), preferred_element_type=F32)


def mem_kv(mem, g, w_kv):
    m, d = mem.shape
    n = w_kv.shape[1]
    bn = _pick(n, (512, 256, 128))
    return pl.pallas_call(
        _mem_kv_body,
        grid=(n // bn,),
        in_specs=[
            pl.BlockSpec((m, d), lambda j: (0, 0)),
            pl.BlockSpec((1, d), lambda j: (0, 0)),
            pl.BlockSpec((d, bn), lambda j: (0, j)),
        ],
        out_specs=pl.BlockSpec((m, bn), lambda j: (0, j)),
        out_shape=jax.ShapeDtypeStruct((m, n), F32),
        compiler_params=_params(("arbitrary",)),
        name="mem_kv",
    )(mem, g.reshape(1, d), w_kv)


def _xattn_prompt_body(h_ref, wq_ref, kv_ref, wo_ref, x_ref, o_ref, wqb_ref, wob_ref, kvb_ref):
    @pl.when(pl.program_id(0) == 0)
    def _():
        wqb_ref[...] = wq_ref[...].astype(BF16)
        wob_ref[...] = wo_ref[...].astype(BF16)
        kvb_ref[...] = kv_ref[...].astype(BF16)

    xw = X_HEADS * X_HEAD_DIM
    q = jnp.dot(h_ref[...], wqb_ref[...], preferred_element_type=F32) / math.sqrt(X_HEAD_DIM)
    outs = []
    for hd in range(X_HEADS):
        cs = slice(hd * X_HEAD_DIM, (hd + 1) * X_HEAD_DIM)
        s = lax.dot_general(q[:, cs].astype(BF16), kvb_ref[:, cs], (((1,), (1,)), ((), ())),
                            preferred_element_type=F32)
        p = jnp.exp(s - jnp.max(s, axis=-1, keepdims=True))
        l = jnp.sum(p, axis=-1, keepdims=True)
        vs = slice(xw + hd * X_HEAD_DIM, xw + (hd + 1) * X_HEAD_DIM)
        outs.append(jnp.dot(p.astype(BF16), kvb_ref[:, vs], preferred_element_type=F32) / l)
    o = jnp.concatenate(outs, axis=1).astype(BF16)
    o_ref[...] = x_ref[...] + jnp.dot(o, wob_ref[...], preferred_element_type=F32)


def xattn_prompt(hq_all, w_q, mkv, w_o, x_all, t):
    d = x_all.shape[1]
    xw = w_q.shape[1]
    nm = mkv.shape[0]
    bm = _pick(t, (512, 256, 128, 64, 32, 16))
    return pl.pallas_call(
        _xattn_prompt_body,
        grid=(t // bm,),
        in_specs=[
            pl.BlockSpec((bm, d), lambda i: (i, 0)),
            pl.BlockSpec((d, xw), lambda i: (0, 0)),
            pl.BlockSpec((nm, 2 * xw), lambda i: (0, 0)),
            pl.BlockSpec((xw, d), lambda i: (0, 0)),
            pl.BlockSpec((bm, d), lambda i: (i, 0)),
        ],
        out_specs=pl.BlockSpec((bm, d), lambda i: (i, 0)),
        out_shape=jax.ShapeDtypeStruct(x_all.shape, F32),
        scratch_shapes=[pltpu.VMEM((d, xw), BF16), pltpu.VMEM((xw, d), BF16), pltpu.VMEM((nm, 2 * xw), BF16)],
        input_output_aliases={4: 0},
        compiler_params=_params(("arbitrary",)),
        name="xattn_prompt",
    )(hq_all, w_q, mkv, w_o, x_all)


def _xattn_sample_body(h_ref, wq_ref, k_ref, v_ref, wo_ref, x_ref, e_ref, et_ref, o_ref):
    bt, nm, xw = k_ref.shape
    q = jnp.dot(h_ref[...], wq_ref[...].astype(BF16), preferred_element_type=F32) / math.sqrt(X_HEAD_DIM)
    prod = (k_ref[...] * q[:, None, :]).reshape(bt * nm, xw).astype(BF16)
    s = jnp.dot(prod, e_ref[...], preferred_element_type=F32).reshape(bt, nm, LANES)
    p = jnp.exp(s - jnp.max(s, axis=1, keepdims=True))
    pn = (p / jnp.sum(p, axis=1, keepdims=True)).astype(BF16).reshape(bt * nm, LANES)
    pe = jnp.dot(pn, et_ref[...], preferred_element_type=F32).reshape(bt, nm, xw)
    o = jnp.sum(pe * v_ref[...], axis=1).astype(BF16)
    o_ref[...] = x_ref[...] + jnp.dot(o, wo_ref[...].astype(BF16), preferred_element_type=F32)


def xattn_sample(hq_all, w_q, mk, mv, w_o, x_all, t):
    d = x_all.shape[1]
    b, nm, xw = mk.shape
    bt = _pick(b, (8, 4, 2, 1))
    assert t % bt == 0
    off = t // bt
    lane_h = jnp.arange(xw) // X_HEAD_DIM
    e = (lane_h[:, None] == jnp.arange(LANES)[None, :]).astype(BF16)
    return pl.pallas_call(
        _xattn_sample_body,
        grid=(b // bt,),
        in_specs=[
            pl.BlockSpec((bt, d), lambda i: (off + i, 0)),
            pl.BlockSpec((d, xw), lambda i: (0, 0)),
            pl.BlockSpec((bt, nm, xw), lambda i: (i, 0, 0)),
            pl.BlockSpec((bt, nm, xw), lambda i: (i, 0, 0)),
            pl.BlockSpec((xw, d), lambda i: (0, 0)),
            pl.BlockSpec((bt, d), lambda i: (off + i, 0)),
            pl.BlockSpec((xw, LANES), lambda i: (0, 0)),
            pl.BlockSpec((LANES, xw), lambda i: (0, 0)),
        ],
        out_specs=pl.BlockSpec((bt, d), lambda i: (off + i, 0)),
        out_shape=jax.ShapeDtypeStruct(x_all.shape, F32),
        input_output_aliases={5: 0},
        compiler_params=_params(("arbitrary",)),
        name="xattn_sample",
    )(hq_all, w_q, mk, mv, w_o, x_all, e, e.T)


def _ffn(x, g, w_gu, w_dn):
    h = rms_norm_rows(x, g, BF16)
    return matmul_residual(gate_up(h, w_gu), w_dn, x)


def kernel(x_prompt, x_sample, state_pool, cache_swa_k, cache_swa_v, cache_mem_k, cache_mem_v, mem_prompt,
           g_ffn1, w_ffn1_gu, w_ffn1_dn, g_mix, w_pool, pool_scale, w_qkv, w_o, sinks,
           g_xq, g_mem, w_xq, w_xkv, w_xo, g_ffn2, w_ffn2_gu, w_ffn2_dn, g_final):
    bp, t, d = x_prompt.shape
    b, s_len, _ = x_sample.shape
    assert bp == 1 and s_len == 1
    depth = g_ffn1.shape[0]
    xw = w_xq.shape[2]
    nm = mem_prompt.shape[1]
    n_kv = d // HEAD_DIM // GQA_GROUP
    kvw = n_kv * HEAD_DIM

    x = jnp.concatenate([x_prompt[0], x_sample[:, 0]], axis=0)
    cos_t, sin_t = rope_tables(t, b)

    pool_p, pool_s = [], []
    swa_kp, swa_vp, swa_ks, swa_vs = [], [], [], []
    mem_kp, mem_vp = [], []
    for layer in range(depth):
        x = _ffn(x, g_ffn1[layer], w_ffn1_gu[layer], w_ffn1_dn[layer])
        i = layer // 2
        if layer % 2 == 0:
            x, h_last = pool_prompt(x, g_mix[layer], w_pool[i], pool_scale[i], t)
            x, hs = pool_sample(x, state_pool[i], g_mix[layer], w_pool[i], pool_scale[i], t)
            pool_p.append(h_last[None, POOL_MAXW - POOL_BUF:])
            pool_s.append(jnp.concatenate([state_pool[i][:, 1:], hs[:, None]], axis=1))
        else:
            hm = rms_norm_rows(x, g_mix[layer], BF16)
            qkv = qkv_rope(hm, w_qkv[i], cos_t, sin_t, d + kvw, kvw)
            buf = cache_swa_k.shape[2]
            keep = min(WINDOW, t)
            swa_kp.append(qkv[t - keep:t, d:d + kvw].reshape(1, keep, n_kv, HEAD_DIM))
            swa_vp.append(qkv[t - keep:t, d + kvw:].reshape(1, keep, n_kv, HEAD_DIM))
            k_new = qkv[t:, d:d + kvw].reshape(b, 1, n_kv, HEAD_DIM)
            v_new = qkv[t:, d + kvw:].reshape(b, 1, n_kv, HEAD_DIM)
            ks_all = jnp.concatenate([cache_swa_k[i], k_new], axis=1)[:, -buf:]
            vs_all = jnp.concatenate([cache_swa_v[i], v_new], axis=1)[:, -buf:]
            swa_ks.append(ks_all)
            swa_vs.append(vs_all)
            assert buf == WINDOW
            o_p = swa_prompt(qkv, sinks[i], t, d)
            o_s = swa_sample(qkv[t:, :d], ks_all.reshape(b, buf, kvw), vs_all.reshape(b, buf, kvw), sinks[i])
            x = matmul_residual(jnp.concatenate([o_p, o_s], axis=0), w_o[i], x)
        mkv = mem_kv(mem_prompt[0], g_mem[layer], w_xkv[layer])
        mem_kp.append(mkv[:, :xw].reshape(1, nm, X_HEADS, X_HEAD_DIM))
        mem_vp.append(mkv[:, xw:].reshape(1, nm, X_HEADS, X_HEAD_DIM))
        hq = rms_norm_rows(x, g_xq[layer], BF16)
        x = xattn_prompt(hq, w_xq[layer], mkv, w_xo[layer], x, t)
        x = xattn_sample(hq, w_xq[layer], cache_mem_k[layer].reshape(b, nm, xw),
                         cache_mem_v[layer].reshape(b, nm, xw), w_xo[layer], x, t)
        x = _ffn(x, g_ffn2[layer], w_ffn2_gu[layer], w_ffn2_dn[layer])
    y = rms_norm_rows(x, g_final, F32)
    return (y[None, :t], y[t:, None], jnp.stack(pool_p), jnp.stack(pool_s), jnp.stack(swa_kp), jnp.stack(swa_vp),
            jnp.stack(swa_ks), jnp.stack(swa_vs), jnp.stack(mem_kp), jnp.stack(mem_vp))
```

```python
import functools
import math

import jax
import jax.numpy as jnp
from jax import lax
from jax.experimental import pallas as pl
from jax.experimental.pallas import tpu as pltpu

F32 = jnp.float32
BF16 = jnp.bfloat16

RMS_EPS = 1e-6
PAST_LEN = 8192
POOL_WINDOWS = (2, 4, 8, 16)
POOL_MAXW = max(POOL_WINDOWS)
POOL_BUF = POOL_MAXW - 1
HEAD_DIM = 64
GQA_GROUP = 4
WINDOW = 128
ROPE_THETA = 10000.0
X_HEADS = 4
X_HEAD_DIM = 128
LANES = 128
V7X_VMEM_LIMIT = 60 * 1024 * 1024


def _params(sem):
    return pltpu.CompilerParams(dimension_semantics=sem, vmem_limit_bytes=V7X_VMEM_LIMIT)


def _pick(n, candidates):
    for c in candidates:
        if n % c == 0:
            return c
    raise ValueError(f"no block size in {candidates} divides {n}")


def _rms(x, g):
    return x * lax.rsqrt(jnp.mean(x * x, axis=-1, keepdims=True) + RMS_EPS) * g


def _norm_body(x_ref, g_ref, o_ref):
    o_ref[...] = _rms(x_ref[...], g_ref[...]).astype(o_ref.dtype)


def rms_norm_rows(x, g, out_dtype):
    m, d = x.shape
    bm = _pick(m, (832, 640, 512, 256, 128, 64, 32, 16, 8))
    return pl.pallas_call(
        _norm_body,
        grid=(m // bm,),
        in_specs=[pl.BlockSpec((bm, d), lambda i: (i, 0)), pl.BlockSpec((1, d), lambda i: (0, 0))],
        out_specs=pl.BlockSpec((bm, d), lambda i: (i, 0)),
        out_shape=jax.ShapeDtypeStruct((m, d), out_dtype),
        compiler_params=_params(("arbitrary",)),
        name="rms_norm_rows",
    )(x, g.reshape(1, d))


def _gate_up_body(h_ref, wg_ref, wu_ref, wd_ref, o_ref, wdb_ref, wb_ref, *, bf):
    @pl.when(pl.program_id(1) == 0)
    def _():
        wb_ref[:, :bf] = wg_ref[...].astype(BF16)
        wb_ref[:, bf:] = wu_ref[...].astype(BF16)
        wdb_ref[...] = wd_ref[...].astype(BF16)

    r = jnp.dot(h_ref[...], wb_ref[...], preferred_element_type=F32)
    a = r[:, :bf]
    b = r[:, bf:]
    o_ref[...] = (a / (1.0 + jnp.exp(-a)) * b * 0.5).astype(o_ref.dtype)


def gate_up(h, w_gu, w_dn, layer):
    m, d = h.shape
    f = w_gu.shape[2] // 2
    bf = _pick(f, (512, 256, 128))
    bm = _pick(m, (1664, 1024, 512, 256, 128, 64, 32, 16))
    nf = f // bf
    return pl.pallas_call(
        functools.partial(_gate_up_body, bf=bf),
        grid=(nf, m // bm),
        in_specs=[
            pl.BlockSpec((bm, d), lambda j, i: (i, 0)),
            pl.BlockSpec((None, d, bf), lambda j, i: (layer, 0, j)),
            pl.BlockSpec((None, d, bf), lambda j, i: (layer, 0, j + nf)),
            pl.BlockSpec((None, bf, d), lambda j, i: (layer, j, 0)),
        ],
        out_specs=[
            pl.BlockSpec((bm, bf), lambda j, i: (i, j)),
            pl.BlockSpec((bf, d), lambda j, i: (j, 0)),
        ],
        out_shape=[jax.ShapeDtypeStruct((m, f), BF16), jax.ShapeDtypeStruct((f, d), BF16)],
        scratch_shapes=[pltpu.VMEM((d, 2 * bf), BF16)],
        compiler_params=_params(("arbitrary", "arbitrary")),
        name="gate_up",
    )(h, w_gu, w_gu, w_dn)


def _down_body(a_ref, w_ref, x_ref, g_ref, *out_refs, emit_x):
    xn = x_ref[...] + jnp.dot(a_ref[...], w_ref[...], preferred_element_type=F32)
    if emit_x:
        out_refs[0][...] = xn
    h_ref = out_refs[-1]
    h_ref[...] = _rms(xn, g_ref[...]).astype(h_ref.dtype)


def down_norm(a, w_bf16, x, g_next, h_dtype, emit_x=True):
    m, k = a.shape
    d = w_bf16.shape[1]
    bm = _pick(m, (320, 256, 128, 64, 32, 16))
    row = lambda i: (i, 0)
    out_specs = [pl.BlockSpec((bm, d), row)]
    out_shape = [jax.ShapeDtypeStruct((m, d), h_dtype)]
    if emit_x:
        out_specs.insert(0, pl.BlockSpec((bm, d), row))
        out_shape.insert(0, jax.ShapeDtypeStruct((m, d), F32))
    return pl.pallas_call(
        functools.partial(_down_body, emit_x=emit_x),
        grid=(m // bm,),
        in_specs=[
            pl.BlockSpec((bm, k), row),
            pl.BlockSpec((k, d), lambda i: (0, 0), pipeline_mode=pl.Buffered(1)),
            pl.BlockSpec((bm, d), row),
            pl.BlockSpec((1, d), lambda i: (0, 0)),
        ],
        out_specs=out_specs,
        out_shape=out_shape,
        input_output_aliases={2: 0} if emit_x else {},
        compiler_params=_params(("arbitrary",)),
        name="down_norm",
    )(a, w_bf16, x, g_next.reshape(1, d))


def _proj_res_body(a_ref, w_ref, x_ref, g_ref, xo_ref, h_ref, wb_ref):
    @pl.when(pl.program_id(0) == 0)
    def _():
        wb_ref[...] = w_ref[...].astype(BF16)

    xn = x_ref[...] + jnp.dot(a_ref[...], wb_ref[...], preferred_element_type=F32)
    xo_ref[...] = xn
    h_ref[...] = _rms(xn, g_ref[...]).astype(h_ref.dtype)


def proj_residual_norm(a, w, layer, x, g_next):
    m, k = a.shape
    d = w.shape[2]
    bm = _pick(m, (416, 320, 256, 128, 64, 32, 16))
    row = lambda i: (i, 0)
    return pl.pallas_call(
        _proj_res_body,
        grid=(m // bm,),
        in_specs=[
            pl.BlockSpec((bm, k), row),
            pl.BlockSpec((None, k, d), lambda i: (layer, 0, 0), pipeline_mode=pl.Buffered(1)),
            pl.BlockSpec((bm, d), row),
            pl.BlockSpec((1, d), lambda i: (0, 0)),
        ],
        out_specs=[pl.BlockSpec((bm, d), row), pl.BlockSpec((bm, d), row)],
        out_shape=[jax.ShapeDtypeStruct((m, d), F32), jax.ShapeDtypeStruct((m, d), BF16)],
        scratch_shapes=[pltpu.VMEM((k, d), BF16)],
        input_output_aliases={2: 0},
        compiler_params=_params(("arbitrary",)),
        name="proj_residual_norm",
    )(a, w, x, g_next.reshape(1, d))


def _qkv_body(h_ref, w_ref, c_ref, s_ref, o_ref, wb_ref, *, n_rope_tiles):
    @pl.when(pl.program_id(1) == 0)
    def _():
        wb_ref[...] = w_ref[...].astype(BF16)

    r = jnp.dot(h_ref[...], wb_ref[...], preferred_element_type=F32)
    bm, bn = r.shape

    @pl.when(pl.program_id(0) < n_rope_tiles)
    def _():
        c = c_ref[...]
        s = s_ref[...]
        lane = lax.broadcasted_iota(jnp.int32, (bm, LANES), 1)
        first_half = (lane % HEAD_DIM) < (HEAD_DIM // 2)
        for ci in range(bn // LANES):
            blk = r[:, ci * LANES:(ci + 1) * LANES]
            partner = jnp.where(first_half,
                                pltpu.roll(blk, LANES - HEAD_DIM // 2, 1),
                                pltpu.roll(blk, HEAD_DIM // 2, 1))
            o_ref[:, ci * LANES:(ci + 1) * LANES] = blk * c + partner * s

    @pl.when(pl.program_id(0) >= n_rope_tiles)
    def _():
        o_ref[...] = r


def qkv_rope(h, w_qkv, layer, cos_t, sin_t, n_rope_cols, bn):
    m, d = h.shape
    n = w_qkv.shape[2]
    assert n % bn == 0 and n_rope_cols % bn == 0 and bn % LANES == 0
    bm = _pick(m, (832, 512, 256, 128, 64, 32, 16))
    return pl.pallas_call(
        functools.partial(_qkv_body, n_rope_tiles=n_rope_cols // bn),
        grid=(n // bn, m // bm),
        in_specs=[
            pl.BlockSpec((bm, d), lambda j, i: (i, 0)),
            pl.BlockSpec((None, d, bn), lambda j, i: (layer, 0, j)),
            pl.BlockSpec((bm, LANES), lambda j, i: (i, 0)),
            pl.BlockSpec((bm, LANES), lambda j, i: (i, 0)),
        ],
        out_specs=pl.BlockSpec((bm, bn), lambda j, i: (i, j)),
        out_shape=jax.ShapeDtypeStruct((m, n), F32),
        scratch_shapes=[pltpu.VMEM((d, bn), BF16)],
        compiler_params=_params(("arbitrary", "arbitrary")),
        name="qkv_rope",
    )(h, w_qkv, cos_t, sin_t)


def rope_tables(t_prompt, n_sample):
    half = HEAD_DIM // 2
    inv = ROPE_THETA ** (-jnp.arange(half, dtype=F32) / half)
    pos = jnp.concatenate([jnp.arange(t_prompt), jnp.full((n_sample,), PAST_LEN)]).astype(F32)
    ang = pos[:, None] * inv[None, :]
    cos, sin = jnp.cos(ang), jnp.sin(ang)
    reps = LANES // HEAD_DIM
    return (jnp.tile(jnp.concatenate([cos, cos], axis=1), (1, reps)),
            jnp.tile(jnp.concatenate([-sin, sin], axis=1), (1, reps)))


def _swa_prompt_body(sink_ref, q_ref, kp_ref, kc_ref, vp_ref, vc_ref, o_ref, *, n_kv):
    n = pl.program_id(0)
    blk = WINDOW
    nq = GQA_GROUP * blk
    keys = lax.broadcasted_iota(jnp.int32, (2 * blk, nq), 0)
    qrow = lax.broadcasted_iota(jnp.int32, (2 * blk, nq), 1) % blk
    diff = qrow + blk - keys
    mask = (diff >= 0) & (diff < WINDOW) & ((keys >= blk) | (n > 0))
    head_of_col = lax.broadcasted_iota(jnp.int32, (1, nq), 1) // blk
    log2e = math.log2(math.e)
    qscale = log2e / math.sqrt(HEAD_DIM)
    scores, values = [], []
    for kv in range(n_kv):
        ks = slice(kv * HEAD_DIM, (kv + 1) * HEAD_DIM)
        k2 = jnp.concatenate([kp_ref[:, ks], kc_ref[:, ks]], axis=0).astype(BF16)
        values.append(jnp.concatenate([vp_ref[:, ks], vc_ref[:, ks]], axis=0).astype(BF16))
        qs = jnp.concatenate(
            [q_ref[:, (kv * GQA_GROUP + g) * HEAD_DIM:(kv * GQA_GROUP + g + 1) * HEAD_DIM] * qscale
             for g in range(GQA_GROUP)], axis=0).astype(BF16)
        scores.append(lax.dot_general(k2, qs, (((1,), (1,)), ((), ())), preferred_element_type=F32))
    probs = []
    for kv in range(n_kv):
        s = jnp.where(mask, scores[kv], -jnp.inf)
        sink = jnp.zeros((1, nq), F32)
        for g in range(GQA_GROUP):
            sink = jnp.where(head_of_col == g, sink_ref[kv * GQA_GROUP + g] * log2e, sink)
        mx = jnp.maximum(jnp.max(s, axis=0, keepdims=True), sink)
        p = jnp.exp2(s - mx)
        denom = jnp.sum(p, axis=0, keepdims=True) + jnp.exp2(sink - mx)
        probs.append((p * (1.0 / denom)).astype(BF16))
    for kv in range(n_kv):
        o = lax.dot_general(probs[kv], values[kv], (((0,), (0,)), ((), ())), preferred_element_type=F32)
        for g in range(GQA_GROUP):
            c0 = (kv * GQA_GROUP + g) * HEAD_DIM
            o_ref[:, c0:c0 + HEAD_DIM] = o[g * blk:(g + 1) * blk].astype(o_ref.dtype)


def swa_prompt(qkv, sinks, t, d):
    n_kv = d // HEAD_DIM // GQA_GROUP
    kvw = n_kv * HEAD_DIM
    assert d % kvw == 0 and t % WINDOW == 0
    kblk = d // kvw
    vblk = kblk + 1
    prev = lambda n: jnp.maximum(n - 1, 0)
    return pl.pallas_call(
        functools.partial(_swa_prompt_body, n_kv=n_kv),
        grid=(t // WINDOW,),
        in_specs=[
            pl.BlockSpec(memory_space=pltpu.SMEM),
            pl.BlockSpec((WINDOW, d), lambda n: (n, 0)),
            pl.BlockSpec((WINDOW, kvw), lambda n: (prev(n), kblk)),
            pl.BlockSpec((WINDOW, kvw), lambda n: (n, kblk)),
            pl.BlockSpec((WINDOW, kvw), lambda n: (prev(n), vblk)),
            pl.BlockSpec((WINDOW, kvw), lambda n: (n, vblk)),
        ],
        out_specs=pl.BlockSpec((WINDOW, d), lambda n: (n, 0)),
        out_shape=jax.ShapeDtypeStruct((t, d), BF16),
        compiler_params=_params(("arbitrary",)),
        name="swa_prompt",
    )(sinks, qkv, qkv, qkv, qkv, qkv)


def _swa_sample_body(q_ref, k_ref, v_ref, sink_ref, e_ref, et_ref, o_ref, *, n_kv):
    bt, nk, kvw = k_ref.shape
    q = q_ref[...]
    k = k_ref[...]
    s = jnp.zeros((bt * nk, LANES), F32)
    for g in range(GQA_GROUP):
        qg = jnp.concatenate(
            [q[:, (kv * GQA_GROUP + g) * HEAD_DIM:(kv * GQA_GROUP + g + 1) * HEAD_DIM] for kv in range(n_kv)], axis=1)
        prod = (k * qg[:, None, :]).reshape(bt * nk, kvw).astype(BF16)
        s = s + jnp.dot(prod, e_ref[g], preferred_element_type=F32)
    s = (s * (1.0 / math.sqrt(HEAD_DIM))).reshape(bt, nk, LANES)
    sink = sink_ref[...].reshape(1, 1, LANES)
    mx = jnp.maximum(jnp.max(s, axis=1, keepdims=True), sink)
    p = jnp.exp(s - mx)
    denom = jnp.sum(p, axis=1, keepdims=True) + jnp.exp(sink - mx)
    pn = (p / denom).astype(BF16).reshape(bt * nk, LANES)
    v = v_ref[...]
    for g in range(GQA_GROUP):
        pe = jnp.dot(pn, et_ref[g], preferred_element_type=F32).reshape(bt, nk, kvw)
        og = jnp.sum(pe * v, axis=1)
        for kv in range(n_kv):
            c0 = (kv * GQA_GROUP + g) * HEAD_DIM
            o_ref[:, c0:c0 + HEAD_DIM] = og[:, kv * HEAD_DIM:(kv + 1) * HEAD_DIM].astype(o_ref.dtype)


def swa_sample(q, k_all, v_all, sinks):
    b, d = q.shape
    _, nk, kvw = k_all.shape
    n_kv = kvw // HEAD_DIM
    bt = _pick(b, (16, 8, 4, 2, 1))
    lane_kv = jnp.arange(kvw) // HEAD_DIM
    col = jnp.arange(LANES)
    e = jnp.stack([(lane_kv[:, None] + g * n_kv == col[None, :]) for g in range(GQA_GROUP)]).astype(BF16)
    et = jnp.swapaxes(e, 1, 2)
    sink_l = jnp.zeros((LANES,), F32).at[:GQA_GROUP * n_kv].set(
        sinks.astype(F32).reshape(n_kv, GQA_GROUP).T.reshape(-1)).reshape(1, LANES)
    return pl.pallas_call(
        functools.partial(_swa_sample_body, n_kv=n_kv),
        grid=(b // bt,),
        in_specs=[
            pl.BlockSpec((bt, d), lambda i: (i, 0)),
            pl.BlockSpec((bt, nk, kvw), lambda i: (i, 0, 0)),
            pl.BlockSpec((bt, nk, kvw), lambda i: (i, 0, 0)),
            pl.BlockSpec((1, LANES), lambda i: (0, 0)),
            pl.BlockSpec((GQA_GROUP, kvw, LANES), lambda i: (0, 0, 0)),
            pl.BlockSpec((GQA_GROUP, LANES, kvw), lambda i: (0, 0, 0)),
        ],
        out_specs=pl.BlockSpec((bt, d), lambda i: (i, 0)),
        out_shape=jax.ShapeDtypeStruct((b, d), BF16),
        compiler_params=_params(("arbitrary",)),
        name="swa_sample",
    )(q, k_all, v_all, sink_l, e, et)


def _pool_prompt_body(x_ref, g_ref, w_ref, sc_ref, gn_ref, o_ref, hn_ref, hl_ref, wb_ref, *, gc):
    i = pl.program_id(0)

    @pl.when(i == 0)
    def _():
        wb_ref[...] = w_ref[...].astype(BF16)
        hl_ref[...] = jnp.zeros_like(hl_ref)

    bm = x_ref.shape[0]
    x = x_ref[...]
    h = _rms(x, g_ref[...])
    hprev = hl_ref[...]
    hl_ref[...] = h[bm - POOL_MAXW:, :]
    pos1 = (lax.broadcasted_iota(jnp.int32, (bm, 1), 0) + i * bm + 1).astype(F32)
    for gi, w in enumerate(POOL_WINDOWS):
        cs = slice(gi * gc, (gi + 1) * gc)
        hg = h[:, cs]
        acc = jnp.concatenate([hprev[:, cs], hg], axis=0)
        step = 1
        while step < w:
            acc = acc + pltpu.roll(acc, step, 0)
            step *= 2
        cnt = jnp.minimum(jnp.float32(w), pos1)
        pooled = (acc[POOL_MAXW:, :] / cnt - hg).astype(BF16)
        y = jnp.dot(pooled, wb_ref[gi], preferred_element_type=F32)
        o_ref[:, cs] = x[:, cs] + y * sc_ref[:, cs]
    hn_ref[...] = _rms(o_ref[...], gn_ref[...]).astype(hn_ref.dtype)


def pool_prompt(x_all, g, w_pool, layer, scale, g_next, t):
    d = x_all.shape[1]
    _, ng, gc, _ = w_pool.shape
    bm = _pick(t, (512, 256, 128, 64, 32, 16))
    return pl.pallas_call(
        functools.partial(_pool_prompt_body, gc=gc),
        grid=(t // bm,),
        in_specs=[
            pl.BlockSpec((bm, d), lambda i: (i, 0)),
            pl.BlockSpec((1, d), lambda i: (0, 0)),
            pl.BlockSpec((None, ng, gc, gc), lambda i: (layer, 0, 0, 0)),
            pl.BlockSpec((1, d), lambda i: (0, 0)),
            pl.BlockSpec((1, d), lambda i: (0, 0)),
        ],
        out_specs=[
            pl.BlockSpec((bm, d), lambda i: (i, 0)),
            pl.BlockSpec((bm, d), lambda i: (i, 0)),
            pl.BlockSpec((POOL_MAXW, d), lambda i: (0, 0)),
        ],
        out_shape=[jax.ShapeDtypeStruct(x_all.shape, F32), jax.ShapeDtypeStruct(x_all.shape, BF16),
                   jax.ShapeDtypeStruct((POOL_MAXW, d), F32)],
        scratch_shapes=[pltpu.VMEM((ng, gc, gc), BF16)],
        input_output_aliases={0: 0},
        compiler_params=_params(("arbitrary",)),
        name="pool_prompt",
    )(x_all, g.reshape(1, d), w_pool, scale.reshape(1, d), g_next.reshape(1, d))


def _pool_sample_body(x_ref, st_ref, g_ref, w_ref, sc_ref, gn_ref, hn_any, o_ref, hn_ref, h_ref, *, gc):
    del hn_any
    x = x_ref[...]
    h = _rms(x, g_ref[...])
    h_ref[...] = h
    for gi, w in enumerate(POOL_WINDOWS):
        cs = slice(gi * gc, (gi + 1) * gc)
        hg = h[:, cs]
        tot = hg + jnp.sum(st_ref[:, POOL_BUF - (w - 1):, cs], axis=1)
        cnt = float(min(w, PAST_LEN + 1))
        pooled = (tot / cnt - hg).astype(BF16)
        y = jnp.dot(pooled, w_ref[gi].astype(BF16), preferred_element_type=F32)
        o_ref[:, cs] = x[:, cs] + y * sc_ref[:, cs]
    hn_ref[...] = _rms(o_ref[...], gn_ref[...]).astype(hn_ref.dtype)


def pool_sample(x_all, hn_all, state, layer, g, w_pool, scale, g_next, t):
    d = x_all.shape[1]
    b = state.shape[1]
    _, ng, gc, _ = w_pool.shape
    bt = _pick(b, (32, 16, 8))
    assert t % bt == 0
    off = t // bt
    return pl.pallas_call(
        functools.partial(_pool_sample_body, gc=gc),
        grid=(b // bt,),
        in_specs=[
            pl.BlockSpec((bt, d), lambda i: (off + i, 0)),
            pl.BlockSpec((None, bt, POOL_BUF, d), lambda i: (layer, i, 0, 0)),
            pl.BlockSpec((1, d), lambda i: (0, 0)),
            pl.BlockSpec((None, ng, gc, gc), lambda i: (layer, 0, 0, 0)),
            pl.BlockSpec((1, d), lambda i: (0, 0)),
            pl.BlockSpec((1, d), lambda i: (0, 0)),
            pl.BlockSpec(memory_space=pl.ANY),
        ],
        out_specs=[
            pl.BlockSpec((bt, d), lambda i: (off + i, 0)),
            pl.BlockSpec((bt, d), lambda i: (off + i, 0)),
            pl.BlockSpec((bt, d), lambda i: (i, 0)),
        ],
        out_shape=[jax.ShapeDtypeStruct(x_all.shape, F32), jax.ShapeDtypeStruct(hn_all.shape, hn_all.dtype),
                   jax.ShapeDtypeStruct((b, d), F32)],
        input_output_aliases={0: 0, 6: 1},
        compiler_params=_params(("arbitrary",)),
        name="pool_sample",
    )(x_all, state, g.reshape(1, d), w_pool, scale.reshape(1, d), g_next.reshape(1, d), hn_all)


def _mem_kv_body(m_ref, g_ref, w_ref, o_ref):
    h = _rms(m_ref[...], g_ref[...]).astype(BF16)
    o_ref[...] = jnp.dot(h, w_ref[...].astype(BF16), preferred_element_type=F32)


def mem_kv(mem, g, w_kv, layer):
    m, d = mem.shape
    n = w_kv.shape[2]
    bn = _pick(n, (512, 256, 128))
    return pl.pallas_call(
        _mem_kv_body,
        grid=(n // bn,),
        in_specs=[
            pl.BlockSpec((m, d), lambda j: (0, 0)),
            pl.BlockSpec((1, d), lambda j: (0, 0)),
            pl.BlockSpec((None, d, bn), lambda j: (layer, 0, j)),
        ],
        out_specs=pl.BlockSpec((m, bn), lambda j: (0, j)),
        out_shape=jax.ShapeDtypeStruct((m, n), F32),
        compiler_params=_params(("arbitrary",)),
        name="mem_kv",
    )(mem, g.reshape(1, d), w_kv)


def _xattn_prompt_body(h_ref, wq_ref, kv_ref, wo_ref, x_ref, gn_ref, o_ref, hn_ref, wqb_ref, wob_ref, kvb_ref):
    @pl.when(pl.program_id(0) == 0)
    def _():
        wqb_ref[...] = wq_ref[...].astype(BF16)
        wob_ref[...] = wo_ref[...].astype(BF16)
        kvb_ref[...] = kv_ref[...].astype(BF16)

    xw = X_HEADS * X_HEAD_DIM
    q = jnp.dot(h_ref[...], wqb_ref[...], preferred_element_type=F32) / math.sqrt(X_HEAD_DIM)
    outs = []
    for hd in range(X_HEADS):
        cs = slice(hd * X_HEAD_DIM, (hd + 1) * X_HEAD_DIM)
        s = lax.dot_general(q[:, cs].astype(BF16), kvb_ref[:, cs], (((1,), (1,)), ((), ())),
                            preferred_element_type=F32)
        p = jnp.exp(s - jnp.max(s, axis=-1, keepdims=True))
        l = jnp.sum(p, axis=-1, keepdims=True)
        vs = slice(xw + hd * X_HEAD_DIM, xw + (hd + 1) * X_HEAD_DIM)
        outs.append(jnp.dot(p.astype(BF16), kvb_ref[:, vs], preferred_element_type=F32) / l)
    o = jnp.concatenate(outs, axis=1).astype(BF16)
    xn = x_ref[...] + jnp.dot(o, wob_ref[...], preferred_element_type=F32)
    o_ref[...] = xn
    hn_ref[...] = _rms(xn, gn_ref[...]).astype(hn_ref.dtype)


def xattn_prompt(hq_all, w_q, mkv, w_o, layer, x_all, g_next, t):
    d = x_all.shape[1]
    xw = w_q.shape[2]
    nm = mkv.shape[0]
    bm = _pick(t, (512, 256, 128, 64, 32, 16))
    return pl.pallas_call(
        _xattn_prompt_body,
        grid=(t // bm,),
        in_specs=[
            pl.BlockSpec((bm, d), lambda i: (i, 0)),
            pl.BlockSpec((None, d, xw), lambda i: (layer, 0, 0)),
            pl.BlockSpec((nm, 2 * xw), lambda i: (0, 0)),
            pl.BlockSpec((None, xw, d), lambda i: (layer, 0, 0)),
            pl.BlockSpec((bm, d), lambda i: (i, 0)),
            pl.BlockSpec((1, d), lambda i: (0, 0)),
        ],
        out_specs=[pl.BlockSpec((bm, d), lambda i: (i, 0)), pl.BlockSpec((bm, d), lambda i: (i, 0))],
        out_shape=[jax.ShapeDtypeStruct(x_all.shape, F32), jax.ShapeDtypeStruct(x_all.shape, BF16)],
        scratch_shapes=[pltpu.VMEM((d, xw), BF16), pltpu.VMEM((xw, d), BF16), pltpu.VMEM((nm, 2 * xw), BF16)],
        input_output_aliases={4: 0},
        compiler_params=_params(("arbitrary",)),
        name="xattn_prompt",
    )(hq_all, w_q, mkv, w_o, x_all, g_next.reshape(1, d))


def _xattn_sample_body(h_ref, wq_ref, k_ref, v_ref, wo_ref, x_ref, e_ref, et_ref, gn_ref, hn_any, o_ref, hn_ref):
    del hn_any
    bt, nm, xw = k_ref.shape
    q = jnp.dot(h_ref[...], wq_ref[...].astype(BF16), preferred_element_type=F32) / math.sqrt(X_HEAD_DIM)
    prod = (k_ref[...] * q[:, None, :]).reshape(bt * nm, xw).astype(BF16)
    s = jnp.dot(prod, e_ref[...], preferred_element_type=F32).reshape(bt, nm, LANES)
    p = jnp.exp(s - jnp.max(s, axis=1, keepdims=True))
    pn = (p / jnp.sum(p, axis=1, keepdims=True)).astype(BF16).reshape(bt * nm, LANES)
    pe = jnp.dot(pn, et_ref[...], preferred_element_type=F32).reshape(bt, nm, xw)
    o = jnp.sum(pe * v_ref[...], axis=1).astype(BF16)
    xn = x_ref[...] + jnp.dot(o, wo_ref[...].astype(BF16), preferred_element_type=F32)
    o_ref[...] = xn
    hn_ref[...] = _rms(xn, gn_ref[...]).astype(hn_ref.dtype)


def xattn_sample(hq_all, w_q, mk, mv, w_o, layer, x_all, hn_all, g_next, t):
    d = x_all.shape[1]
    b, nm, xw = mk.shape
    bt = _pick(b, (8, 4, 2, 1))
    assert t % bt == 0
    off = t // bt
    lane_h = jnp.arange(xw) // X_HEAD_DIM
    e = (lane_h[:, None] == jnp.arange(LANES)[None, :]).astype(BF16)
    return pl.pallas_call(
        _xattn_sample_body,
        grid=(b // bt,),
        in_specs=[
            pl.BlockSpec((bt, d), lambda i: (off + i, 0)),
            pl.BlockSpec((None, d, xw), lambda i: (layer, 0, 0)),
            pl.BlockSpec((bt, nm, xw), lambda i: (i, 0, 0)),
            pl.BlockSpec((bt, nm, xw), lambda i: (i, 0, 0)),
            pl.BlockSpec((None, xw, d), lambda i: (layer, 0, 0)),
            pl.BlockSpec((bt, d), lambda i: (off + i, 0)),
            pl.BlockSpec((xw, LANES), lambda i: (0, 0)),
            pl.BlockSpec((LANES, xw), lambda i: (0, 0)),
            pl.BlockSpec((1, d), lambda i: (0, 0)),
            pl.BlockSpec(memory_space=pl.ANY),
        ],
        out_specs=[pl.BlockSpec((bt, d), lambda i: (off + i, 0)), pl.BlockSpec((bt, d), lambda i: (off + i, 0))],
        out_shape=[jax.ShapeDtypeStruct(x_all.shape, F32), jax.ShapeDtypeStruct(hn_all.shape, hn_all.dtype)],
        input_output_aliases={5: 0, 9: 1},
        compiler_params=_params(("arbitrary",)),
        name="xattn_sample",
    )(hq_all, w_q, mk, mv, w_o, x_all, e, e.T, g_next.reshape(1, d), hn_all)


def kernel(x_prompt, x_sample, state_pool, cache_swa_k, cache_swa_v, cache_mem_k, cache_mem_v, mem_prompt,
           g_ffn1, w_ffn1_gu, w_ffn1_dn, g_mix, w_pool, pool_scale, w_qkv, w_o, sinks,
           g_xq, g_mem, w_xq, w_xkv, w_xo, g_ffn2, w_ffn2_gu, w_ffn2_dn, g_final):
    bp, t, d = x_prompt.shape
    b, s_len, _ = x_sample.shape
    assert bp == 1 and s_len == 1
    depth = g_ffn1.shape[0]
    xw = w_xq.shape[2]
    nm = mem_prompt.shape[1]
    n_kv = d // HEAD_DIM // GQA_GROUP
    kvw = n_kv * HEAD_DIM

    x = jnp.concatenate([x_prompt[0], x_sample[:, 0]], axis=0)
    cos_t, sin_t = rope_tables(t, b)

    pool_p, pool_s = [], []
    swa_kp, swa_vp, swa_ks, swa_vs = [], [], [], []
    mem_kp, mem_vp = [], []
    h = rms_norm_rows(x, g_ffn1[0], BF16)
    y = None
    for layer in range(depth):
        i = layer // 2
        act, wdb = gate_up(h, w_ffn1_gu, w_ffn1_dn, layer)
        x, h = down_norm(act, wdb, x, g_mix[layer], BF16)
        if layer % 2 == 0:
            x, hq, h_last = pool_prompt(x, g_mix[layer], w_pool, i, pool_scale[i], g_xq[layer], t)
            x, hq, hs = pool_sample(x, hq, state_pool, i, g_mix[layer], w_pool, pool_scale[i], g_xq[layer], t)
            pool_p.append(h_last[None, POOL_MAXW - POOL_BUF:])
            pool_s.append(jnp.concatenate([state_pool[i][:, 1:], hs[:, None]], axis=1))
        else:
            qkv = qkv_rope(h, w_qkv, i, cos_t, sin_t, d + kvw, kvw)
            buf = cache_swa_k.shape[2]
            keep = min(WINDOW, t)
            swa_kp.append(qkv[t - keep:t, d:d + kvw].reshape(1, keep, n_kv, HEAD_DIM))
            swa_vp.append(qkv[t - keep:t, d + kvw:].reshape(1, keep, n_kv, HEAD_DIM))
            k_new = qkv[t:, d:d + kvw].reshape(b, 1, n_kv, HEAD_DIM)
            v_new = qkv[t:, d + kvw:].reshape(b, 1, n_kv, HEAD_DIM)
            ks_all = jnp.concatenate([cache_swa_k[i], k_new], axis=1)[:, -buf:]
            vs_all = jnp.concatenate([cache_swa_v[i], v_new], axis=1)[:, -buf:]
            swa_ks.append(ks_all)
            swa_vs.append(vs_all)
            assert buf == WINDOW
            o_p = swa_prompt(qkv, sinks[i], t, d)
            o_s = swa_sample(qkv[t:, :d], ks_all.reshape(b, buf, kvw), vs_all.reshape(b, buf, kvw), sinks[i])
            x, hq = proj_residual_norm(jnp.concatenate([o_p, o_s], axis=0), w_o, i, x, g_xq[layer])
        mkv = mem_kv(mem_prompt[0], g_mem[layer], w_xkv, layer)
        mem_kp.append(mkv[:, :xw].reshape(1, nm, X_HEADS, X_HEAD_DIM))
        mem_vp.append(mkv[:, xw:].reshape(1, nm, X_HEADS, X_HEAD_DIM))
        x, h = xattn_prompt(hq, w_xq, mkv, w_xo, layer, x, g_ffn2[layer], t)
        x, h = xattn_sample(hq, w_xq, cache_mem_k[layer].reshape(b, nm, xw), cache_mem_v[layer].reshape(b, nm, xw),
                            w_xo, layer, x, h, g_ffn2[layer], t)
        act, wdb = gate_up(h, w_ffn2_gu, w_ffn2_dn, layer)
        if layer + 1 < depth:
            x, h = down_norm(act, wdb, x, g_ffn1[layer + 1], BF16)
        else:
            (y,) = down_norm(act, wdb, x, g_final, F32, emit_x=False)
    return (y[None, :t], y[t:, None], jnp.stack(pool_p), jnp.stack(pool_s), jnp.stack(swa_kp), jnp.stack(swa_vp),
            jnp.stack(swa_ks), jnp.stack(swa_vs), jnp.stack(mem_kp), jnp.stack(mem_vp))
```

```python
import functools
import math

import jax
import jax.numpy as jnp
from jax import lax
from jax.experimental import pallas as pl
from jax.experimental.pallas import tpu as pltpu

F32 = jnp.float32
BF16 = jnp.bfloat16

RMS_EPS = 1e-6
PAST_LEN = 8192
POOL_WINDOWS = (2, 4, 8, 16)
POOL_MAXW = max(POOL_WINDOWS)
POOL_BUF = POOL_MAXW - 1
HEAD_DIM = 64
GQA_GROUP = 4
WINDOW = 128
ROPE_THETA = 10000.0
X_HEADS = 4
X_HEAD_DIM = 128
LANES = 128
V7X_VMEM_LIMIT = 60 * 1024 * 1024


def _params(sem):
    return pltpu.CompilerParams(dimension_semantics=sem, vmem_limit_bytes=V7X_VMEM_LIMIT)


def _pick(n, candidates):
    for c in candidates:
        if n % c == 0:
            return c
    raise ValueError(f"no block size in {candidates} divides {n}")


def _rms(x, g):
    return x * lax.rsqrt(jnp.mean(x * x, axis=-1, keepdims=True) + RMS_EPS) * g


def _norm_body(x_ref, g_ref, o_ref):
    o_ref[...] = _rms(x_ref[...], g_ref[...]).astype(o_ref.dtype)


def rms_norm_rows(x, g, out_dtype):
    m, d = x.shape
    bm = _pick(m, (832, 640, 512, 256, 128, 64, 32, 16, 8))
    return pl.pallas_call(
        _norm_body,
        grid=(m // bm,),
        in_specs=[pl.BlockSpec((bm, d), lambda i: (i, 0)), pl.BlockSpec((1, d), lambda i: (0, 0))],
        out_specs=pl.BlockSpec((bm, d), lambda i: (i, 0)),
        out_shape=jax.ShapeDtypeStruct((m, d), out_dtype),
        compiler_params=_params(("arbitrary",)),
        name="rms_norm_rows",
    )(x, g.reshape(1, d))


def _gate_up_body(h_ref, wg_ref, wu_ref, wd_ref, o_ref, wdb_ref, wb_ref, *, bf):
    @pl.when(pl.program_id(1) == 0)
    def _():
        wb_ref[:, :bf] = wg_ref[...].astype(BF16)
        wb_ref[:, bf:] = wu_ref[...].astype(BF16)
        wdb_ref[...] = wd_ref[...].astype(BF16)

    r = jnp.dot(h_ref[...], wb_ref[...], preferred_element_type=F32)
    a = r[:, :bf]
    b = r[:, bf:]
    o_ref[...] = (a / (1.0 + jnp.exp(-a)) * b * 0.5).astype(o_ref.dtype)


def gate_up(h, w_gu, w_dn, layer):
    m, d = h.shape
    f = w_gu.shape[2] // 2
    bf = _pick(f, (512, 256, 128))
    bm = _pick(m, (1664, 1024, 512, 256, 128, 64, 32, 16))
    nf = f // bf
    return pl.pallas_call(
        functools.partial(_gate_up_body, bf=bf),
        grid=(nf, m // bm),
        in_specs=[
            pl.BlockSpec((bm, d), lambda j, i: (i, 0)),
            pl.BlockSpec((None, d, bf), lambda j, i: (layer, 0, j)),
            pl.BlockSpec((None, d, bf), lambda j, i: (layer, 0, j + nf)),
            pl.BlockSpec((None, bf, d), lambda j, i: (layer, j, 0)),
        ],
        out_specs=[
            pl.BlockSpec((bm, bf), lambda j, i: (i, j)),
            pl.BlockSpec((bf, d), lambda j, i: (j, 0)),
        ],
        out_shape=[jax.ShapeDtypeStruct((m, f), BF16), jax.ShapeDtypeStruct((f, d), BF16)],
        scratch_shapes=[pltpu.VMEM((d, 2 * bf), BF16)],
        compiler_params=_params(("arbitrary", "arbitrary")),
        name="gate_up",
    )(h, w_gu, w_gu, w_dn)


def _down_body(a_ref, w_ref, x_ref, g_ref, *out_refs, emit_x):
    xn = x_ref[...] + jnp.dot(a_ref[...], w_ref[...], preferred_element_type=F32)
    if emit_x:
        out_refs[0][...] = xn
    h_ref = out_refs[-1]
    h_ref[...] = _rms(xn, g_ref[...]).astype(h_ref.dtype)


def down_norm(a, w_bf16, x, g_next, h_dtype, emit_x=True):
    m, k = a.shape
    d = w_bf16.shape[1]
    bm = _pick(m, (320, 256, 128, 64, 32, 16))
    row = lambda i: (i, 0)
    out_specs = [pl.BlockSpec((bm, d), row)]
    out_shape = [jax.ShapeDtypeStruct((m, d), h_dtype)]
    if emit_x:
        out_specs.insert(0, pl.BlockSpec((bm, d), row))
        out_shape.insert(0, jax.ShapeDtypeStruct((m, d), F32))
    return pl.pallas_call(
        functools.partial(_down_body, emit_x=emit_x),
        grid=(m // bm,),
        in_specs=[
            pl.BlockSpec((bm, k), row),
            pl.BlockSpec((k, d), lambda i: (0, 0), pipeline_mode=pl.Buffered(1)),
            pl.BlockSpec((bm, d), row),
            pl.BlockSpec((1, d), lambda i: (0, 0)),
        ],
        out_specs=out_specs,
        out_shape=out_shape,
        input_output_aliases={2: 0} if emit_x else {},
        compiler_params=_params(("arbitrary",)),
        name="down_norm",
    )(a, w_bf16, x, g_next.reshape(1, d))


def _proj_res_body(a_ref, w_ref, x_ref, g_ref, xo_ref, h_ref, wb_ref):
    @pl.when(pl.program_id(0) == 0)
    def _():
        wb_ref[...] = w_ref[...].astype(BF16)

    xn = x_ref[...] + jnp.dot(a_ref[...], wb_ref[...], preferred_element_type=F32)
    xo_ref[...] = xn
    h_ref[...] = _rms(xn, g_ref[...]).astype(h_ref.dtype)


def proj_residual_norm(a, w, layer, x, g_next):
    m, k = a.shape
    d = w.shape[2]
    bm = _pick(m, (416, 320, 256, 128, 64, 32, 16))
    row = lambda i: (i, 0)
    return pl.pallas_call(
        _proj_res_body,
        grid=(m // bm,),
        in_specs=[
            pl.BlockSpec((bm, k), row),
            pl.BlockSpec((None, k, d), lambda i: (layer, 0, 0), pipeline_mode=pl.Buffered(1)),
            pl.BlockSpec((bm, d), row),
            pl.BlockSpec((1, d), lambda i: (0, 0)),
        ],
        out_specs=[pl.BlockSpec((bm, d), row), pl.BlockSpec((bm, d), row)],
        out_shape=[jax.ShapeDtypeStruct((m, d), F32), jax.ShapeDtypeStruct((m, d), BF16)],
        scratch_shapes=[pltpu.VMEM((k, d), BF16)],
        input_output_aliases={2: 0},
        compiler_params=_params(("arbitrary",)),
        name="proj_residual_norm",
    )(a, w, x, g_next.reshape(1, d))


def _qkv_body(h_ref, w_ref, c_ref, s_ref, o_ref, wb_ref, *, n_rope_tiles):
    @pl.when(pl.program_id(1) == 0)
    def _():
        wb_ref[...] = w_ref[...].astype(BF16)

    r = jnp.dot(h_ref[...], wb_ref[...], preferred_element_type=F32)
    bm, bn = r.shape

    @pl.when(pl.program_id(0) < n_rope_tiles)
    def _():
        c = c_ref[...]
        s = s_ref[...]
        lane = lax.broadcasted_iota(jnp.int32, (bm, LANES), 1)
        first_half = (lane % HEAD_DIM) < (HEAD_DIM // 2)
        for ci in range(bn // LANES):
            blk = r[:, ci * LANES:(ci + 1) * LANES]
            partner = jnp.where(first_half,
                                pltpu.roll(blk, LANES - HEAD_DIM // 2, 1),
                                pltpu.roll(blk, HEAD_DIM // 2, 1))
            o_ref[:, ci * LANES:(ci + 1) * LANES] = blk * c + partner * s

    @pl.when(pl.program_id(0) >= n_rope_tiles)
    def _():
        o_ref[...] = r


def qkv_rope(h, w_qkv, layer, cos_t, sin_t, n_rope_cols, bn):
    m, d = h.shape
    n = w_qkv.shape[2]
    assert n % bn == 0 and n_rope_cols % bn == 0 and bn % LANES == 0
    bm = _pick(m, (832, 512, 256, 128, 64, 32, 16))
    return pl.pallas_call(
        functools.partial(_qkv_body, n_rope_tiles=n_rope_cols // bn),
        grid=(n // bn, m // bm),
        in_specs=[
            pl.BlockSpec((bm, d), lambda j, i: (i, 0)),
            pl.BlockSpec((None, d, bn), lambda j, i: (layer, 0, j)),
            pl.BlockSpec((bm, LANES), lambda j, i: (i, 0)),
            pl.BlockSpec((bm, LANES), lambda j, i: (i, 0)),
        ],
        out_specs=pl.BlockSpec((bm, bn), lambda j, i: (i, j)),
        out_shape=jax.ShapeDtypeStruct((m, n), F32),
        scratch_shapes=[pltpu.VMEM((d, bn), BF16)],
        compiler_params=_params(("arbitrary", "arbitrary")),
        name="qkv_rope",
    )(h, w_qkv, cos_t, sin_t)


def rope_tables(t_prompt, n_sample):
    half = HEAD_DIM // 2
    inv = ROPE_THETA ** (-jnp.arange(half, dtype=F32) / half)
    pos = jnp.concatenate([jnp.arange(t_prompt), jnp.full((n_sample,), PAST_LEN)]).astype(F32)
    ang = pos[:, None] * inv[None, :]
    cos, sin = jnp.cos(ang), jnp.sin(ang)
    reps = LANES // HEAD_DIM
    return (jnp.tile(jnp.concatenate([cos, cos], axis=1), (1, reps)),
            jnp.tile(jnp.concatenate([-sin, sin], axis=1), (1, reps)))


def _swa_prompt_body(sink_ref, q_ref, kp_ref, kc_ref, vp_ref, vc_ref, o_ref, *, n_kv):
    n = pl.program_id(0)
    blk = WINDOW
    nq = GQA_GROUP * blk
    keys = lax.broadcasted_iota(jnp.int32, (2 * blk, nq), 0)
    qrow = lax.broadcasted_iota(jnp.int32, (2 * blk, nq), 1) % blk
    diff = qrow + blk - keys
    mask = (diff >= 0) & (diff < WINDOW) & ((keys >= blk) | (n > 0))
    head_of_col = lax.broadcasted_iota(jnp.int32, (1, nq), 1) // blk
    log2e = math.log2(math.e)
    qscale = log2e / math.sqrt(HEAD_DIM)
    scores, values = [], []
    for kv in range(n_kv):
        ks = slice(kv * HEAD_DIM, (kv + 1) * HEAD_DIM)
        k2 = jnp.concatenate([kp_ref[:, ks], kc_ref[:, ks]], axis=0).astype(BF16)
        values.append(jnp.concatenate([vp_ref[:, ks], vc_ref[:, ks]], axis=0).astype(BF16))
        qs = jnp.concatenate(
            [q_ref[:, (kv * GQA_GROUP + g) * HEAD_DIM:(kv * GQA_GROUP + g + 1) * HEAD_DIM] * qscale
             for g in range(GQA_GROUP)], axis=0).astype(BF16)
        scores.append(lax.dot_general(k2, qs, (((1,), (1,)), ((), ())), preferred_element_type=F32))
    probs = []
    for kv in range(n_kv):
        s = jnp.where(mask, scores[kv], -jnp.inf)
        sink = jnp.zeros((1, nq), F32)
        for g in range(GQA_GROUP):
            sink = jnp.where(head_of_col == g, sink_ref[kv * GQA_GROUP + g] * log2e, sink)
        mx = jnp.maximum(jnp.max(s, axis=0, keepdims=True), sink)
        p = jnp.exp2(s - mx)
        denom = jnp.sum(p, axis=0, keepdims=True) + jnp.exp2(sink - mx)
        probs.append((p * (1.0 / denom)).astype(BF16))
    for kv in range(n_kv):
        o = lax.dot_general(probs[kv], values[kv], (((0,), (0,)), ((), ())), preferred_element_type=F32)
        for g in range(GQA_GROUP):
            c0 = (kv * GQA_GROUP + g) * HEAD_DIM
            o_ref[:, c0:c0 + HEAD_DIM] = o[g * blk:(g + 1) * blk].astype(o_ref.dtype)


def swa_prompt(qkv, sinks, t, d):
    n_kv = d // HEAD_DIM // GQA_GROUP
    kvw = n_kv * HEAD_DIM
    assert d % kvw == 0 and t % WINDOW == 0
    kblk = d // kvw
    vblk = kblk + 1
    prev = lambda n: jnp.maximum(n - 1, 0)
    return pl.pallas_call(
        functools.partial(_swa_prompt_body, n_kv=n_kv),
        grid=(t // WINDOW,),
        in_specs=[
            pl.BlockSpec(memory_space=pltpu.SMEM),
            pl.BlockSpec((WINDOW, d), lambda n: (n, 0)),
            pl.BlockSpec((WINDOW, kvw), lambda n: (prev(n), kblk)),
            pl.BlockSpec((WINDOW, kvw), lambda n: (n, kblk)),
            pl.BlockSpec((WINDOW, kvw), lambda n: (prev(n), vblk)),
            pl.BlockSpec((WINDOW, kvw), lambda n: (n, vblk)),
        ],
        out_specs=pl.BlockSpec((WINDOW, d), lambda n: (n, 0)),
        out_shape=jax.ShapeDtypeStruct((t, d), BF16),
        compiler_params=_params(("arbitrary",)),
        name="swa_prompt",
    )(sinks, qkv, qkv, qkv, qkv, qkv)


def _swa_sample_body(sink_ref, qkvt_ref, k_ref, v_ref, ko_ref, vo_ref, o_ref, *, n_kv, bt):
    i = pl.program_id(0)
    nk = k_ref.shape[1]
    nb = qkvt_ref.shape[1]
    kvw = n_kv * HEAD_DIM
    n_heads = n_kv * GQA_GROUP
    nq = n_heads * HEAD_DIM
    sub = 8

    @pl.when(i == 0)
    def _():
        o_ref[...] = jnp.zeros_like(o_ref)

    sample_lane = lax.broadcasted_iota(jnp.int32, (1, nb), 1)
    key_lane = lax.broadcasted_iota(jnp.int32, (1, nk), 1)
    sublane = lax.broadcasted_iota(jnp.int32, (sub, nk), 0)
    head_row = lax.broadcasted_iota(jnp.int32, (n_heads, 1), 0)
    sink = jnp.zeros((n_heads, 1), F32)
    for hd in range(n_heads):
        sink = jnp.where(head_row == hd, sink_ref[hd], sink)
    qt = (qkvt_ref[:nq, :] * (1.0 / math.sqrt(HEAD_DIM))).astype(BF16)
    knt = qkvt_ref[nq:nq + kvw, :]
    vnt = qkvt_ref[nq + kvw:, :]
    wide_row = lax.broadcasted_iota(jnp.int32, (nb, bt * nk), 0)
    wide_sample = lax.broadcasted_iota(jnp.int32, (nb, bt * nk), 1) // nk + i * bt
    qcols_all = jnp.dot(qt, (wide_row == wide_sample).astype(BF16), preferred_element_type=F32)
    pvs = []
    for j in range(bt):
        bg = i * bt + j
        is_b = sample_lane == bg
        qcols = qcols_all[:, j * nk:(j + 1) * nk]
        kn = jnp.sum(jnp.where(is_b, knt, 0.0), axis=1, keepdims=True)
        vn = jnp.sum(jnp.where(is_b, vnt, 0.0), axis=1, keepdims=True)
        rows = slice(j * kvw, (j + 1) * kvw)
        kfull = jnp.where(key_lane == nk - 1, kn, pltpu.roll(k_ref[rows, :], nk - 1, 1))
        vfull = jnp.where(key_lane == nk - 1, vn, pltpu.roll(v_ref[rows, :], nk - 1, 1))
        ko_ref[rows, :] = kfull
        vo_ref[rows, :] = vfull
        tiles = []
        for tile in range(n_heads // sub):
            st = jnp.zeros((sub, nk), F32)
            for r in range(sub):
                hidx = tile * sub + r
                kv = hidx // GQA_GROUP
                prod = kfull[kv * HEAD_DIM:(kv + 1) * HEAD_DIM] * qcols[hidx * HEAD_DIM:(hidx + 1) * HEAD_DIM]
                red = jnp.sum(prod.reshape(HEAD_DIM // sub, sub, nk), axis=0)
                step = sub // 2
                while step >= 1:
                    red = red + pltpu.roll(red, step, 0)
                    step //= 2
                st = jnp.where(sublane == r, red, st)
            tiles.append(st)
        s = jnp.concatenate(tiles, axis=0)
        mx = jnp.maximum(jnp.max(s, axis=1, keepdims=True), sink)
        p = jnp.exp(s - mx)
        pn = p / (jnp.sum(p, axis=1, keepdims=True) + jnp.exp(sink - mx))
        pvs.append(jnp.concatenate(
            [vfull[(hidx // GQA_GROUP) * HEAD_DIM:(hidx // GQA_GROUP + 1) * HEAD_DIM] * pn[hidx:hidx + 1, :]
             for hidx in range(n_heads)], axis=0).astype(BF16))
    tall_sample = lax.broadcasted_iota(jnp.int32, (bt * nk, nb), 0) // nk + i * bt
    tall_lane = lax.broadcasted_iota(jnp.int32, (bt * nk, nb), 1)
    o_ref[...] += jnp.dot(jnp.concatenate(pvs, axis=1), (tall_sample == tall_lane).astype(BF16),
                          preferred_element_type=F32)


def swa_sample(qkvt, kt, vt, sinks, n_kv):
    nrow, b = qkvt.shape
    kvw = n_kv * HEAD_DIM
    d = nrow - 2 * kvw
    nk = kt.shape[1]
    assert d == n_kv * GQA_GROUP * HEAD_DIM and (n_kv * GQA_GROUP) % 8 == 0
    bt = _pick(b, (8, 4, 2, 1))
    cache_spec = pl.BlockSpec((bt * kvw, nk), lambda i: (i, 0))
    return pl.pallas_call(
        functools.partial(_swa_sample_body, n_kv=n_kv, bt=bt),
        grid=(b // bt,),
        in_specs=[
            pl.BlockSpec(memory_space=pltpu.SMEM),
            pl.BlockSpec((nrow, b), lambda i: (0, 0)),
            cache_spec,
            cache_spec,
        ],
        out_specs=[cache_spec, cache_spec, pl.BlockSpec((d, b), lambda i: (0, 0))],
        out_shape=[jax.ShapeDtypeStruct(kt.shape, F32), jax.ShapeDtypeStruct(vt.shape, F32),
                   jax.ShapeDtypeStruct((d, b), F32)],
        compiler_params=_params(("arbitrary",)),
        name="swa_sample",
    )(sinks, qkvt, kt, vt)


def _pool_prompt_body(x_ref, g_ref, w_ref, sc_ref, gn_ref, o_ref, hn_ref, hl_ref, wb_ref, *, gc):
    i = pl.program_id(0)

    @pl.when(i == 0)
    def _():
        wb_ref[...] = w_ref[...].astype(BF16)
        hl_ref[...] = jnp.zeros_like(hl_ref)

    bm = x_ref.shape[0]
    x = x_ref[...]
    h = _rms(x, g_ref[...])
    hprev = hl_ref[...]
    hl_ref[...] = h[bm - POOL_MAXW:, :]
    pos1 = (lax.broadcasted_iota(jnp.int32, (bm, 1), 0) + i * bm + 1).astype(F32)
    for gi, w in enumerate(POOL_WINDOWS):
        cs = slice(gi * gc, (gi + 1) * gc)
        hg = h[:, cs]
        acc = jnp.concatenate([hprev[:, cs], hg], axis=0)
        step = 1
        while step < w:
            acc = acc + pltpu.roll(acc, step, 0)
            step *= 2
        cnt = jnp.minimum(jnp.float32(w), pos1)
        pooled = (acc[POOL_MAXW:, :] / cnt - hg).astype(BF16)
        y = jnp.dot(pooled, wb_ref[gi], preferred_element_type=F32)
        o_ref[:, cs] = x[:, cs] + y * sc_ref[:, cs]
    hn_ref[...] = _rms(o_ref[...], gn_ref[...]).astype(hn_ref.dtype)


def pool_prompt(x_all, g, w_pool, layer, scale, g_next, t):
    d = x_all.shape[1]
    _, ng, gc, _ = w_pool.shape
    bm = _pick(t, (512, 256, 128, 64, 32, 16))
    return pl.pallas_call(
        functools.partial(_pool_prompt_body, gc=gc),
        grid=(t // bm,),
        in_specs=[
            pl.BlockSpec((bm, d), lambda i: (i, 0)),
            pl.BlockSpec((1, d), lambda i: (0, 0)),
            pl.BlockSpec((None, ng, gc, gc), lambda i: (layer, 0, 0, 0)),
            pl.BlockSpec((1, d), lambda i: (0, 0)),
            pl.BlockSpec((1, d), lambda i: (0, 0)),
        ],
        out_specs=[
            pl.BlockSpec((bm, d), lambda i: (i, 0)),
            pl.BlockSpec((bm, d), lambda i: (i, 0)),
            pl.BlockSpec((POOL_MAXW, d), lambda i: (0, 0)),
        ],
        out_shape=[jax.ShapeDtypeStruct(x_all.shape, F32), jax.ShapeDtypeStruct(x_all.shape, BF16),
                   jax.ShapeDtypeStruct((POOL_MAXW, d), F32)],
        scratch_shapes=[pltpu.VMEM((ng, gc, gc), BF16)],
        input_output_aliases={0: 0},
        compiler_params=_params(("arbitrary",)),
        name="pool_prompt",
    )(x_all, g.reshape(1, d), w_pool, scale.reshape(1, d), g_next.reshape(1, d))


def _pool_sample_body(x_ref, st_ref, g_ref, w_ref, sc_ref, gn_ref, hn_any, o_ref, hn_ref, ns_ref, *, gc):
    del hn_any
    x = x_ref[...]
    h = _rms(x, g_ref[...])
    ns_ref[:POOL_BUF - 1] = st_ref[1:]
    ns_ref[POOL_BUF - 1] = h
    for gi, w in enumerate(POOL_WINDOWS):
        cs = slice(gi * gc, (gi + 1) * gc)
        hg = h[:, cs]
        tot = hg + jnp.sum(st_ref[POOL_BUF - (w - 1):, :, cs], axis=0)
        cnt = float(min(w, PAST_LEN + 1))
        pooled = (tot / cnt - hg).astype(BF16)
        y = jnp.dot(pooled, w_ref[gi].astype(BF16), preferred_element_type=F32)
        o_ref[:, cs] = x[:, cs] + y * sc_ref[:, cs]
    hn_ref[...] = _rms(o_ref[...], gn_ref[...]).astype(hn_ref.dtype)


def pool_sample(x_all, hn_all, state, layer, g, w_pool, scale, g_next, t):
    d = x_all.shape[1]
    b = state.shape[2]
    _, ng, gc, _ = w_pool.shape
    bt = _pick(b, (32, 16, 8))
    assert t % bt == 0
    off = t // bt
    return pl.pallas_call(
        functools.partial(_pool_sample_body, gc=gc),
        grid=(b // bt,),
        in_specs=[
            pl.BlockSpec((bt, d), lambda i: (off + i, 0)),
            pl.BlockSpec((None, POOL_BUF, bt, d), lambda i: (layer, 0, i, 0)),
            pl.BlockSpec((1, d), lambda i: (0, 0)),
            pl.BlockSpec((None, ng, gc, gc), lambda i: (layer, 0, 0, 0)),
            pl.BlockSpec((1, d), lambda i: (0, 0)),
            pl.BlockSpec((1, d), lambda i: (0, 0)),
            pl.BlockSpec(memory_space=pl.ANY),
        ],
        out_specs=[
            pl.BlockSpec((bt, d), lambda i: (off + i, 0)),
            pl.BlockSpec((bt, d), lambda i: (off + i, 0)),
            pl.BlockSpec((POOL_BUF, bt, d), lambda i: (0, i, 0)),
        ],
        out_shape=[jax.ShapeDtypeStruct(x_all.shape, F32), jax.ShapeDtypeStruct(hn_all.shape, hn_all.dtype),
                   jax.ShapeDtypeStruct((POOL_BUF, b, d), F32)],
        input_output_aliases={0: 0, 6: 1},
        compiler_params=_params(("arbitrary",)),
        name="pool_sample",
    )(x_all, state, g.reshape(1, d), w_pool, scale.reshape(1, d), g_next.reshape(1, d), hn_all)


def _mem_kv_body(m_ref, g_ref, w_ref, o_ref):
    h = _rms(m_ref[...], g_ref[...]).astype(BF16)
    o_ref[...] = jnp.dot(h, w_ref[...].astype(BF16), preferred_element_type=F32)


def mem_kv(mem, g, w_kv, layer):
    m, d = mem.shape
    n = w_kv.shape[2]
    bn = _pick(n, (512, 256, 128))
    return pl.pallas_call(
        _mem_kv_body,
        grid=(n // bn,),
        in_specs=[
            pl.BlockSpec((m, d), lambda j: (0, 0)),
            pl.BlockSpec((1, d), lambda j: (0, 0)),
            pl.BlockSpec((None, d, bn), lambda j: (layer, 0, j)),
        ],
        out_specs=pl.BlockSpec((m, bn), lambda j: (0, j)),
        out_shape=jax.ShapeDtypeStruct((m, n), F32),
        compiler_params=_params(("arbitrary",)),
        name="mem_kv",
    )(mem, g.reshape(1, d), w_kv)


def _xattn_prompt_body(h_ref, wq_ref, kv_ref, wo_ref, x_ref, gn_ref, o_ref, hn_ref, wqb_ref, wob_ref, kvb_ref):
    @pl.when(pl.program_id(0) == 0)
    def _():
        wqb_ref[...] = wq_ref[...].astype(BF16)
        wob_ref[...] = wo_ref[...].astype(BF16)
        kvb_ref[...] = kv_ref[...].astype(BF16)

    xw = X_HEADS * X_HEAD_DIM
    q = jnp.dot(h_ref[...], wqb_ref[...], preferred_element_type=F32) / math.sqrt(X_HEAD_DIM)
    outs = []
    for hd in range(X_HEADS):
        cs = slice(hd * X_HEAD_DIM, (hd + 1) * X_HEAD_DIM)
        s = lax.dot_general(q[:, cs].astype(BF16), kvb_ref[:, cs], (((1,), (1,)), ((), ())),
                            preferred_element_type=F32)
        p = jnp.exp(s - jnp.max(s, axis=-1, keepdims=True))
        l = jnp.sum(p, axis=-1, keepdims=True)
        vs = slice(xw + hd * X_HEAD_DIM, xw + (hd + 1) * X_HEAD_DIM)
        outs.append(jnp.dot(p.astype(BF16), kvb_ref[:, vs], preferred_element_type=F32) / l)
    o = jnp.concatenate(outs, axis=1).astype(BF16)
    xn = x_ref[...] + jnp.dot(o, wob_ref[...], preferred_element_type=F32)
    o_ref[...] = xn
    hn_ref[...] = _rms(xn, gn_ref[...]).astype(hn_ref.dtype)


def xattn_prompt(hq_all, w_q, mkv, w_o, layer, x_all, g_next, t):
    d = x_all.shape[1]
    xw = w_q.shape[2]
    nm = mkv.shape[0]
    bm = _pick(t, (512, 256, 128, 64, 32, 16))
    return pl.pallas_call(
        _xattn_prompt_body,
        grid=(t // bm,),
        in_specs=[
            pl.BlockSpec((bm, d), lambda i: (i, 0)),
            pl.BlockSpec((None, d, xw), lambda i: (layer, 0, 0)),
            pl.BlockSpec((nm, 2 * xw), lambda i: (0, 0)),
            pl.BlockSpec((None, xw, d), lambda i: (layer, 0, 0)),
            pl.BlockSpec((bm, d), lambda i: (i, 0)),
            pl.BlockSpec((1, d), lambda i: (0, 0)),
        ],
        out_specs=[pl.BlockSpec((bm, d), lambda i: (i, 0)), pl.BlockSpec((bm, d), lambda i: (i, 0))],
        out_shape=[jax.ShapeDtypeStruct(x_all.shape, F32), jax.ShapeDtypeStruct(x_all.shape, BF16)],
        scratch_shapes=[pltpu.VMEM((d, xw), BF16), pltpu.VMEM((xw, d), BF16), pltpu.VMEM((nm, 2 * xw), BF16)],
        input_output_aliases={4: 0},
        compiler_params=_params(("arbitrary",)),
        name="xattn_prompt",
    )(hq_all, w_q, mkv, w_o, x_all, g_next.reshape(1, d))


def _xq_sample_body(h_ref, wq_ref, o_ref):
    o_ref[...] = jnp.dot(h_ref[...], wq_ref[...].astype(BF16), preferred_element_type=F32) / math.sqrt(X_HEAD_DIM)


def xq_sample(hq_all, w_q, layer, t, b):
    d = hq_all.shape[1]
    xw = w_q.shape[2]
    assert t % b == 0
    return pl.pallas_call(
        _xq_sample_body,
        grid=(1,),
        in_specs=[
            pl.BlockSpec((b, d), lambda i: (t // b, 0)),
            pl.BlockSpec((None, d, xw), lambda i: (layer, 0, 0)),
        ],
        out_specs=pl.BlockSpec((b, xw), lambda i: (0, 0)),
        out_shape=jax.ShapeDtypeStruct((b, xw), F32),
        compiler_params=_params(("arbitrary",)),
        name="xq_sample",
    )(hq_all, w_q)


def _xattn_core_body(q_ref, k_ref, v_ref, ones_ref, o_ref):
    bt, sub, hd = q_ref.shape
    nv = k_ref.shape[0] // (bt * sub)
    k = k_ref[...].reshape(bt, nv, sub, hd)
    prod = (k * q_ref[...][:, None]).reshape(bt * nv * sub, hd).astype(BF16)
    s = jnp.dot(prod, ones_ref[...], preferred_element_type=F32).reshape(bt, nv, sub, hd)

    def fold(x, op):
        step = X_HEADS
        while step < sub:
            x = op(x, pltpu.roll(x, step, 2))
            step *= 2
        return x

    mx = fold(jnp.max(s, axis=1, keepdims=True), jnp.maximum)
    p = jnp.exp(s - mx)
    l = fold(jnp.sum(p, axis=1, keepdims=True), jnp.add)
    acc = fold(jnp.sum(p * v_ref[...].reshape(bt, nv, sub, hd), axis=1, keepdims=True), jnp.add)
    o_ref[...] = (acc / l).reshape(bt, sub, hd)


def xattn_core_sample(q_s, mk_flat, mv_flat, layer, nm):
    b = q_s.shape[0]
    hd = X_HEAD_DIM
    sub = 8
    assert sub % X_HEADS == 0 and nm % (sub // X_HEADS) == 0
    bt = _pick(b, (8, 4, 2, 1))
    nblk = b // bt
    rows = bt * nm * X_HEADS
    q8 = jnp.tile(q_s.reshape(b, X_HEADS, hd), (1, sub // X_HEADS, 1))
    o8 = pl.pallas_call(
        _xattn_core_body,
        grid=(nblk,),
        in_specs=[
            pl.BlockSpec((bt, sub, hd), lambda i: (i, 0, 0)),
            pl.BlockSpec((rows, hd), lambda i: (layer * nblk + i, 0)),
            pl.BlockSpec((rows, hd), lambda i: (layer * nblk + i, 0)),
            pl.BlockSpec((hd, hd), lambda i: (0, 0)),
        ],
        out_specs=pl.BlockSpec((bt, sub, hd), lambda i: (i, 0, 0)),
        out_shape=jax.ShapeDtypeStruct((b, sub, hd), F32),
        compiler_params=_params(("arbitrary",)),
        name="xattn_core_sample",
    )(q8, mk_flat, mv_flat, jnp.ones((hd, hd), BF16))
    return o8[:, :X_HEADS].reshape(b, X_HEADS * hd)


def _xout_sample_body(a_ref, wo_ref, x_ref, gn_ref, hn_any, o_ref, hn_ref):
    del hn_any
    xn = x_ref[...] + jnp.dot(a_ref[...].astype(BF16), wo_ref[...].astype(BF16), preferred_element_type=F32)
    o_ref[...] = xn
    hn_ref[...] = _rms(xn, gn_ref[...]).astype(hn_ref.dtype)


def xout_sample(o_s, w_o, layer, x_all, hn_all, g_next, t):
    d = x_all.shape[1]
    b, xw = o_s.shape
    assert t % b == 0
    off = t // b
    return pl.pallas_call(
        _xout_sample_body,
        grid=(1,),
        in_specs=[
            pl.BlockSpec((b, xw), lambda i: (0, 0)),
            pl.BlockSpec((None, xw, d), lambda i: (layer, 0, 0)),
            pl.BlockSpec((b, d), lambda i: (off, 0)),
            pl.BlockSpec((1, d), lambda i: (0, 0)),
            pl.BlockSpec(memory_space=pl.ANY),
        ],
        out_specs=[pl.BlockSpec((b, d), lambda i: (off, 0)), pl.BlockSpec((b, d), lambda i: (off, 0))],
        out_shape=[jax.ShapeDtypeStruct(x_all.shape, F32), jax.ShapeDtypeStruct(hn_all.shape, hn_all.dtype)],
        input_output_aliases={2: 0, 4: 1},
        compiler_params=_params(("arbitrary",)),
        name="xout_sample",
    )(o_s, w_o, x_all, g_next.reshape(1, d), hn_all)


def kernel(x_prompt, x_sample, state_pool, cache_swa_k, cache_swa_v, cache_mem_k, cache_mem_v, mem_prompt,
           g_ffn1, w_ffn1_gu, w_ffn1_dn, g_mix, w_pool, pool_scale, w_qkv, w_o, sinks,
           g_xq, g_mem, w_xq, w_xkv, w_xo, g_ffn2, w_ffn2_gu, w_ffn2_dn, g_final):
    bp, t, d = x_prompt.shape
    b, s_len, _ = x_sample.shape
    assert bp == 1 and s_len == 1
    depth = g_ffn1.shape[0]
    xw = w_xq.shape[2]
    nm = mem_prompt.shape[1]
    n_kv = d // HEAD_DIM // GQA_GROUP
    kvw = n_kv * HEAD_DIM

    x = jnp.concatenate([x_prompt[0], x_sample[:, 0]], axis=0)
    cos_t, sin_t = rope_tables(t, b)
    mk_flat = cache_mem_k.reshape(-1, X_HEAD_DIM)
    mv_flat = cache_mem_v.reshape(-1, X_HEAD_DIM)
    state_sm = jnp.swapaxes(state_pool, 1, 2)

    pool_p, pool_s = [], []
    swa_kp, swa_vp, swa_ks, swa_vs = [], [], [], []
    mem_kp, mem_vp = [], []
    h = rms_norm_rows(x, g_ffn1[0], BF16)
    y = None
    for layer in range(depth):
        i = layer // 2
        act, wdb = gate_up(h, w_ffn1_gu, w_ffn1_dn, layer)
        x, h = down_norm(act, wdb, x, g_mix[layer], BF16)
        if layer % 2 == 0:
            x, hq, h_last = pool_prompt(x, g_mix[layer], w_pool, i, pool_scale[i], g_xq[layer], t)
            x, hq, new_state = pool_sample(x, hq, state_sm, i, g_mix[layer], w_pool, pool_scale[i], g_xq[layer], t)
            pool_p.append(h_last[None, POOL_MAXW - POOL_BUF:])
            pool_s.append(jnp.swapaxes(new_state, 0, 1))
        else:
            qkv = qkv_rope(h, w_qkv, i, cos_t, sin_t, d + kvw, kvw)
            buf = cache_swa_k.shape[2]
            keep = min(WINDOW, t)
            swa_kp.append(qkv[t - keep:t, d:d + kvw].reshape(1, keep, n_kv, HEAD_DIM))
            swa_vp.append(qkv[t - keep:t, d + kvw:].reshape(1, keep, n_kv, HEAD_DIM))
            assert buf == WINDOW
            to_fm = lambda c: jnp.transpose(c, (0, 2, 3, 1)).reshape(b * kvw, buf)
            from_fm = lambda c: jnp.transpose(c.reshape(b, n_kv, HEAD_DIM, buf), (0, 3, 1, 2))
            ks_fm, vs_fm, o_fm = swa_sample(qkv[t:].T, to_fm(cache_swa_k[i]), to_fm(cache_swa_v[i]), sinks[i], n_kv)
            swa_ks.append(from_fm(ks_fm))
            swa_vs.append(from_fm(vs_fm))
            o_p = swa_prompt(qkv, sinks[i], t, d)
            o_all = jnp.concatenate([o_p, o_fm.T.astype(BF16)], axis=0)
            x, hq = proj_residual_norm(o_all, w_o, i, x, g_xq[layer])
        mkv = mem_kv(mem_prompt[0], g_mem[layer], w_xkv, layer)
        mem_kp.append(mkv[:, :xw].reshape(1, nm, X_HEADS, X_HEAD_DIM))
        mem_vp.append(mkv[:, xw:].reshape(1, nm, X_HEADS, X_HEAD_DIM))
        x, h = xattn_prompt(hq, w_xq, mkv, w_xo, layer, x, g_ffn2[layer], t)
        o_s = xattn_core_sample(xq_sample(hq, w_xq, layer, t, b), mk_flat, mv_flat, layer, nm)
        x, h = xout_sample(o_s, w_xo, layer, x, h, g_ffn2[layer], t)
        act, wdb = gate_up(h, w_ffn2_gu, w_ffn2_dn, layer)
        if layer + 1 < depth:
            x, h = down_norm(act, wdb, x, g_ffn1[layer + 1], BF16)
        else:
            (y,) = down_norm(act, wdb, x, g_final, F32, emit_x=False)
    return (y[None, :t], y[t:, None], jnp.stack(pool_p), jnp.stack(pool_s), jnp.stack(swa_kp), jnp.stack(swa_vp),
            jnp.stack(swa_ks), jnp.stack(swa_vs), jnp.stack(mem_kp), jnp.stack(mem_vp))
```

```python
import functools
import math

import jax
import jax.numpy as jnp
from jax import lax
from jax.experimental import pallas as pl
from jax.experimental.pallas import tpu as pltpu

F32 = jnp.float32
BF16 = jnp.bfloat16

RMS_EPS = 1e-6
PAST_LEN = 8192
POOL_WINDOWS = (2, 4, 8, 16)
POOL_MAXW = max(POOL_WINDOWS)
POOL_BUF = POOL_MAXW - 1
HEAD_DIM = 64
GQA_GROUP = 4
WINDOW = 128
ROPE_THETA = 10000.0
X_HEADS = 4
X_HEAD_DIM = 128
LANES = 128
V7X_VMEM_LIMIT = 60 * 1024 * 1024


def _params(sem):
    return pltpu.CompilerParams(dimension_semantics=sem, vmem_limit_bytes=V7X_VMEM_LIMIT)


def _pick(n, candidates):
    for c in candidates:
        if n % c == 0:
            return c
    raise ValueError(f"no block size in {candidates} divides {n}")


def _rms(x, g):
    return x * lax.rsqrt(jnp.mean(x * x, axis=-1, keepdims=True) + RMS_EPS) * g


def _norm_body(x_ref, g_ref, o_ref):
    o_ref[...] = _rms(x_ref[...], g_ref[...]).astype(o_ref.dtype)


def rms_norm_rows(x, g, out_dtype):
    m, d = x.shape
    bm = _pick(m, (832, 640, 512, 256, 128, 64, 32, 16, 8))
    return pl.pallas_call(
        _norm_body,
        grid=(m // bm,),
        in_specs=[pl.BlockSpec((bm, d), lambda i: (i, 0)), pl.BlockSpec((1, d), lambda i: (0, 0))],
        out_specs=pl.BlockSpec((bm, d), lambda i: (i, 0)),
        out_shape=jax.ShapeDtypeStruct((m, d), out_dtype),
        compiler_params=_params(("arbitrary",)),
        name="rms_norm_rows",
    )(x, g.reshape(1, d))


def _gate_up_body(h_ref, wg_ref, wu_ref, wd_ref, o_ref, wdb_ref, wb_ref, *, bf):
    @pl.when(pl.program_id(1) == 0)
    def _():
        wb_ref[:, :bf] = wg_ref[...].astype(BF16)
        wb_ref[:, bf:] = wu_ref[...].astype(BF16)
        wdb_ref[...] = wd_ref[...].astype(BF16)

    r = jnp.dot(h_ref[...], wb_ref[...], preferred_element_type=F32)
    a = r[:, :bf]
    b = r[:, bf:]
    o_ref[...] = (a / (1.0 + jnp.exp(-a)) * b * 0.5).astype(o_ref.dtype)


def gate_up(h, w_gu, w_dn, layer):
    m, d = h.shape
    f = w_gu.shape[2] // 2
    bf = _pick(f, (512, 256, 128))
    bm = _pick(m, (1664, 1024, 512, 256, 128, 64, 32, 16))
    nf = f // bf
    return pl.pallas_call(
        functools.partial(_gate_up_body, bf=bf),
        grid=(nf, m // bm),
        in_specs=[
            pl.BlockSpec((bm, d), lambda j, i: (i, 0)),
            pl.BlockSpec((None, d, bf), lambda j, i: (layer, 0, j)),
            pl.BlockSpec((None, d, bf), lambda j, i: (layer, 0, j + nf)),
            pl.BlockSpec((None, bf, d), lambda j, i: (layer, j, 0)),
        ],
        out_specs=[
            pl.BlockSpec((bm, bf), lambda j, i: (i, j)),
            pl.BlockSpec((bf, d), lambda j, i: (j, 0)),
        ],
        out_shape=[jax.ShapeDtypeStruct((m, f), BF16), jax.ShapeDtypeStruct((f, d), BF16)],
        scratch_shapes=[pltpu.VMEM((d, 2 * bf), BF16)],
        compiler_params=_params(("arbitrary", "arbitrary")),
        name="gate_up",
    )(h, w_gu, w_gu, w_dn)


def _down_body(a_ref, w_ref, x_ref, g_ref, *out_refs, emit_x):
    xn = x_ref[...] + jnp.dot(a_ref[...], w_ref[...], preferred_element_type=F32)
    if emit_x:
        out_refs[0][...] = xn
    h_ref = out_refs[-1]
    h_ref[...] = _rms(xn, g_ref[...]).astype(h_ref.dtype)


def down_norm(a, w_bf16, x, g_next, h_dtype, emit_x=True, row0=0, nrows=None):
    m, k = a.shape
    d = w_bf16.shape[1]
    nrows = m if nrows is None else nrows
    assert emit_x is False or (row0 == 0 and nrows == m)
    bm = _pick(math.gcd(nrows, row0) if row0 else nrows, (320, 256, 128, 64, 32, 16))
    off = row0 // bm
    row = lambda i: (i, 0)
    src = lambda i: (i + off, 0)
    out_specs = [pl.BlockSpec((bm, d), row)]
    out_shape = [jax.ShapeDtypeStruct((nrows, d), h_dtype)]
    if emit_x:
        out_specs.insert(0, pl.BlockSpec((bm, d), row))
        out_shape.insert(0, jax.ShapeDtypeStruct((m, d), F32))
    return pl.pallas_call(
        functools.partial(_down_body, emit_x=emit_x),
        grid=(nrows // bm,),
        in_specs=[
            pl.BlockSpec((bm, k), src),
            pl.BlockSpec((k, d), lambda i: (0, 0), pipeline_mode=pl.Buffered(1)),
            pl.BlockSpec((bm, d), src),
            pl.BlockSpec((1, d), lambda i: (0, 0)),
        ],
        out_specs=out_specs,
        out_shape=out_shape,
        input_output_aliases={2: 0} if emit_x else {},
        compiler_params=_params(("arbitrary",)),
        name="down_norm",
    )(a, w_bf16, x, g_next.reshape(1, d))


def _proj_res_body(a_ref, w_ref, x_ref, g_ref, xo_ref, h_ref):
    xn = x_ref[...] + lax.dot_general(a_ref[...], w_ref[...], (((0,), (0,)), ((), ())),
                                      preferred_element_type=F32)
    xo_ref[...] = xn
    h_ref[...] = _rms(xn, g_ref[...]).astype(h_ref.dtype)


def proj_residual_norm(a_fm, w_bf16, x, g_next):
    k, m = a_fm.shape
    d = w_bf16.shape[1]
    bm = _pick(m, (640, 512, 256, 128))
    row = lambda i: (i, 0)
    return pl.pallas_call(
        _proj_res_body,
        grid=(m // bm,),
        in_specs=[
            pl.BlockSpec((k, bm), lambda i: (0, i)),
            pl.BlockSpec((k, d), lambda i: (0, 0), pipeline_mode=pl.Buffered(1)),
            pl.BlockSpec((bm, d), row),
            pl.BlockSpec((1, d), lambda i: (0, 0)),
        ],
        out_specs=[pl.BlockSpec((bm, d), row), pl.BlockSpec((bm, d), row)],
        out_shape=[jax.ShapeDtypeStruct((m, d), F32), jax.ShapeDtypeStruct((m, d), BF16)],
        input_output_aliases={2: 0},
        compiler_params=_params(("arbitrary",)),
        name="proj_residual_norm",
    )(a_fm, w_bf16, x, g_next.reshape(1, d))


def _qkv_body(h_ref, w_ref, c_ref, s_ref, o_ref, wb_ref, *, n_rope_tiles):
    @pl.when(pl.program_id(1) == 0)
    def _():
        wb_ref[...] = w_ref[...].astype(BF16)

    r = jnp.dot(h_ref[...], wb_ref[...], preferred_element_type=F32)
    bm, bn = r.shape

    @pl.when(pl.program_id(0) < n_rope_tiles)
    def _():
        c = c_ref[...]
        s = s_ref[...]
        lane = lax.broadcasted_iota(jnp.int32, (bm, LANES), 1)
        first_half = (lane % HEAD_DIM) < (HEAD_DIM // 2)
        for ci in range(bn // LANES):
            blk = r[:, ci * LANES:(ci + 1) * LANES]
            partner = jnp.where(first_half,
                                pltpu.roll(blk, LANES - HEAD_DIM // 2, 1),
                                pltpu.roll(blk, HEAD_DIM // 2, 1))
            o_ref[:, ci * LANES:(ci + 1) * LANES] = blk * c + partner * s

    @pl.when(pl.program_id(0) >= n_rope_tiles)
    def _():
        o_ref[...] = r


def qkv_rope(h, w_qkv, layer, cos_t, sin_t, n_rope_cols, bn):
    m, d = h.shape
    n = w_qkv.shape[2]
    assert n % bn == 0 and n_rope_cols % bn == 0 and bn % LANES == 0
    bm = _pick(m, (1664, 1024, 512, 256, 128, 64, 32, 16))
    return pl.pallas_call(
        functools.partial(_qkv_body, n_rope_tiles=n_rope_cols // bn),
        grid=(n // bn, m // bm),
        in_specs=[
            pl.BlockSpec((bm, d), lambda j, i: (i, 0)),
            pl.BlockSpec((None, d, bn), lambda j, i: (layer, 0, j)),
            pl.BlockSpec((bm, LANES), lambda j, i: (i, 0)),
            pl.BlockSpec((bm, LANES), lambda j, i: (i, 0)),
        ],
        out_specs=pl.BlockSpec((bm, bn), lambda j, i: (i, j)),
        out_shape=jax.ShapeDtypeStruct((m, n), F32),
        scratch_shapes=[pltpu.VMEM((d, bn), BF16)],
        compiler_params=_params(("arbitrary", "arbitrary")),
        name="qkv_rope",
    )(h, w_qkv, cos_t, sin_t)


def rope_tables(t_prompt, n_sample):
    half = HEAD_DIM // 2
    inv = ROPE_THETA ** (-jnp.arange(half, dtype=F32) / half)
    pos = jnp.concatenate([jnp.arange(t_prompt), jnp.full((n_sample,), PAST_LEN)]).astype(F32)
    ang = pos[:, None] * inv[None, :]
    cos, sin = jnp.cos(ang), jnp.sin(ang)
    reps = LANES // HEAD_DIM
    return (jnp.tile(jnp.concatenate([cos, cos], axis=1), (1, reps)),
            jnp.tile(jnp.concatenate([-sin, sin], axis=1), (1, reps)))


def _swa_prompt_body(sink_ref, q_ref, kp_ref, kc_ref, vp_ref, vc_ref, wo_ref, o_ref, wob_ref, *, n_kv):
    n = pl.program_id(0)
    blk = WINDOW
    hd = HEAD_DIM
    assert LANES == 2 * hd
    wob_ref[...] = wo_ref[...].astype(BF16)
    nq = GQA_GROUP * blk
    keys = lax.broadcasted_iota(jnp.int32, (2 * blk, nq), 0)
    qrow = lax.broadcasted_iota(jnp.int32, (2 * blk, nq), 1) % blk
    diff = qrow + blk - keys
    mask = (diff >= 0) & (diff < WINDOW) & ((keys >= blk) | (n > 0))
    head_of_col = lax.broadcasted_iota(jnp.int32, (1, nq), 1) // blk
    lane = lax.broadcasted_iota(jnp.int32, (1, LANES), 1)
    log2e = math.log2(math.e)
    qscale = log2e / math.sqrt(hd)
    def mine(kv):
        return (lane >= (kv % 2) * hd) & (lane < (kv % 2 + 1) * hd)

    def scores(kv):
        ps = slice((kv // 2) * LANES, (kv // 2 + 1) * LANES)
        kpair = jnp.concatenate([kp_ref[:, ps], kc_ref[:, ps]], axis=0).astype(BF16)
        parts = []
        for g in range(GQA_GROUP):
            c0 = (kv * GQA_GROUP + g - (g % 2)) * hd
            src = q_ref[:, c0:c0 + LANES] * qscale
            if g % 2 != kv % 2:
                src = pltpu.roll(src, hd, 1)
            parts.append(jnp.where(mine(kv), src, 0.0))
        qs = jnp.concatenate(parts, axis=0).astype(BF16)
        return lax.dot_general(kpair, qs, (((1,), (1,)), ((), ())), preferred_element_type=F32)

    def softmax(kv, s):
        s = jnp.where(mask, s, -jnp.inf)
        sink = jnp.zeros((1, nq), F32)
        for g in range(GQA_GROUP):
            sink = jnp.where(head_of_col == g, sink_ref[kv * GQA_GROUP + g] * log2e, sink)
        mx = jnp.maximum(jnp.max(s, axis=0, keepdims=True), sink)
        return jnp.exp2(s - mx).astype(BF16), jnp.exp2(sink - mx)

    def weighted_values(kv, pb, sink_term):
        ps = slice((kv // 2) * LANES, (kv // 2 + 1) * LANES)
        vpair = jnp.concatenate([vp_ref[:, ps], vc_ref[:, ps]], axis=0)
        vaug = jnp.where(mine(kv), vpair, 1.0).astype(BF16)
        ot = lax.dot_general(vaug, pb, (((0,), (0,)), ((), ())), preferred_element_type=F32)
        half = kv % 2
        other = (1 - half) * hd
        denom = ot[other:other + 1, :] + sink_term
        on = (ot[half * hd:(half + 1) * hd, :] * (1.0 / denom)).astype(o_ref.dtype)
        for g in range(GQA_GROUP):
            r0 = (kv * GQA_GROUP + g) * hd
            o_ref[r0:r0 + hd, :] = on[:, g * blk:(g + 1) * blk]

    ahead = 2
    pending = {kv: scores(kv) for kv in range(min(ahead, n_kv))}
    for kv in range(n_kv):
        pb, sink_term = softmax(kv, pending.pop(kv))
        if kv + ahead < n_kv:
            pending[kv + ahead] = scores(kv + ahead)
        weighted_values(kv, pb, sink_term)


def swa_prompt(qkv, sinks, w_o, layer, t, d):
    n_kv = d // HEAD_DIM // GQA_GROUP
    kvw = n_kv * HEAD_DIM
    nblk = t // WINDOW
    assert d % kvw == 0 and t % WINDOW == 0 and n_kv % 2 == 0 and d % nblk == 0 and (d // nblk) % 16 == 0
    wrows = d // nblk
    kblk = d // kvw
    vblk = kblk + 1
    prev = lambda n: jnp.maximum(n - 1, 0)
    return pl.pallas_call(
        functools.partial(_swa_prompt_body, n_kv=n_kv),
        grid=(nblk,),
        in_specs=[
            pl.BlockSpec(memory_space=pltpu.SMEM),
            pl.BlockSpec((WINDOW, d), lambda n: (n, 0)),
            pl.BlockSpec((WINDOW, kvw), lambda n: (prev(n), kblk)),
            pl.BlockSpec((WINDOW, kvw), lambda n: (n, kblk)),
            pl.BlockSpec((WINDOW, kvw), lambda n: (prev(n), vblk)),
            pl.BlockSpec((WINDOW, kvw), lambda n: (n, vblk)),
            pl.BlockSpec((None, wrows, d), lambda n: (layer, n, 0)),
        ],
        out_specs=[pl.BlockSpec((d, WINDOW), lambda n: (0, n)), pl.BlockSpec((wrows, d), lambda n: (n, 0))],
        out_shape=[jax.ShapeDtypeStruct((d, qkv.shape[0]), BF16), jax.ShapeDtypeStruct((d, d), BF16)],
        compiler_params=_params(("arbitrary",)),
        name="swa_prompt",
    )(sinks, qkv, qkv, qkv, qkv, qkv, w_o)


def _swa_sample_body(sink_ref, qkvt_ref, k_ref, v_ref, o_any, ko_ref, vo_ref, o_ref, acc_ref, *, n_kv, bt):
    del o_any
    i = pl.program_id(0)
    nk = k_ref.shape[1]
    nb = qkvt_ref.shape[1]
    kvw = n_kv * HEAD_DIM
    n_heads = n_kv * GQA_GROUP
    nq = n_heads * HEAD_DIM
    sub = 8

    @pl.when(i == 0)
    def _():
        acc_ref[...] = jnp.zeros_like(acc_ref)

    sample_lane = lax.broadcasted_iota(jnp.int32, (1, nb), 1)
    key_lane = lax.broadcasted_iota(jnp.int32, (1, nk), 1)
    sublane = lax.broadcasted_iota(jnp.int32, (sub, nk), 0)
    head_row = lax.broadcasted_iota(jnp.int32, (n_heads, 1), 0)
    sink = jnp.zeros((n_heads, 1), F32)
    for hd in range(n_heads):
        sink = jnp.where(head_row == hd, sink_ref[hd], sink)
    qt = (qkvt_ref[:nq, :] * (1.0 / math.sqrt(HEAD_DIM))).astype(BF16)
    knt = qkvt_ref[nq:nq + kvw, :]
    vnt = qkvt_ref[nq + kvw:, :]
    gs = bt
    wide_row = lax.broadcasted_iota(jnp.int32, (nb, gs * nk), 0)
    wide_lane_group = lax.broadcasted_iota(jnp.int32, (nb, gs * nk), 1) // nk
    tall_lane_group = lax.broadcasted_iota(jnp.int32, (gs * nk, nb), 0) // nk
    tall_lane = lax.broadcasted_iota(jnp.int32, (gs * nk, nb), 1)
    qcols_of = [jnp.dot(qt, (wide_row == wide_lane_group + (i * bt + g0)).astype(BF16),
                        preferred_element_type=F32) for g0 in range(0, bt, gs)]
    pvs = []
    for j in range(bt):
        bg = i * bt + j
        is_b = sample_lane == bg
        qcols = qcols_of[j // gs][:, (j % gs) * nk:(j % gs + 1) * nk]
        kn = jnp.sum(jnp.where(is_b, knt, 0.0), axis=1, keepdims=True)
        vn = jnp.sum(jnp.where(is_b, vnt, 0.0), axis=1, keepdims=True)
        rows = slice(j * kvw, (j + 1) * kvw)
        kfull = jnp.where(key_lane == nk - 1, kn, pltpu.roll(k_ref[rows, :], nk - 1, 1))
        vfull = jnp.where(key_lane == nk - 1, vn, pltpu.roll(v_ref[rows, :], nk - 1, 1))
        ko_ref[rows, :] = kfull
        vo_ref[rows, :] = vfull
        tiles = []
        for tile in range(n_heads // sub):
            st = jnp.zeros((sub, nk), F32)
            for r in range(sub):
                hidx = tile * sub + r
                kv = hidx // GQA_GROUP
                prod = kfull[kv * HEAD_DIM:(kv + 1) * HEAD_DIM] * qcols[hidx * HEAD_DIM:(hidx + 1) * HEAD_DIM]
                red = jnp.sum(prod.reshape(HEAD_DIM // sub, sub, nk), axis=0)
                step = sub // 2
                while step >= 1:
                    red = red + pltpu.roll(red, step, 0)
                    step //= 2
                st = jnp.where(sublane == r, red, st)
            tiles.append(st)
        s = jnp.concatenate(tiles, axis=0)
        mx = jnp.maximum(jnp.max(s, axis=1, keepdims=True), sink)
        p = jnp.exp(s - mx)
        pn = p / (jnp.sum(p, axis=1, keepdims=True) + jnp.exp(sink - mx))
        pvs.append(jnp.concatenate(
            [vfull[(hidx // GQA_GROUP) * HEAD_DIM:(hidx // GQA_GROUP + 1) * HEAD_DIM] * pn[hidx:hidx + 1, :]
             for hidx in range(n_heads)], axis=0).astype(BF16))
        if len(pvs) == gs:
            first = i * bt + j + 1 - gs
            acc_ref[...] += jnp.dot(jnp.concatenate(pvs, axis=1),
                                    (tall_lane_group + first == tall_lane).astype(BF16),
                                    preferred_element_type=F32)
            pvs = []

    @pl.when(i == pl.num_programs(0) - 1)
    def _():
        o_ref[...] = acc_ref[...].astype(o_ref.dtype)


def swa_sample(qkvt, kt, vt, sinks, o_fm, n_kv):
    nrow, b = qkvt.shape
    kvw = n_kv * HEAD_DIM
    d = nrow - 2 * kvw
    nk = kt.shape[1]
    t = o_fm.shape[1] - b
    assert d == n_kv * GQA_GROUP * HEAD_DIM and (n_kv * GQA_GROUP) % 8 == 0 and t % b == 0
    bt = _pick(b, (8, 4, 2, 1))
    cache_spec = pl.BlockSpec((bt * kvw, nk), lambda i: (i, 0))
    return pl.pallas_call(
        functools.partial(_swa_sample_body, n_kv=n_kv, bt=bt),
        grid=(b // bt,),
        in_specs=[
            pl.BlockSpec(memory_space=pltpu.SMEM),
            pl.BlockSpec((nrow, b), lambda i: (0, 0)),
            cache_spec,
            cache_spec,
            pl.BlockSpec(memory_space=pl.ANY),
        ],
        out_specs=[cache_spec, cache_spec, pl.BlockSpec((d, b), lambda i: (0, t // b))],
        out_shape=[jax.ShapeDtypeStruct(kt.shape, F32), jax.ShapeDtypeStruct(vt.shape, F32),
                   jax.ShapeDtypeStruct(o_fm.shape, o_fm.dtype)],
        scratch_shapes=[pltpu.VMEM((d, b), F32)],
        input_output_aliases={4: 2},
        compiler_params=_params(("arbitrary",)),
        name="swa_sample",
    )(sinks, qkvt, kt, vt, o_fm)


def _pool_prompt_body(x_ref, g_ref, w_ref, sc_ref, gn_ref, o_ref, hn_ref, hl_ref, wb_ref, *, gc):
    i = pl.program_id(0)

    @pl.when(i == 0)
    def _():
        wb_ref[...] = w_ref[...].astype(BF16)
        hl_ref[...] = jnp.zeros_like(hl_ref)

    bm = x_ref.shape[0]
    x = x_ref[...]
    h = _rms(x, g_ref[...])
    hprev = hl_ref[...]
    hl_ref[...] = h[bm - POOL_MAXW:, :]
    pos1 = (lax.broadcasted_iota(jnp.int32, (bm, 1), 0) + i * bm + 1).astype(F32)
    for gi, w in enumerate(POOL_WINDOWS):
        cs = slice(gi * gc, (gi + 1) * gc)
        hg = h[:, cs]
        acc = jnp.concatenate([hprev[:, cs], hg], axis=0)
        step = 1
        while step < w:
            acc = acc + pltpu.roll(acc, step, 0)
            step *= 2
        cnt = jnp.minimum(jnp.float32(w), pos1)
        pooled = (acc[POOL_MAXW:, :] / cnt - hg).astype(BF16)
        y = jnp.dot(pooled, wb_ref[gi], preferred_element_type=F32)
        o_ref[:, cs] = x[:, cs] + y * sc_ref[:, cs]
    hn_ref[...] = _rms(o_ref[...], gn_ref[...]).astype(hn_ref.dtype)


def pool_prompt(x_all, g, w_pool, layer, scale, g_next, t):
    d = x_all.shape[1]
    _, ng, gc, _ = w_pool.shape
    bm = _pick(t, (512, 256, 128, 64, 32, 16))
    return pl.pallas_call(
        functools.partial(_pool_prompt_body, gc=gc),
        grid=(t // bm,),
        in_specs=[
            pl.BlockSpec((bm, d), lambda i: (i, 0)),
            pl.BlockSpec((1, d), lambda i: (0, 0)),
            pl.BlockSpec((None, ng, gc, gc), lambda i: (layer, 0, 0, 0)),
            pl.BlockSpec((1, d), lambda i: (0, 0)),
            pl.BlockSpec((1, d), lambda i: (0, 0)),
        ],
        out_specs=[
            pl.BlockSpec((bm, d), lambda i: (i, 0)),
            pl.BlockSpec((bm, d), lambda i: (i, 0)),
            pl.BlockSpec((POOL_MAXW, d), lambda i: (0, 0)),
        ],
        out_shape=[jax.ShapeDtypeStruct(x_all.shape, F32), jax.ShapeDtypeStruct(x_all.shape, BF16),
                   jax.ShapeDtypeStruct((POOL_MAXW, d), F32)],
        scratch_shapes=[pltpu.VMEM((ng, gc, gc), BF16)],
        input_output_aliases={0: 0},
        compiler_params=_params(("arbitrary",)),
        name="pool_prompt",
    )(x_all, g.reshape(1, d), w_pool, scale.reshape(1, d), g_next.reshape(1, d))


def _pool_sample_body(x_ref, st_ref, g_ref, w_ref, sc_ref, gn_ref, hn_any, o_ref, hn_ref, ns_ref, *, gc):
    del hn_any
    x = x_ref[...]
    h = _rms(x, g_ref[...])
    ns_ref[:POOL_BUF - 1] = st_ref[1:]
    ns_ref[POOL_BUF - 1] = h
    for gi, w in enumerate(POOL_WINDOWS):
        cs = slice(gi * gc, (gi + 1) * gc)
        hg = h[:, cs]
        tot = hg + jnp.sum(st_ref[POOL_BUF - (w - 1):, :, cs], axis=0)
        cnt = float(min(w, PAST_LEN + 1))
        pooled = (tot / cnt - hg).astype(BF16)
        y = jnp.dot(pooled, w_ref[gi].astype(BF16), preferred_element_type=F32)
        o_ref[:, cs] = x[:, cs] + y * sc_ref[:, cs]
    hn_ref[...] = _rms(o_ref[...], gn_ref[...]).astype(hn_ref.dtype)


def pool_sample(x_all, hn_all, state, layer, g, w_pool, scale, g_next, t):
    d = x_all.shape[1]
    b = state.shape[2]
    _, ng, gc, _ = w_pool.shape
    bt = _pick(b, (32, 16, 8))
    assert t % bt == 0
    off = t // bt
    return pl.pallas_call(
        functools.partial(_pool_sample_body, gc=gc),
        grid=(b // bt,),
        in_specs=[
            pl.BlockSpec((bt, d), lambda i: (off + i, 0)),
            pl.BlockSpec((None, POOL_BUF, bt, d), lambda i: (layer, 0, i, 0)),
            pl.BlockSpec((1, d), lambda i: (0, 0)),
            pl.BlockSpec((None, ng, gc, gc), lambda i: (layer, 0, 0, 0)),
            pl.BlockSpec((1, d), lambda i: (0, 0)),
            pl.BlockSpec((1, d), lambda i: (0, 0)),
            pl.BlockSpec(memory_space=pl.ANY),
        ],
        out_specs=[
            pl.BlockSpec((bt, d), lambda i: (off + i, 0)),
            pl.BlockSpec((bt, d), lambda i: (off + i, 0)),
            pl.BlockSpec((POOL_BUF, bt, d), lambda i: (0, i, 0)),
        ],
        out_shape=[jax.ShapeDtypeStruct(x_all.shape, F32), jax.ShapeDtypeStruct(hn_all.shape, hn_all.dtype),
                   jax.ShapeDtypeStruct((POOL_BUF, b, d), F32)],
        input_output_aliases={0: 0, 6: 1},
        compiler_params=_params(("arbitrary",)),
        name="pool_sample",
    )(x_all, state, g.reshape(1, d), w_pool, scale.reshape(1, d), g_next.reshape(1, d), hn_all)


def _mem_kv_body(m_ref, g_ref, w_ref, o_ref):
    h = _rms(m_ref[...], g_ref[...]).astype(BF16)
    o_ref[...] = jnp.dot(h, w_ref[...].astype(BF16), preferred_element_type=F32)


def mem_kv(mem, g, w_kv, layer):
    m, d = mem.shape
    n = w_kv.shape[2]
    bn = _pick(n, (512, 256, 128))
    return pl.pallas_call(
        _mem_kv_body,
        grid=(n // bn,),
        in_specs=[
            pl.BlockSpec((m, d), lambda j: (0, 0)),
            pl.BlockSpec((1, d), lambda j: (0, 0)),
            pl.BlockSpec((None, d, bn), lambda j: (layer, 0, j)),
        ],
        out_specs=pl.BlockSpec((m, bn), lambda j: (0, j)),
        out_shape=jax.ShapeDtypeStruct((m, n), F32),
        compiler_params=_params(("arbitrary",)),
        name="mem_kv",
    )(mem, g.reshape(1, d), w_kv)


def _xattn_prompt_body(h_ref, wq_ref, kv_ref, wo_ref, x_ref, gn_ref, o_ref, hn_ref, wqb_ref, wob_ref, kvb_ref):
    @pl.when(pl.program_id(0) == 0)
    def _():
        wqb_ref[...] = wq_ref[...].astype(BF16)
        wob_ref[...] = wo_ref[...].astype(BF16)
        kvb_ref[...] = kv_ref[...].astype(BF16)

    xw = X_HEADS * X_HEAD_DIM
    bm = h_ref.shape[0]
    nchunk = 2 if bm % 32 == 0 else 1
    rows = [slice(c * (bm // nchunk), (c + 1) * (bm // nchunk)) for c in range(nchunk)]
    qs = [jnp.dot(h_ref[r, :], wqb_ref[...], preferred_element_type=F32) / math.sqrt(X_HEAD_DIM) for r in rows]
    scores = [[lax.dot_general(q[:, hd * X_HEAD_DIM:(hd + 1) * X_HEAD_DIM].astype(BF16),
                               kvb_ref[:, hd * X_HEAD_DIM:(hd + 1) * X_HEAD_DIM], (((1,), (1,)), ((), ())),
                               preferred_element_type=F32) for hd in range(X_HEADS)] for q in qs]
    attn = []
    for c in range(nchunk):
        outs = []
        for hd in range(X_HEADS):
            s = scores[c][hd]
            p = jnp.exp(s - jnp.max(s, axis=-1, keepdims=True))
            l = jnp.sum(p, axis=-1, keepdims=True)
            vs = slice(xw + hd * X_HEAD_DIM, xw + (hd + 1) * X_HEAD_DIM)
            outs.append(jnp.dot(p.astype(BF16), kvb_ref[:, vs], preferred_element_type=F32) / l)
        attn.append(jnp.concatenate(outs, axis=1).astype(BF16))
    for c, r in enumerate(rows):
        xn = x_ref[r, :] + jnp.dot(attn[c], wob_ref[...], preferred_element_type=F32)
        o_ref[r, :] = xn
        hn_ref[r, :] = _rms(xn, gn_ref[...]).astype(hn_ref.dtype)


def xattn_prompt(hq_all, w_q, mkv, w_o, layer, x_all, g_next, t):
    d = x_all.shape[1]
    xw = w_q.shape[2]
    nm = mkv.shape[0]
    bm = _pick(t, (512, 256, 128, 64, 32, 16))
    return pl.pallas_call(
        _xattn_prompt_body,
        grid=(t // bm,),
        in_specs=[
            pl.BlockSpec((bm, d), lambda i: (i, 0)),
            pl.BlockSpec((None, d, xw), lambda i: (layer, 0, 0)),
            pl.BlockSpec((nm, 2 * xw), lambda i: (0, 0)),
            pl.BlockSpec((None, xw, d), lambda i: (layer, 0, 0)),
            pl.BlockSpec((bm, d), lambda i: (i, 0)),
            pl.BlockSpec((1, d), lambda i: (0, 0)),
        ],
        out_specs=[pl.BlockSpec((bm, d), lambda i: (i, 0)), pl.BlockSpec((bm, d), lambda i: (i, 0))],
        out_shape=[jax.ShapeDtypeStruct(x_all.shape, F32), jax.ShapeDtypeStruct(x_all.shape, BF16)],
        scratch_shapes=[pltpu.VMEM((d, xw), BF16), pltpu.VMEM((xw, d), BF16), pltpu.VMEM((nm, 2 * xw), BF16)],
        input_output_aliases={4: 0},
        compiler_params=_params(("arbitrary",)),
        name="xattn_prompt",
    )(hq_all, w_q, mkv, w_o, x_all, g_next.reshape(1, d))


def _xq_sample_body(h_ref, wq_ref, o_ref):
    o_ref[...] = jnp.dot(h_ref[...], wq_ref[...].astype(BF16), preferred_element_type=F32) / math.sqrt(X_HEAD_DIM)


def xq_sample(hq_all, w_q, layer, t, b):
    d = hq_all.shape[1]
    xw = w_q.shape[2]
    assert t % b == 0
    return pl.pallas_call(
        _xq_sample_body,
        grid=(1,),
        in_specs=[
            pl.BlockSpec((b, d), lambda i: (t // b, 0)),
            pl.BlockSpec((None, d, xw), lambda i: (layer, 0, 0)),
        ],
        out_specs=pl.BlockSpec((b, xw), lambda i: (0, 0)),
        out_shape=jax.ShapeDtypeStruct((b, xw), F32),
        compiler_params=_params(("arbitrary",)),
        name="xq_sample",
    )(hq_all, w_q)


def _xattn_core_body(q_ref, k_ref, v_ref, ones_ref, o_ref):
    bt, sub, hd = q_ref.shape
    nv = k_ref.shape[0] // (bt * sub)
    k = k_ref[...].reshape(bt, nv, sub, hd)
    prod = (k * q_ref[...][:, None]).reshape(bt * nv * sub, hd).astype(BF16)
    s = jnp.dot(prod, ones_ref[...], preferred_element_type=F32).reshape(bt, nv, sub, hd)

    def fold(x, op):
        step = X_HEADS
        while step < sub:
            x = op(x, pltpu.roll(x, step, 2))
            step *= 2
        return x

    mx = fold(jnp.max(s, axis=1, keepdims=True), jnp.maximum)
    p = jnp.exp(s - mx)
    l = fold(jnp.sum(p, axis=1, keepdims=True), jnp.add)
    acc = fold(jnp.sum(p * v_ref[...].reshape(bt, nv, sub, hd), axis=1, keepdims=True), jnp.add)
    o_ref[...] = (acc / l).reshape(bt, sub, hd)


def xattn_core_sample(q_s, mk_flat, mv_flat, layer, nm):
    b = q_s.shape[0]
    hd = X_HEAD_DIM
    sub = 8
    assert sub % X_HEADS == 0 and nm % (sub // X_HEADS) == 0
    bt = _pick(b, (8, 4, 2, 1))
    nblk = b // bt
    rows = bt * nm * X_HEADS
    q8 = jnp.tile(q_s.reshape(b, X_HEADS, hd), (1, sub // X_HEADS, 1))
    o8 = pl.pallas_call(
        _xattn_core_body,
        grid=(nblk,),
        in_specs=[
            pl.BlockSpec((bt, sub, hd), lambda i: (i, 0, 0)),
            pl.BlockSpec((rows, hd), lambda i: (layer * nblk + i, 0)),
            pl.BlockSpec((rows, hd), lambda i: (layer * nblk + i, 0)),
            pl.BlockSpec((hd, hd), lambda i: (0, 0)),
        ],
        out_specs=pl.BlockSpec((bt, sub, hd), lambda i: (i, 0, 0)),
        out_shape=jax.ShapeDtypeStruct((b, sub, hd), F32),
        compiler_params=_params(("arbitrary",)),
        name="xattn_core_sample",
    )(q8, mk_flat, mv_flat, jnp.ones((hd, hd), BF16))
    return o8[:, :X_HEADS].reshape(b, X_HEADS * hd)


def _xout_sample_body(a_ref, wo_ref, x_ref, gn_ref, hn_any, o_ref, hn_ref):
    del hn_any
    xn = x_ref[...] + jnp.dot(a_ref[...].astype(BF16), wo_ref[...].astype(BF16), preferred_element_type=F32)
    o_ref[...] = xn
    hn_ref[...] = _rms(xn, gn_ref[...]).astype(hn_ref.dtype)


def xout_sample(o_s, w_o, layer, x_all, hn_all, g_next, t):
    d = x_all.shape[1]
    b, xw = o_s.shape
    assert t % b == 0
    off = t // b
    return pl.pallas_call(
        _xout_sample_body,
        grid=(1,),
        in_specs=[
            pl.BlockSpec((b, xw), lambda i: (0, 0)),
            pl.BlockSpec((None, xw, d), lambda i: (layer, 0, 0)),
            pl.BlockSpec((b, d), lambda i: (off, 0)),
            pl.BlockSpec((1, d), lambda i: (0, 0)),
            pl.BlockSpec(memory_space=pl.ANY),
        ],
        out_specs=[pl.BlockSpec((b, d), lambda i: (off, 0)), pl.BlockSpec((b, d), lambda i: (off, 0))],
        out_shape=[jax.ShapeDtypeStruct(x_all.shape, F32), jax.ShapeDtypeStruct(hn_all.shape, hn_all.dtype)],
        input_output_aliases={2: 0, 4: 1},
        compiler_params=_params(("arbitrary",)),
        name="xout_sample",
    )(o_s, w_o, x_all, g_next.reshape(1, d), hn_all)


def kernel(x_prompt, x_sample, state_pool, cache_swa_k, cache_swa_v, cache_mem_k, cache_mem_v, mem_prompt,
           g_ffn1, w_ffn1_gu, w_ffn1_dn, g_mix, w_pool, pool_scale, w_qkv, w_o, sinks,
           g_xq, g_mem, w_xq, w_xkv, w_xo, g_ffn2, w_ffn2_gu, w_ffn2_dn, g_final):
    bp, t, d = x_prompt.shape
    b, s_len, _ = x_sample.shape
    assert bp == 1 and s_len == 1
    depth = g_ffn1.shape[0]
    xw = w_xq.shape[2]
    nm = mem_prompt.shape[1]
    n_kv = d // HEAD_DIM // GQA_GROUP
    kvw = n_kv * HEAD_DIM

    x = jnp.concatenate([x_prompt[0], x_sample[:, 0]], axis=0)
    cos_t, sin_t = rope_tables(t, b)
    mk_flat = cache_mem_k.reshape(-1, X_HEAD_DIM)
    mv_flat = cache_mem_v.reshape(-1, X_HEAD_DIM)
    state_sm = jnp.swapaxes(state_pool, 1, 2)

    pool_p, pool_s = [], []
    swa_kp, swa_vp, swa_ks, swa_vs = [], [], [], []
    mem_kp, mem_vp = [], []
    h = rms_norm_rows(x, g_ffn1[0], BF16)
    y_p = y_s = None
    for layer in range(depth):
        i = layer // 2
        act, wdb = gate_up(h, w_ffn1_gu, w_ffn1_dn, layer)
        x, h = down_norm(act, wdb, x, g_mix[layer], BF16)
        if layer % 2 == 0:
            x, hq, h_last = pool_prompt(x, g_mix[layer], w_pool, i, pool_scale[i], g_xq[layer], t)
            x, hq, new_state = pool_sample(x, hq, state_sm, i, g_mix[layer], w_pool, pool_scale[i], g_xq[layer], t)
            pool_p.append(h_last[None, POOL_MAXW - POOL_BUF:])
            pool_s.append(jnp.swapaxes(new_state, 0, 1))
        else:
            qkv = qkv_rope(h, w_qkv, i, cos_t, sin_t, d + kvw, kvw)
            buf = cache_swa_k.shape[2]
            keep = min(WINDOW, t)
            swa_kp.append(qkv[t - keep:t, d:d + kvw].reshape(1, keep, n_kv, HEAD_DIM))
            swa_vp.append(qkv[t - keep:t, d + kvw:].reshape(1, keep, n_kv, HEAD_DIM))
            assert buf == WINDOW
            to_fm = lambda c: jnp.transpose(c, (0, 2, 3, 1)).reshape(b * kvw, buf)
            from_fm = lambda c: jnp.transpose(c.reshape(b, n_kv, HEAD_DIM, buf), (0, 3, 1, 2))
            o_fm, wo_bf16 = swa_prompt(qkv, sinks[i], w_o, i, t, d)
            ks_fm, vs_fm, o_fm = swa_sample(qkv[t:].T, to_fm(cache_swa_k[i]), to_fm(cache_swa_v[i]), sinks[i],
                                            o_fm, n_kv)
            swa_ks.append(from_fm(ks_fm))
            swa_vs.append(from_fm(vs_fm))
            x, hq = proj_residual_norm(o_fm, wo_bf16, x, g_xq[layer])
        mkv = mem_kv(mem_prompt[0], g_mem[layer], w_xkv, layer)
        mem_kp.append(mkv[:, :xw].reshape(1, nm, X_HEADS, X_HEAD_DIM))
        mem_vp.append(mkv[:, xw:].reshape(1, nm, X_HEADS, X_HEAD_DIM))
        x, h = xattn_prompt(hq, w_xq, mkv, w_xo, layer, x, g_ffn2[layer], t)
        o_s = xattn_core_sample(xq_sample(hq, w_xq, layer, t, b), mk_flat, mv_flat, layer, nm)
        x, h = xout_sample(o_s, w_xo, layer, x, h, g_ffn2[layer], t)
        act, wdb = gate_up(h, w_ffn2_gu, w_ffn2_dn, layer)
        if layer + 1 < depth:
            x, h = down_norm(act, wdb, x, g_ffn1[layer + 1], BF16)
        else:
            (y_p,) = down_norm(act, wdb, x, g_final, F32, emit_x=False, row0=0, nrows=t)
            (y_s,) = down_norm(act, wdb, x, g_final, F32, emit_x=False, row0=t, nrows=b)
    return (y_p[None], y_s[:, None], jnp.stack(pool_p), jnp.stack(pool_s), jnp.stack(swa_kp), jnp.stack(swa_vp),
            jnp.stack(swa_ks), jnp.stack(swa_vs), jnp.stack(mem_kp), jnp.stack(mem_vp))
```

```python
import functools
import math

import jax
import jax.numpy as jnp
from jax import lax
from jax.experimental import pallas as pl
from jax.experimental.pallas import tpu as pltpu

F32 = jnp.float32
BF16 = jnp.bfloat16

RMS_EPS = 1e-6
PAST_LEN = 8192
POOL_WINDOWS = (2, 4, 8, 16)
POOL_MAXW = max(POOL_WINDOWS)
POOL_BUF = POOL_MAXW - 1
HEAD_DIM = 64
GQA_GROUP = 4
WINDOW = 128
ROPE_THETA = 10000.0
X_HEADS = 4
X_HEAD_DIM = 128
LANES = 128
V7X_VMEM_LIMIT = 60 * 1024 * 1024


def _params(sem):
    return pltpu.CompilerParams(dimension_semantics=sem, vmem_limit_bytes=V7X_VMEM_LIMIT)


def _pick(n, candidates):
    for c in candidates:
        if n % c == 0:
            return c
    raise ValueError(f"no block size in {candidates} divides {n}")


def _rms(x, g):
    return x * lax.rsqrt(jnp.mean(x * x, axis=-1, keepdims=True) + RMS_EPS) * g


def _stack_norm_body(x_ref, g_ref, *refs):
    xo_ref, h_ref = refs[-2:]
    x = x_ref[...]
    xo_ref[...] = x
    h_ref[...] = _rms(x, g_ref[...]).astype(h_ref.dtype)


def stack_and_norm(x_p, x_s, g):
    t, d = x_p.shape
    b = x_s.shape[0]
    assert t % b == 0
    bm = _pick(t, (512, 256, 128, 64, 32, 16))
    g2 = g.reshape(1, d)
    shapes = [jax.ShapeDtypeStruct((t + b, d), F32), jax.ShapeDtypeStruct((t + b, d), BF16)]
    x_all, h_all = pl.pallas_call(
        _stack_norm_body,
        grid=(t // bm,),
        in_specs=[pl.BlockSpec((bm, d), lambda i: (i, 0)), pl.BlockSpec((1, d), lambda i: (0, 0))],
        out_specs=[pl.BlockSpec((bm, d), lambda i: (i, 0)), pl.BlockSpec((bm, d), lambda i: (i, 0))],
        out_shape=shapes,
        compiler_params=_params(("arbitrary",)),
        name="stack_norm_prompt",
    )(x_p, g2)
    return pl.pallas_call(
        _stack_norm_body,
        grid=(1,),
        in_specs=[pl.BlockSpec((b, d), lambda i: (0, 0)), pl.BlockSpec((1, d), lambda i: (0, 0)),
                  pl.BlockSpec(memory_space=pl.ANY), pl.BlockSpec(memory_space=pl.ANY)],
        out_specs=[pl.BlockSpec((b, d), lambda i: (t // b, 0)), pl.BlockSpec((b, d), lambda i: (t // b, 0))],
        out_shape=shapes,
        input_output_aliases={2: 0, 3: 1},
        compiler_params=_params(("arbitrary",)),
        name="stack_norm_sample",
    )(x_s, g2, x_all, h_all)


def _gate_up_body(h_ref, wg_ref, wu_ref, wd_ref, o_ref, wdb_ref, wb_ref, *, bf):
    @pl.when(pl.program_id(1) == 0)
    def _():
        wb_ref[:, :bf] = wg_ref[...].astype(BF16)
        wb_ref[:, bf:] = wu_ref[...].astype(BF16)
        wdb_ref[...] = wd_ref[...].astype(BF16)

    bm = h_ref.shape[0]
    nchunk = 4 if bm % 64 == 0 else 1
    cr = bm // nchunk
    rs = [jnp.dot(h_ref[c * cr:(c + 1) * cr, :], wb_ref[...], preferred_element_type=F32) for c in range(nchunk)]
    for c in range(nchunk):
        a = rs[c][:, :bf]
        b = rs[c][:, bf:]
        o_ref[c * cr:(c + 1) * cr, :] = (a / (1.0 + jnp.exp(-a)) * b * 0.5).astype(o_ref.dtype)


def gate_up(h, w_gu, w_dn, layer):
    m, d = h.shape
    f = w_gu.shape[2] // 2
    bf = _pick(f, (512, 256, 128))
    bm = _pick(m, (1664, 1024, 512, 256, 128, 64, 32, 16))
    nf = f // bf
    return pl.pallas_call(
        functools.partial(_gate_up_body, bf=bf),
        grid=(nf, m // bm),
        in_specs=[
            pl.BlockSpec((bm, d), lambda j, i: (i, 0)),
            pl.BlockSpec((None, d, bf), lambda j, i: (layer, 0, j)),
            pl.BlockSpec((None, d, bf), lambda j, i: (layer, 0, j + nf)),
            pl.BlockSpec((None, bf, d), lambda j, i: (layer, j, 0)),
        ],
        out_specs=[
            pl.BlockSpec((bm, bf), lambda j, i: (i, j)),
            pl.BlockSpec((bf, d), lambda j, i: (j, 0)),
        ],
        out_shape=[jax.ShapeDtypeStruct((m, f), BF16), jax.ShapeDtypeStruct((f, d), BF16)],
        scratch_shapes=[pltpu.VMEM((d, 2 * bf), BF16)],
        compiler_params=_params(("arbitrary", "arbitrary")),
        name="gate_up",
    )(h, w_gu, w_gu, w_dn)


def _down_body(a_ref, w_ref, x_ref, g_ref, *out_refs, emit_x):
    xn = x_ref[...] + jnp.dot(a_ref[...], w_ref[...], preferred_element_type=F32)
    if emit_x:
        out_refs[0][...] = xn
    h_ref = out_refs[-1]
    h_ref[...] = _rms(xn, g_ref[...]).astype(h_ref.dtype)


def down_norm(a, w_bf16, x, g_next, h_dtype, emit_x=True, row0=0, nrows=None):
    m, k = a.shape
    d = w_bf16.shape[1]
    nrows = m if nrows is None else nrows
    assert emit_x is False or (row0 == 0 and nrows == m)
    bm = _pick(math.gcd(nrows, row0) if row0 else nrows, (320, 256, 128, 64, 32, 16))
    off = row0 // bm
    row = lambda i: (i, 0)
    src = lambda i: (i + off, 0)
    out_specs = [pl.BlockSpec((bm, d), row)]
    out_shape = [jax.ShapeDtypeStruct((nrows, d), h_dtype)]
    if emit_x:
        out_specs.insert(0, pl.BlockSpec((bm, d), row))
        out_shape.insert(0, jax.ShapeDtypeStruct((m, d), F32))
    return pl.pallas_call(
        functools.partial(_down_body, emit_x=emit_x),
        grid=(nrows // bm,),
        in_specs=[
            pl.BlockSpec((bm, k), src),
            pl.BlockSpec((k, d), lambda i: (0, 0), pipeline_mode=pl.Buffered(1)),
            pl.BlockSpec((bm, d), src),
            pl.BlockSpec((1, d), lambda i: (0, 0)),
        ],
        out_specs=out_specs,
        out_shape=out_shape,
        input_output_aliases={2: 0} if emit_x else {},
        compiler_params=_params(("arbitrary",)),
        name="down_norm",
    )(a, w_bf16, x, g_next.reshape(1, d))


def _proj_res_body(a_ref, w_ref, x_ref, g_ref, xo_ref, h_ref):
    xn = x_ref[...] + lax.dot_general(a_ref[...], w_ref[...], (((0,), (0,)), ((), ())),
                                      preferred_element_type=F32)
    xo_ref[...] = xn
    h_ref[...] = _rms(xn, g_ref[...]).astype(h_ref.dtype)


def proj_residual_norm(a_fm, w_bf16, x, g_next):
    k, m = a_fm.shape
    d = w_bf16.shape[1]
    bm = _pick(m, (640, 512, 256, 128))
    row = lambda i: (i, 0)
    return pl.pallas_call(
        _proj_res_body,
        grid=(m // bm,),
        in_specs=[
            pl.BlockSpec((k, bm), lambda i: (0, i)),
            pl.BlockSpec((k, d), lambda i: (0, 0), pipeline_mode=pl.Buffered(1)),
            pl.BlockSpec((bm, d), row),
            pl.BlockSpec((1, d), lambda i: (0, 0)),
        ],
        out_specs=[pl.BlockSpec((bm, d), row), pl.BlockSpec((bm, d), row)],
        out_shape=[jax.ShapeDtypeStruct((m, d), F32), jax.ShapeDtypeStruct((m, d), BF16)],
        input_output_aliases={2: 0},
        compiler_params=_params(("arbitrary",)),
        name="proj_residual_norm",
    )(a_fm, w_bf16, x, g_next.reshape(1, d))


def _qkv_body(h_ref, w_ref, c_ref, s_ref, o_ref, wb_ref, *, n_rope_tiles):
    @pl.when(pl.program_id(1) == 0)
    def _():
        wb_ref[...] = w_ref[...].astype(BF16)

    bm, bn = o_ref.shape
    is_rope = pl.program_id(0) < n_rope_tiles
    nchunk = 4 if bm % 64 == 0 else 1
    cr = bm // nchunk
    rs = [jnp.dot(h_ref[c * cr:(c + 1) * cr, :], wb_ref[...], preferred_element_type=F32) for c in range(nchunk)]
    lane = lax.broadcasted_iota(jnp.int32, (cr, LANES), 1)
    first_half = (lane % HEAD_DIM) < (HEAD_DIM // 2)
    for c in range(nchunk):
        rows = slice(c * cr, (c + 1) * cr)
        cos = c_ref[rows, :]
        sin = s_ref[rows, :]
        for ci in range(bn // LANES):
            blk = rs[c][:, ci * LANES:(ci + 1) * LANES]
            partner = jnp.where(first_half,
                                pltpu.roll(blk, LANES - HEAD_DIM // 2, 1),
                                pltpu.roll(blk, HEAD_DIM // 2, 1))
            o_ref[rows, ci * LANES:(ci + 1) * LANES] = jnp.where(is_rope, blk * cos + partner * sin, blk)


def qkv_rope(h, w_qkv, layer, cos_t, sin_t, n_rope_cols, bn):
    m, d = h.shape
    n = w_qkv.shape[2]
    assert n % bn == 0 and n_rope_cols % bn == 0 and bn % LANES == 0
    bm = _pick(m, (1664, 1024, 512, 256, 128, 64, 32, 16))
    return pl.pallas_call(
        functools.partial(_qkv_body, n_rope_tiles=n_rope_cols // bn),
        grid=(n // bn, m // bm),
        in_specs=[
            pl.BlockSpec((bm, d), lambda j, i: (i, 0)),
            pl.BlockSpec((None, d, bn), lambda j, i: (layer, 0, j)),
            pl.BlockSpec((bm, LANES), lambda j, i: (i, 0)),
            pl.BlockSpec((bm, LANES), lambda j, i: (i, 0)),
        ],
        out_specs=pl.BlockSpec((bm, bn), lambda j, i: (i, j)),
        out_shape=jax.ShapeDtypeStruct((m, n), F32),
        scratch_shapes=[pltpu.VMEM((d, bn), BF16)],
        compiler_params=_params(("arbitrary", "arbitrary")),
        name="qkv_rope",
    )(h, w_qkv, cos_t, sin_t)


def rope_tables(t_prompt, n_sample):
    half = HEAD_DIM // 2
    inv = ROPE_THETA ** (-jnp.arange(half, dtype=F32) / half)
    pos = jnp.concatenate([jnp.arange(t_prompt), jnp.full((n_sample,), PAST_LEN)]).astype(F32)
    ang = pos[:, None] * inv[None, :]
    cos, sin = jnp.cos(ang), jnp.sin(ang)
    reps = LANES // HEAD_DIM
    return (jnp.tile(jnp.concatenate([cos, cos], axis=1), (1, reps)),
            jnp.tile(jnp.concatenate([-sin, sin], axis=1), (1, reps)))


def _swa_prompt_body(sink_ref, q_ref, kp_ref, kc_ref, vp_ref, vc_ref, wo_ref, o_ref, wob_ref, *, n_kv):
    n = pl.program_id(0)
    blk = WINDOW
    hd = HEAD_DIM
    assert LANES == 2 * hd
    wob_ref[...] = wo_ref[...].astype(BF16)
    nq = GQA_GROUP * blk
    keys = lax.broadcasted_iota(jnp.int32, (2 * blk, nq), 0)
    qrow = lax.broadcasted_iota(jnp.int32, (2 * blk, nq), 1) % blk
    diff = qrow + blk - keys
    mask = (diff >= 0) & (diff < WINDOW) & ((keys >= blk) | (n > 0))
    head_of_col = lax.broadcasted_iota(jnp.int32, (1, nq), 1) // blk
    lane = lax.broadcasted_iota(jnp.int32, (1, LANES), 1)
    log2e = math.log2(math.e)
    qscale = log2e / math.sqrt(hd)
    def mine(kv):
        return (lane >= (kv % 2) * hd) & (lane < (kv % 2 + 1) * hd)

    def scores(kv):
        ps = slice((kv // 2) * LANES, (kv // 2 + 1) * LANES)
        kpair = jnp.concatenate([kp_ref[:, ps], kc_ref[:, ps]], axis=0).astype(BF16)
        parts = []
        for g in range(GQA_GROUP):
            c0 = (kv * GQA_GROUP + g - (g % 2)) * hd
            src = q_ref[:, c0:c0 + LANES] * qscale
            if g % 2 != kv % 2:
                src = pltpu.roll(src, hd, 1)
            parts.append(jnp.where(mine(kv), src, 0.0))
        qs = jnp.concatenate(parts, axis=0).astype(BF16)
        return lax.dot_general(kpair, qs, (((1,), (1,)), ((), ())), preferred_element_type=F32)

    def softmax(kv, s):
        s = jnp.where(mask, s, -jnp.inf)
        sink = jnp.zeros((1, nq), F32)
        for g in range(GQA_GROUP):
            sink = jnp.where(head_of_col == g, sink_ref[kv * GQA_GROUP + g] * log2e, sink)
        mx = jnp.maximum(jnp.max(s, axis=0, keepdims=True), sink)
        return jnp.exp2(s - mx).astype(BF16), jnp.exp2(sink - mx)

    def weighted_values(kv, pb, sink_term):
        ps = slice((kv // 2) * LANES, (kv // 2 + 1) * LANES)
        vpair = jnp.concatenate([vp_ref[:, ps], vc_ref[:, ps]], axis=0)
        vaug = jnp.where(mine(kv), vpair, 1.0).astype(BF16)
        ot = lax.dot_general(vaug, pb, (((0,), (0,)), ((), ())), preferred_element_type=F32)
        half = kv % 2
        other = (1 - half) * hd
        denom = ot[other:other + 1, :] + sink_term
        on = (ot[half * hd:(half + 1) * hd, :] * (1.0 / denom)).astype(o_ref.dtype)
        for g in range(GQA_GROUP):
            r0 = (kv * GQA_GROUP + g) * hd
            o_ref[r0:r0 + hd, :] = on[:, g * blk:(g + 1) * blk]

    ahead = 2
    pending = {kv: scores(kv) for kv in range(min(ahead, n_kv))}
    for kv in range(n_kv):
        pb, sink_term = softmax(kv, pending.pop(kv))
        if kv + ahead < n_kv:
            pending[kv + ahead] = scores(kv + ahead)
        weighted_values(kv, pb, sink_term)


def swa_prompt(qkv, sinks, w_o, layer, t, d):
    n_kv = d // HEAD_DIM // GQA_GROUP
    kvw = n_kv * HEAD_DIM
    nblk = t // WINDOW
    assert d % kvw == 0 and t % WINDOW == 0 and n_kv % 2 == 0 and d % nblk == 0 and (d // nblk) % 16 == 0
    wrows = d // nblk
    kblk = d // kvw
    vblk = kblk + 1
    prev = lambda n: jnp.maximum(n - 1, 0)
    return pl.pallas_call(
        functools.partial(_swa_prompt_body, n_kv=n_kv),
        grid=(nblk,),
        in_specs=[
            pl.BlockSpec(memory_space=pltpu.SMEM),
            pl.BlockSpec((WINDOW, d), lambda n: (n, 0)),
            pl.BlockSpec((WINDOW, kvw), lambda n: (prev(n), kblk)),
            pl.BlockSpec((WINDOW, kvw), lambda n: (n, kblk)),
            pl.BlockSpec((WINDOW, kvw), lambda n: (prev(n), vblk)),
            pl.BlockSpec((WINDOW, kvw), lambda n: (n, vblk)),
            pl.BlockSpec((None, wrows, d), lambda n: (layer, n, 0)),
        ],
        out_specs=[pl.BlockSpec((d, WINDOW), lambda n: (0, n)), pl.BlockSpec((wrows, d), lambda n: (n, 0))],
        out_shape=[jax.ShapeDtypeStruct((d, qkv.shape[0]), BF16), jax.ShapeDtypeStruct((d, d), BF16)],
        compiler_params=_params(("arbitrary",)),
        name="swa_prompt",
    )(sinks, qkv, qkv, qkv, qkv, qkv, w_o)


def _swa_sample_body(sink_ref, qkvt_ref, k_ref, v_ref, o_any, ko_ref, vo_ref, o_ref, acc_ref, *, n_kv, bt):
    del o_any
    i = pl.program_id(0)
    nk = k_ref.shape[1]
    nb = qkvt_ref.shape[1]
    kvw = n_kv * HEAD_DIM
    n_heads = n_kv * GQA_GROUP
    nq = n_heads * HEAD_DIM
    sub = 8

    @pl.when(i == 0)
    def _():
        acc_ref[...] = jnp.zeros_like(acc_ref)

    sample_lane = lax.broadcasted_iota(jnp.int32, (1, nb), 1)
    key_lane = lax.broadcasted_iota(jnp.int32, (1, nk), 1)
    sublane = lax.broadcasted_iota(jnp.int32, (sub, nk), 0)
    head_row = lax.broadcasted_iota(jnp.int32, (n_heads, 1), 0)
    sink = jnp.zeros((n_heads, 1), F32)
    for hd in range(n_heads):
        sink = jnp.where(head_row == hd, sink_ref[hd], sink)
    qt = (qkvt_ref[:nq, :] * (1.0 / math.sqrt(HEAD_DIM))).astype(BF16)
    knt = qkvt_ref[nq:nq + kvw, :]
    vnt = qkvt_ref[nq + kvw:, :]
    gs = bt
    wide_row = lax.broadcasted_iota(jnp.int32, (nb, gs * nk), 0)
    wide_lane_group = lax.broadcasted_iota(jnp.int32, (nb, gs * nk), 1) // nk
    tall_lane_group = lax.broadcasted_iota(jnp.int32, (gs * nk, nb), 0) // nk
    tall_lane = lax.broadcasted_iota(jnp.int32, (gs * nk, nb), 1)
    qcols_of = [jnp.dot(qt, (wide_row == wide_lane_group + (i * bt + g0)).astype(BF16),
                        preferred_element_type=F32) for g0 in range(0, bt, gs)]
    pvs = []
    for j in range(bt):
        bg = i * bt + j
        is_b = sample_lane == bg
        qcols = qcols_of[j // gs][:, (j % gs) * nk:(j % gs + 1) * nk]
        kn = jnp.sum(jnp.where(is_b, knt, 0.0), axis=1, keepdims=True)
        vn = jnp.sum(jnp.where(is_b, vnt, 0.0), axis=1, keepdims=True)
        rows = slice(j * kvw, (j + 1) * kvw)
        kfull = jnp.where(key_lane == nk - 1, kn, pltpu.roll(k_ref[rows, :], nk - 1, 1))
        vfull = jnp.where(key_lane == nk - 1, vn, pltpu.roll(v_ref[rows, :], nk - 1, 1))
        ko_ref[rows, :] = kfull
        vo_ref[rows, :] = vfull
        tiles = []
        for tile in range(n_heads // sub):
            st = jnp.zeros((sub, nk), F32)
            for r in range(sub):
                hidx = tile * sub + r
                kv = hidx // GQA_GROUP
                prod = kfull[kv * HEAD_DIM:(kv + 1) * HEAD_DIM] * qcols[hidx * HEAD_DIM:(hidx + 1) * HEAD_DIM]
                red = jnp.sum(prod.reshape(HEAD_DIM // sub, sub, nk), axis=0)
                step = sub // 2
                while step >= 1:
                    red = red + pltpu.roll(red, step, 0)
                    step //= 2
                st = jnp.where(sublane == r, red, st)
            tiles.append(st)
        s = jnp.concatenate(tiles, axis=0)
        mx = jnp.maximum(jnp.max(s, axis=1, keepdims=True), sink)
        p = jnp.exp(s - mx)
        pn = p / (jnp.sum(p, axis=1, keepdims=True) + jnp.exp(sink - mx))
        pvs.append(jnp.concatenate(
            [vfull[(hidx // GQA_GROUP) * HEAD_DIM:(hidx // GQA_GROUP + 1) * HEAD_DIM] * pn[hidx:hidx + 1, :]
             for hidx in range(n_heads)], axis=0).astype(BF16))
        if len(pvs) == gs:
            first = i * bt + j + 1 - gs
            acc_ref[...] += jnp.dot(jnp.concatenate(pvs, axis=1),
                                    (tall_lane_group + first == tall_lane).astype(BF16),
                                    preferred_element_type=F32)
            pvs = []

    @pl.when(i == pl.num_programs(0) - 1)
    def _():
        o_ref[...] = acc_ref[...].astype(o_ref.dtype)


def swa_sample(qkvt, kt, vt, sinks, o_fm, n_kv):
    nrow, b = qkvt.shape
    kvw = n_kv * HEAD_DIM
    d = nrow - 2 * kvw
    nk = kt.shape[1]
    t = o_fm.shape[1] - b
    assert d == n_kv * GQA_GROUP * HEAD_DIM and (n_kv * GQA_GROUP) % 8 == 0 and t % b == 0
    bt = _pick(b, (8, 4, 2, 1))
    cache_spec = pl.BlockSpec((bt * kvw, nk), lambda i: (i, 0))
    return pl.pallas_call(
        functools.partial(_swa_sample_body, n_kv=n_kv, bt=bt),
        grid=(b // bt,),
        in_specs=[
            pl.BlockSpec(memory_space=pltpu.SMEM),
            pl.BlockSpec((nrow, b), lambda i: (0, 0)),
            cache_spec,
            cache_spec,
            pl.BlockSpec(memory_space=pl.ANY),
        ],
        out_specs=[cache_spec, cache_spec, pl.BlockSpec((d, b), lambda i: (0, t // b))],
        out_shape=[jax.ShapeDtypeStruct(kt.shape, F32), jax.ShapeDtypeStruct(vt.shape, F32),
                   jax.ShapeDtypeStruct(o_fm.shape, o_fm.dtype)],
        scratch_shapes=[pltpu.VMEM((d, b), F32)],
        input_output_aliases={4: 2},
        compiler_params=_params(("arbitrary",)),
        name="swa_sample",
    )(sinks, qkvt, kt, vt, o_fm)


def _pool_prompt_body(x_ref, g_ref, w_ref, sc_ref, gn_ref, o_ref, hn_ref, hl_ref, wb_ref, *, gc):
    i = pl.program_id(0)

    @pl.when(i == 0)
    def _():
        wb_ref[...] = w_ref[...].astype(BF16)
        hl_ref[...] = jnp.zeros_like(hl_ref)

    bm = x_ref.shape[0]
    x = x_ref[...]
    h = _rms(x, g_ref[...])
    hprev = hl_ref[...]
    hl_ref[...] = h[bm - POOL_MAXW:, :]
    pos1 = (lax.broadcasted_iota(jnp.int32, (bm, 1), 0) + i * bm + 1).astype(F32)
    for gi, w in enumerate(POOL_WINDOWS):
        cs = slice(gi * gc, (gi + 1) * gc)
        hg = h[:, cs]
        acc = jnp.concatenate([hprev[:, cs], hg], axis=0)
        step = 1
        while step < w:
            acc = acc + pltpu.roll(acc, step, 0)
            step *= 2
        inv_cnt = 1.0 / jnp.minimum(jnp.float32(w), pos1)
        pooled = (acc[POOL_MAXW:, :] * inv_cnt - hg).astype(BF16)
        y = jnp.dot(pooled, wb_ref[gi], preferred_element_type=F32)
        o_ref[:, cs] = x[:, cs] + y * sc_ref[:, cs]
    hn_ref[...] = _rms(o_ref[...], gn_ref[...]).astype(hn_ref.dtype)


def pool_prompt(x_all, g, w_pool, layer, scale, g_next, t):
    d = x_all.shape[1]
    _, ng, gc, _ = w_pool.shape
    bm = _pick(t, (512, 256, 128, 64, 32, 16))
    return pl.pallas_call(
        functools.partial(_pool_prompt_body, gc=gc),
        grid=(t // bm,),
        in_specs=[
            pl.BlockSpec((bm, d), lambda i: (i, 0)),
            pl.BlockSpec((1, d), lambda i: (0, 0)),
            pl.BlockSpec((None, ng, gc, gc), lambda i: (layer, 0, 0, 0)),
            pl.BlockSpec((1, d), lambda i: (0, 0)),
            pl.BlockSpec((1, d), lambda i: (0, 0)),
        ],
        out_specs=[
            pl.BlockSpec((bm, d), lambda i: (i, 0)),
            pl.BlockSpec((bm, d), lambda i: (i, 0)),
            pl.BlockSpec((POOL_MAXW, d), lambda i: (0, 0)),
        ],
        out_shape=[jax.ShapeDtypeStruct(x_all.shape, F32), jax.ShapeDtypeStruct(x_all.shape, BF16),
                   jax.ShapeDtypeStruct((POOL_MAXW, d), F32)],
        scratch_shapes=[pltpu.VMEM((ng, gc, gc), BF16)],
        input_output_aliases={0: 0},
        compiler_params=_params(("arbitrary",)),
        name="pool_prompt",
    )(x_all, g.reshape(1, d), w_pool, scale.reshape(1, d), g_next.reshape(1, d))


def _pool_sample_body(x_ref, st_ref, g_ref, w_ref, sc_ref, gn_ref, hn_any, o_ref, hn_ref, ns_ref, *, gc):
    del hn_any
    x = x_ref[...]
    h = _rms(x, g_ref[...])
    ns_ref[:POOL_BUF - 1] = st_ref[1:]
    ns_ref[POOL_BUF - 1] = h
    for gi, w in enumerate(POOL_WINDOWS):
        cs = slice(gi * gc, (gi + 1) * gc)
        hg = h[:, cs]
        tot = hg + jnp.sum(st_ref[POOL_BUF - (w - 1):, :, cs], axis=0)
        cnt = float(min(w, PAST_LEN + 1))
        pooled = (tot / cnt - hg).astype(BF16)
        y = jnp.dot(pooled, w_ref[gi].astype(BF16), preferred_element_type=F32)
        o_ref[:, cs] = x[:, cs] + y * sc_ref[:, cs]
    hn_ref[...] = _rms(o_ref[...], gn_ref[...]).astype(hn_ref.dtype)


def pool_sample(x_all, hn_all, state, layer, g, w_pool, scale, g_next, t):
    d = x_all.shape[1]
    b = state.shape[2]
    _, ng, gc, _ = w_pool.shape
    bt = _pick(b, (32, 16, 8))
    assert t % bt == 0
    off = t // bt
    return pl.pallas_call(
        functools.partial(_pool_sample_body, gc=gc),
        grid=(b // bt,),
        in_specs=[
            pl.BlockSpec((bt, d), lambda i: (off + i, 0)),
            pl.BlockSpec((None, POOL_BUF, bt, d), lambda i: (layer, 0, i, 0)),
            pl.BlockSpec((1, d), lambda i: (0, 0)),
            pl.BlockSpec((None, ng, gc, gc), lambda i: (layer, 0, 0, 0)),
            pl.BlockSpec((1, d), lambda i: (0, 0)),
            pl.BlockSpec((1, d), lambda i: (0, 0)),
            pl.BlockSpec(memory_space=pl.ANY),
        ],
        out_specs=[
            pl.BlockSpec((bt, d), lambda i: (off + i, 0)),
            pl.BlockSpec((bt, d), lambda i: (off + i, 0)),
            pl.BlockSpec((POOL_BUF, bt, d), lambda i: (0, i, 0)),
        ],
        out_shape=[jax.ShapeDtypeStruct(x_all.shape, F32), jax.ShapeDtypeStruct(hn_all.shape, hn_all.dtype),
                   jax.ShapeDtypeStruct((POOL_BUF, b, d), F32)],
        input_output_aliases={0: 0, 6: 1},
        compiler_params=_params(("arbitrary",)),
        name="pool_sample",
    )(x_all, state, g.reshape(1, d), w_pool, scale.reshape(1, d), g_next.reshape(1, d), hn_all)


def _mem_kv_body(m_ref, g_ref, w_ref, o_ref):
    h = _rms(m_ref[...], g_ref[...]).astype(BF16)
    o_ref[...] = jnp.dot(h, w_ref[...].astype(BF16), preferred_element_type=F32)


def mem_kv(mem, g, w_kv, layer):
    m, d = mem.shape
    n = w_kv.shape[2]
    bn = _pick(n, (512, 256, 128))
    return pl.pallas_call(
        _mem_kv_body,
        grid=(n // bn,),
        in_specs=[
            pl.BlockSpec((m, d), lambda j: (0, 0)),
            pl.BlockSpec((1, d), lambda j: (0, 0)),
            pl.BlockSpec((None, d, bn), lambda j: (layer, 0, j)),
        ],
        out_specs=pl.BlockSpec((m, bn), lambda j: (0, j)),
        out_shape=jax.ShapeDtypeStruct((m, n), F32),
        compiler_params=_params(("arbitrary",)),
        name="mem_kv",
    )(mem, g.reshape(1, d), w_kv)


def _xattn_prompt_body(h_ref, wq_ref, kv_ref, wo_ref, x_ref, gn_ref, o_ref, hn_ref, wqb_ref, wob_ref, kvb_ref):
    @pl.when(pl.program_id(0) == 0)
    def _():
        wqb_ref[...] = wq_ref[...].astype(BF16)
        wob_ref[...] = wo_ref[...].astype(BF16)
        kvb_ref[...] = kv_ref[...].astype(BF16)

    xw = X_HEADS * X_HEAD_DIM
    bm = h_ref.shape[0]
    nchunk = 2 if bm % 32 == 0 else 1
    rows = [slice(c * (bm // nchunk), (c + 1) * (bm // nchunk)) for c in range(nchunk)]
    qs = [jnp.dot(h_ref[r, :], wqb_ref[...], preferred_element_type=F32) / math.sqrt(X_HEAD_DIM) for r in rows]
    scores = [[lax.dot_general(q[:, hd * X_HEAD_DIM:(hd + 1) * X_HEAD_DIM].astype(BF16),
                               kvb_ref[:, hd * X_HEAD_DIM:(hd + 1) * X_HEAD_DIM], (((1,), (1,)), ((), ())),
                               preferred_element_type=F32) for hd in range(X_HEADS)] for q in qs]
    attn = []
    for c in range(nchunk):
        outs = []
        for hd in range(X_HEADS):
            s = scores[c][hd]
            p = jnp.exp(s - jnp.max(s, axis=-1, keepdims=True))
            l = jnp.sum(p, axis=-1, keepdims=True)
            vs = slice(xw + hd * X_HEAD_DIM, xw + (hd + 1) * X_HEAD_DIM)
            outs.append(jnp.dot(p.astype(BF16), kvb_ref[:, vs], preferred_element_type=F32) / l)
        attn.append(jnp.concatenate(outs, axis=1).astype(BF16))
    for c, r in enumerate(rows):
        xn = x_ref[r, :] + jnp.dot(attn[c], wob_ref[...], preferred_element_type=F32)
        o_ref[r, :] = xn
        hn_ref[r, :] = _rms(xn, gn_ref[...]).astype(hn_ref.dtype)


def xattn_prompt(hq_all, w_q, mkv, w_o, layer, x_all, g_next, t):
    d = x_all.shape[1]
    xw = w_q.shape[2]
    nm = mkv.shape[0]
    bm = _pick(t, (512, 256, 128, 64, 32, 16))
    return pl.pallas_call(
        _xattn_prompt_body,
        grid=(t // bm,),
        in_specs=[
            pl.BlockSpec((bm, d), lambda i: (i, 0)),
            pl.BlockSpec((None, d, xw), lambda i: (layer, 0, 0)),
            pl.BlockSpec((nm, 2 * xw), lambda i: (0, 0)),
            pl.BlockSpec((None, xw, d), lambda i: (layer, 0, 0)),
            pl.BlockSpec((bm, d), lambda i: (i, 0)),
            pl.BlockSpec((1, d), lambda i: (0, 0)),
        ],
        out_specs=[pl.BlockSpec((bm, d), lambda i: (i, 0)), pl.BlockSpec((bm, d), lambda i: (i, 0))],
        out_shape=[jax.ShapeDtypeStruct(x_all.shape, F32), jax.ShapeDtypeStruct(x_all.shape, BF16)],
        scratch_shapes=[pltpu.VMEM((d, xw), BF16), pltpu.VMEM((xw, d), BF16), pltpu.VMEM((nm, 2 * xw), BF16)],
        input_output_aliases={4: 0},
        compiler_params=_params(("arbitrary",)),
        name="xattn_prompt",
    )(hq_all, w_q, mkv, w_o, x_all, g_next.reshape(1, d))


def _xq_sample_body(h_ref, wq_ref, o_ref):
    o_ref[...] = jnp.dot(h_ref[...], wq_ref[...].astype(BF16), preferred_element_type=F32) / math.sqrt(X_HEAD_DIM)


def xq_sample(hq_all, w_q, layer, t, b):
    d = hq_all.shape[1]
    xw = w_q.shape[2]
    assert t % b == 0
    return pl.pallas_call(
        _xq_sample_body,
        grid=(1,),
        in_specs=[
            pl.BlockSpec((b, d), lambda i: (t // b, 0)),
            pl.BlockSpec((None, d, xw), lambda i: (layer, 0, 0)),
        ],
        out_specs=pl.BlockSpec((b, xw), lambda i: (0, 0)),
        out_shape=jax.ShapeDtypeStruct((b, xw), F32),
        compiler_params=_params(("arbitrary",)),
        name="xq_sample",
    )(hq_all, w_q)


def _xattn_core_body(q_ref, k_ref, v_ref, ones_ref, o_ref):
    bt, sub, hd = q_ref.shape
    nv = k_ref.shape[0] // (bt * sub)
    k = k_ref[...].reshape(bt, nv, sub, hd)
    prod = (k * q_ref[...][:, None]).reshape(bt * nv * sub, hd).astype(BF16)
    s = jnp.dot(prod, ones_ref[...], preferred_element_type=F32).reshape(bt, nv, sub, hd)

    def fold(x, op):
        step = X_HEADS
        while step < sub:
            x = op(x, pltpu.roll(x, step, 2))
            step *= 2
        return x

    mx = fold(jnp.max(s, axis=1, keepdims=True), jnp.maximum)
    p = jnp.exp(s - mx)
    l = fold(jnp.sum(p, axis=1, keepdims=True), jnp.add)
    acc = fold(jnp.sum(p * v_ref[...].reshape(bt, nv, sub, hd), axis=1, keepdims=True), jnp.add)
    o_ref[...] = (acc / l).reshape(bt, sub, hd)


def xattn_core_sample(q_s, mk_flat, mv_flat, layer, nm):
    b = q_s.shape[0]
    hd = X_HEAD_DIM
    sub = 8
    assert sub % X_HEADS == 0 and nm % (sub // X_HEADS) == 0
    bt = _pick(b, (8, 4, 2, 1))
    nblk = b // bt
    rows = bt * nm * X_HEADS
    q8 = jnp.tile(q_s.reshape(b, X_HEADS, hd), (1, sub // X_HEADS, 1))
    o8 = pl.pallas_call(
        _xattn_core_body,
        grid=(nblk,),
        in_specs=[
            pl.BlockSpec((bt, sub, hd), lambda i: (i, 0, 0)),
            pl.BlockSpec((rows, hd), lambda i: (layer * nblk + i, 0)),
            pl.BlockSpec((rows, hd), lambda i: (layer * nblk + i, 0)),
            pl.BlockSpec((hd, hd), lambda i: (0, 0)),
        ],
        out_specs=pl.BlockSpec((bt, sub, hd), lambda i: (i, 0, 0)),
        out_shape=jax.ShapeDtypeStruct((b, sub, hd), F32),
        compiler_params=_params(("arbitrary",)),
        name="xattn_core_sample",
    )(q8, mk_flat, mv_flat, jnp.ones((hd, hd), BF16))
    return o8[:, :X_HEADS].reshape(b, X_HEADS * hd)


def _xout_sample_body(a_ref, wo_ref, x_ref, gn_ref, hn_any, o_ref, hn_ref):
    del hn_any
    xn = x_ref[...] + jnp.dot(a_ref[...].astype(BF16), wo_ref[...].astype(BF16), preferred_element_type=F32)
    o_ref[...] = xn
    hn_ref[...] = _rms(xn, gn_ref[...]).astype(hn_ref.dtype)


def xout_sample(o_s, w_o, layer, x_all, hn_all, g_next, t):
    d = x_all.shape[1]
    b, xw = o_s.shape
    assert t % b == 0
    off = t // b
    return pl.pallas_call(
        _xout_sample_body,
        grid=(1,),
        in_specs=[
            pl.BlockSpec((b, xw), lambda i: (0, 0)),
            pl.BlockSpec((None, xw, d), lambda i: (layer, 0, 0)),
            pl.BlockSpec((b, d), lambda i: (off, 0)),
            pl.BlockSpec((1, d), lambda i: (0, 0)),
            pl.BlockSpec(memory_space=pl.ANY),
        ],
        out_specs=[pl.BlockSpec((b, d), lambda i: (off, 0)), pl.BlockSpec((b, d), lambda i: (off, 0))],
        out_shape=[jax.ShapeDtypeStruct(x_all.shape, F32), jax.ShapeDtypeStruct(hn_all.shape, hn_all.dtype)],
        input_output_aliases={2: 0, 4: 1},
        compiler_params=_params(("arbitrary",)),
        name="xout_sample",
    )(o_s, w_o, x_all, g_next.reshape(1, d), hn_all)


def kernel(x_prompt, x_sample, state_pool, cache_swa_k, cache_swa_v, cache_mem_k, cache_mem_v, mem_prompt,
           g_ffn1, w_ffn1_gu, w_ffn1_dn, g_mix, w_pool, pool_scale, w_qkv, w_o, sinks,
           g_xq, g_mem, w_xq, w_xkv, w_xo, g_ffn2, w_ffn2_gu, w_ffn2_dn, g_final):
    bp, t, d = x_prompt.shape
    b, s_len, _ = x_sample.shape
    assert bp == 1 and s_len == 1
    depth = g_ffn1.shape[0]
    xw = w_xq.shape[2]
    nm = mem_prompt.shape[1]
    n_kv = d // HEAD_DIM // GQA_GROUP
    kvw = n_kv * HEAD_DIM

    x, h = stack_and_norm(x_prompt[0], x_sample[:, 0], g_ffn1[0])
    cos_t, sin_t = rope_tables(t, b)
    mk_flat = cache_mem_k.reshape(-1, X_HEAD_DIM)
    mv_flat = cache_mem_v.reshape(-1, X_HEAD_DIM)
    state_sm = jnp.swapaxes(state_pool, 1, 2)

    pool_p, pool_s = [], []
    swa_kp, swa_vp, swa_ks, swa_vs = [], [], [], []
    mem_kp, mem_vp = [], []
    y_p = y_s = None
    for layer in range(depth):
        i = layer // 2
        act, wdb = gate_up(h, w_ffn1_gu, w_ffn1_dn, layer)
        x, h = down_norm(act, wdb, x, g_mix[layer], BF16)
        if layer % 2 == 0:
            x, hq, h_last = pool_prompt(x, g_mix[layer], w_pool, i, pool_scale[i], g_xq[layer], t)
            x, hq, new_state = pool_sample(x, hq, state_sm, i, g_mix[layer], w_pool, pool_scale[i], g_xq[layer], t)
            pool_p.append(h_last[None, POOL_MAXW - POOL_BUF:])
            pool_s.append(jnp.swapaxes(new_state, 0, 1))
        else:
            qkv = qkv_rope(h, w_qkv, i, cos_t, sin_t, d + kvw, kvw)
            buf = cache_swa_k.shape[2]
            keep = min(WINDOW, t)
            swa_kp.append(qkv[t - keep:t, d:d + kvw].reshape(1, keep, n_kv, HEAD_DIM))
            swa_vp.append(qkv[t - keep:t, d + kvw:].reshape(1, keep, n_kv, HEAD_DIM))
            assert buf == WINDOW
            to_fm = lambda c: jnp.transpose(c, (0, 2, 3, 1)).reshape(b * kvw, buf)
            from_fm = lambda c: jnp.transpose(c.reshape(b, n_kv, HEAD_DIM, buf), (0, 3, 1, 2))
            o_fm, wo_bf16 = swa_prompt(qkv, sinks[i], w_o, i, t, d)
            ks_fm, vs_fm, o_fm = swa_sample(qkv[t:].T, to_fm(cache_swa_k[i]), to_fm(cache_swa_v[i]), sinks[i],
                                            o_fm, n_kv)
            swa_ks.append(from_fm(ks_fm))
            swa_vs.append(from_fm(vs_fm))
            x, hq = proj_residual_norm(o_fm, wo_bf16, x, g_xq[layer])
        mkv = mem_kv(mem_prompt[0], g_mem[layer], w_xkv, layer)
        mem_kp.append(mkv[:, :xw].reshape(1, nm, X_HEADS, X_HEAD_DIM))
        mem_vp.append(mkv[:, xw:].reshape(1, nm, X_HEADS, X_HEAD_DIM))
        x, h = xattn_prompt(hq, w_xq, mkv, w_xo, layer, x, g_ffn2[layer], t)
        o_s = xattn_core_sample(xq_sample(hq, w_xq, layer, t, b), mk_flat, mv_flat, layer, nm)
        x, h = xout_sample(o_s, w_xo, layer, x, h, g_ffn2[layer], t)
        act, wdb = gate_up(h, w_ffn2_gu, w_ffn2_dn, layer)
        if layer + 1 < depth:
            x, h = down_norm(act, wdb, x, g_ffn1[layer + 1], BF16)
        else:
            (y_p,) = down_norm(act, wdb, x, g_final, F32, emit_x=False, row0=0, nrows=t)
            (y_s,) = down_norm(act, wdb, x, g_final, F32, emit_x=False, row0=t, nrows=b)
    return (y_p[None], y_s[:, None], jnp.stack(pool_p), jnp.stack(pool_s), jnp.stack(swa_kp), jnp.stack(swa_vp),
            jnp.stack(swa_ks), jnp.stack(swa_vs), jnp.stack(mem_kp), jnp.stack(mem_vp))
```

```python
import functools
import math

import jax
import jax.numpy as jnp
from jax import lax
from jax.experimental import pallas as pl
from jax.experimental.pallas import tpu as pltpu

F32 = jnp.float32
BF16 = jnp.bfloat16

RMS_EPS = 1e-6
PAST_LEN = 8192
POOL_WINDOWS = (2, 4, 8, 16)
POOL_MAXW = max(POOL_WINDOWS)
POOL_BUF = POOL_MAXW - 1
HEAD_DIM = 64
GQA_GROUP = 4
WINDOW = 128
ROPE_THETA = 10000.0
X_HEADS = 4
X_HEAD_DIM = 128
LANES = 128
V7X_VMEM_LIMIT = 60 * 1024 * 1024


def _params(sem):
    return pltpu.CompilerParams(dimension_semantics=sem, vmem_limit_bytes=V7X_VMEM_LIMIT)


def _pick(n, candidates):
    for c in candidates:
        if n % c == 0:
            return c
    raise ValueError(f"no block size in {candidates} divides {n}")


def _rms(x, g):
    return x * lax.rsqrt(jnp.mean(x * x, axis=-1, keepdims=True) + RMS_EPS) * g


def _stack_norm_body(x_ref, g_ref, *refs):
    xo_ref, h_ref = refs[-2:]
    x = x_ref[...]
    xo_ref[...] = x
    h_ref[...] = _rms(x, g_ref[...]).astype(h_ref.dtype)


def stack_and_norm(x_p, x_s, g):
    t, d = x_p.shape
    b = x_s.shape[0]
    assert t % b == 0
    bm = _pick(t, (512, 256, 128, 64, 32, 16))
    g2 = g.reshape(1, d)
    shapes = [jax.ShapeDtypeStruct((t + b, d), F32), jax.ShapeDtypeStruct((t + b, d), BF16)]
    x_all, h_all = pl.pallas_call(
        _stack_norm_body,
        grid=(t // bm,),
        in_specs=[pl.BlockSpec((bm, d), lambda i: (i, 0)), pl.BlockSpec((1, d), lambda i: (0, 0))],
        out_specs=[pl.BlockSpec((bm, d), lambda i: (i, 0)), pl.BlockSpec((bm, d), lambda i: (i, 0))],
        out_shape=shapes,
        compiler_params=_params(("arbitrary",)),
        name="stack_norm_prompt",
    )(x_p, g2)
    return pl.pallas_call(
        _stack_norm_body,
        grid=(1,),
        in_specs=[pl.BlockSpec((b, d), lambda i: (0, 0)), pl.BlockSpec((1, d), lambda i: (0, 0)),
                  pl.BlockSpec(memory_space=pl.ANY), pl.BlockSpec(memory_space=pl.ANY)],
        out_specs=[pl.BlockSpec((b, d), lambda i: (t // b, 0)), pl.BlockSpec((b, d), lambda i: (t // b, 0))],
        out_shape=shapes,
        input_output_aliases={2: 0, 3: 1},
        compiler_params=_params(("arbitrary",)),
        name="stack_norm_sample",
    )(x_s, g2, x_all, h_all)


def _gate_up_body(h_ref, wg_ref, wu_ref, wd_ref, o_ref, wdb_ref, wb_ref, *, bf):
    @pl.when(pl.program_id(1) == 0)
    def _():
        wb_ref[:, :bf] = wg_ref[...].astype(BF16)
        wb_ref[:, bf:] = wu_ref[...].astype(BF16)
        wdb_ref[...] = wd_ref[...].astype(BF16)

    bm = h_ref.shape[0]
    nchunk = 4 if bm % 64 == 0 else 1
    cr = bm // nchunk
    rs = [jnp.dot(h_ref[c * cr:(c + 1) * cr, :], wb_ref[...], preferred_element_type=F32) for c in range(nchunk)]
    for c in range(nchunk):
        a = rs[c][:, :bf]
        b = rs[c][:, bf:]
        o_ref[c * cr:(c + 1) * cr, :] = (a / (1.0 + jnp.exp(-a)) * b * 0.5).astype(o_ref.dtype)


def gate_up(h, w_gu, w_dn, layer):
    m, d = h.shape
    f = w_gu.shape[2] // 2
    bf = _pick(f, (512, 256, 128))
    bm = _pick(m, (1664, 1024, 512, 256, 128, 64, 32, 16))
    nf = f // bf
    return pl.pallas_call(
        functools.partial(_gate_up_body, bf=bf),
        grid=(nf, m // bm),
        in_specs=[
            pl.BlockSpec((bm, d), lambda j, i: (i, 0)),
            pl.BlockSpec((None, d, bf), lambda j, i: (layer, 0, j)),
            pl.BlockSpec((None, d, bf), lambda j, i: (layer, 0, j + nf)),
            pl.BlockSpec((None, bf, d), lambda j, i: (layer, j, 0)),
        ],
        out_specs=[
            pl.BlockSpec((bm, bf), lambda j, i: (i, j)),
            pl.BlockSpec((bf, d), lambda j, i: (j, 0)),
        ],
        out_shape=[jax.ShapeDtypeStruct((m, f), BF16), jax.ShapeDtypeStruct((f, d), BF16)],
        scratch_shapes=[pltpu.VMEM((d, 2 * bf), BF16)],
        compiler_params=_params(("arbitrary", "arbitrary")),
        name="gate_up",
    )(h, w_gu, w_gu, w_dn)


def _down_body(a_ref, w_ref, x_ref, g_ref, *out_refs, emit_x):
    xn = x_ref[...] + jnp.dot(a_ref[...], w_ref[...], preferred_element_type=F32)
    if emit_x:
        out_refs[0][...] = xn
    h_ref = out_refs[-1]
    h_ref[...] = _rms(xn, g_ref[...]).astype(h_ref.dtype)


def down_norm(a, w_bf16, x, g_next, h_dtype, emit_x=True, row0=0, nrows=None):
    m, k = a.shape
    d = w_bf16.shape[1]
    nrows = m if nrows is None else nrows
    assert emit_x is False or (row0 == 0 and nrows == m)
    bm = _pick(math.gcd(nrows, row0) if row0 else nrows, (320, 256, 128, 64, 32, 16))
    off = row0 // bm
    row = lambda i: (i, 0)
    src = lambda i: (i + off, 0)
    out_specs = [pl.BlockSpec((bm, d), row)]
    out_shape = [jax.ShapeDtypeStruct((nrows, d), h_dtype)]
    if emit_x:
        out_specs.insert(0, pl.BlockSpec((bm, d), row))
        out_shape.insert(0, jax.ShapeDtypeStruct((m, d), F32))
    return pl.pallas_call(
        functools.partial(_down_body, emit_x=emit_x),
        grid=(nrows // bm,),
        in_specs=[
            pl.BlockSpec((bm, k), src),
            pl.BlockSpec((k, d), lambda i: (0, 0), pipeline_mode=pl.Buffered(1)),
            pl.BlockSpec((bm, d), src),
            pl.BlockSpec((1, d), lambda i: (0, 0)),
        ],
        out_specs=out_specs,
        out_shape=out_shape,
        input_output_aliases={2: 0} if emit_x else {},
        compiler_params=_params(("arbitrary",)),
        name="down_norm",
    )(a, w_bf16, x, g_next.reshape(1, d))


def _proj_res_body(a_ref, w_ref, x_ref, g_ref, xo_ref, h_ref):
    xn = x_ref[...] + lax.dot_general(a_ref[...], w_ref[...], (((0,), (0,)), ((), ())),
                                      preferred_element_type=F32)
    xo_ref[...] = xn
    h_ref[...] = _rms(xn, g_ref[...]).astype(h_ref.dtype)


def proj_residual_norm(a_fm, w_bf16, x, g_next):
    k, m = a_fm.shape
    d = w_bf16.shape[1]
    bm = _pick(m, (640, 512, 256, 128))
    row = lambda i: (i, 0)
    return pl.pallas_call(
        _proj_res_body,
        grid=(m // bm,),
        in_specs=[
            pl.BlockSpec((k, bm), lambda i: (0, i)),
            pl.BlockSpec((k, d), lambda i: (0, 0), pipeline_mode=pl.Buffered(1)),
            pl.BlockSpec((bm, d), row),
            pl.BlockSpec((1, d), lambda i: (0, 0)),
        ],
        out_specs=[pl.BlockSpec((bm, d), row), pl.BlockSpec((bm, d), row)],
        out_shape=[jax.ShapeDtypeStruct((m, d), F32), jax.ShapeDtypeStruct((m, d), BF16)],
        input_output_aliases={2: 0},
        compiler_params=_params(("arbitrary",)),
        name="proj_residual_norm",
    )(a_fm, w_bf16, x, g_next.reshape(1, d))


def _qkv_body(h_ref, w_ref, c_ref, s_ref, o_ref, wb_ref, *, n_rope_tiles):
    @pl.when(pl.program_id(1) == 0)
    def _():
        wb_ref[...] = w_ref[...].astype(BF16)

    bm, bn = o_ref.shape
    is_rope = pl.program_id(0) < n_rope_tiles
    nchunk = 4 if bm % 64 == 0 else 1
    cr = bm // nchunk
    rs = [jnp.dot(h_ref[c * cr:(c + 1) * cr, :], wb_ref[...], preferred_element_type=F32) for c in range(nchunk)]
    lane = lax.broadcasted_iota(jnp.int32, (cr, LANES), 1)
    first_half = (lane % HEAD_DIM) < (HEAD_DIM // 2)
    for c in range(nchunk):
        rows = slice(c * cr, (c + 1) * cr)
        cos = c_ref[rows, :]
        sin = s_ref[rows, :]
        for ci in range(bn // LANES):
            blk = rs[c][:, ci * LANES:(ci + 1) * LANES]
            partner = jnp.where(first_half,
                                pltpu.roll(blk, LANES - HEAD_DIM // 2, 1),
                                pltpu.roll(blk, HEAD_DIM // 2, 1))
            o_ref[rows, ci * LANES:(ci + 1) * LANES] = jnp.where(is_rope, blk * cos + partner * sin, blk)


def qkv_rope(h, w_qkv, layer, cos_t, sin_t, n_rope_cols, bn):
    m, d = h.shape
    n = w_qkv.shape[2]
    assert n % bn == 0 and n_rope_cols % bn == 0 and bn % LANES == 0
    bm = _pick(m, (1664, 1024, 512, 256, 128, 64, 32, 16))
    return pl.pallas_call(
        functools.partial(_qkv_body, n_rope_tiles=n_rope_cols // bn),
        grid=(n // bn, m // bm),
        in_specs=[
            pl.BlockSpec((bm, d), lambda j, i: (i, 0)),
            pl.BlockSpec((None, d, bn), lambda j, i: (layer, 0, j)),
            pl.BlockSpec((bm, LANES), lambda j, i: (i, 0)),
            pl.BlockSpec((bm, LANES), lambda j, i: (i, 0)),
        ],
        out_specs=pl.BlockSpec((bm, bn), lambda j, i: (i, j)),
        out_shape=jax.ShapeDtypeStruct((m, n), F32),
        scratch_shapes=[pltpu.VMEM((d, bn), BF16)],
        compiler_params=_params(("arbitrary", "arbitrary")),
        name="qkv_rope",
    )(h, w_qkv, cos_t, sin_t)


def rope_tables(t_prompt, n_sample):
    half = HEAD_DIM // 2
    inv = ROPE_THETA ** (-jnp.arange(half, dtype=F32) / half)
    pos = jnp.concatenate([jnp.arange(t_prompt), jnp.full((n_sample,), PAST_LEN)]).astype(F32)
    ang = pos[:, None] * inv[None, :]
    cos, sin = jnp.cos(ang), jnp.sin(ang)
    reps = LANES // HEAD_DIM
    return (jnp.tile(jnp.concatenate([cos, cos], axis=1), (1, reps)),
            jnp.tile(jnp.concatenate([-sin, sin], axis=1), (1, reps)))


def _swa_prompt_body(sink_ref, q_ref, kp_ref, kc_ref, vp_ref, vc_ref, wo_ref, o_ref, wob_ref, *, n_kv, nsub):
    n = pl.program_id(0)
    blk = WINDOW
    hd = HEAD_DIM
    assert LANES == 2 * hd
    wob_ref[...] = wo_ref[...].astype(BF16)
    nq = GQA_GROUP * blk
    keys = lax.broadcasted_iota(jnp.int32, (2 * blk, nq), 0)
    qrow = lax.broadcasted_iota(jnp.int32, (2 * blk, nq), 1) % blk
    diff = qrow + blk - keys
    in_window = (diff >= 0) & (diff < WINDOW)
    masks = [in_window & ((keys >= blk) | (n > 0))] + [in_window] * (nsub - 1)
    head_of_col = lax.broadcasted_iota(jnp.int32, (1, nq), 1) // blk
    lane = lax.broadcasted_iota(jnp.int32, (1, LANES), 1)
    log2e = math.log2(math.e)
    qscale = log2e / math.sqrt(hd)
    def mine(kv):
        return (lane >= (kv % 2) * hd) & (lane < (kv % 2 + 1) * hd)

    def window_rows(prev_ref, cur_ref, sub, ps):
        prev = prev_ref[:, ps] if sub == 0 else cur_ref[(sub - 1) * blk:sub * blk, ps]
        return jnp.concatenate([prev, cur_ref[sub * blk:(sub + 1) * blk, ps]], axis=0)

    def scores(item):
        kv, sub = item
        ps = slice((kv // 2) * LANES, (kv // 2 + 1) * LANES)
        kpair = window_rows(kp_ref, kc_ref, sub, ps).astype(BF16)
        parts = []
        for g in range(GQA_GROUP):
            c0 = (kv * GQA_GROUP + g - (g % 2)) * hd
            src = q_ref[sub * blk:(sub + 1) * blk, c0:c0 + LANES] * qscale
            if g % 2 != kv % 2:
                src = pltpu.roll(src, hd, 1)
            parts.append(jnp.where(mine(kv), src, 0.0))
        qs = jnp.concatenate(parts, axis=0).astype(BF16)
        return lax.dot_general(kpair, qs, (((1,), (1,)), ((), ())), preferred_element_type=F32)

    def softmax(item, s):
        kv, sub = item
        s = jnp.where(masks[sub], s, -jnp.inf)
        sink = jnp.zeros((1, nq), F32)
        for g in range(GQA_GROUP):
            sink = jnp.where(head_of_col == g, sink_ref[kv * GQA_GROUP + g] * log2e, sink)
        mx = jnp.maximum(jnp.max(s, axis=0, keepdims=True), sink)
        return jnp.exp2(s - mx).astype(BF16), jnp.exp2(sink - mx)

    def weighted_values(item, pb, sink_term):
        kv, sub = item
        ps = slice((kv // 2) * LANES, (kv // 2 + 1) * LANES)
        vpair = window_rows(vp_ref, vc_ref, sub, ps)
        vaug = jnp.where(mine(kv), vpair, 1.0).astype(BF16)
        ot = lax.dot_general(vaug, pb, (((0,), (0,)), ((), ())), preferred_element_type=F32)
        half = kv % 2
        other = (1 - half) * hd
        denom = ot[other:other + 1, :] + sink_term
        on = (ot[half * hd:(half + 1) * hd, :] * (1.0 / denom)).astype(o_ref.dtype)
        for g in range(GQA_GROUP):
            r0 = (kv * GQA_GROUP + g) * hd
            o_ref[r0:r0 + hd, sub * blk:(sub + 1) * blk] = on[:, g * blk:(g + 1) * blk]

    items = [(kv, sub) for sub in range(nsub) for kv in range(n_kv)]
    ahead = 2
    pending = {i: scores(items[i]) for i in range(min(ahead, len(items)))}
    for i, item in enumerate(items):
        pb, sink_term = softmax(item, pending.pop(i))
        if i + ahead < len(items):
            pending[i + ahead] = scores(items[i + ahead])
        weighted_values(item, pb, sink_term)


def swa_prompt(qkv, sinks, w_o, layer, t, d):
    n_kv = d // HEAD_DIM // GQA_GROUP
    kvw = n_kv * HEAD_DIM
    assert t % WINDOW == 0
    nsub = 2 if (t // WINDOW) % 2 == 0 else 1
    nblk = t // (nsub * WINDOW)
    assert d % kvw == 0 and n_kv % 2 == 0 and d % nblk == 0 and (d // nblk) % 16 == 0
    wrows = d // nblk
    kblk = d // kvw
    vblk = kblk + 1
    prev = lambda n: jnp.maximum(n * nsub - 1, 0)
    return pl.pallas_call(
        functools.partial(_swa_prompt_body, n_kv=n_kv, nsub=nsub),
        grid=(nblk,),
        in_specs=[
            pl.BlockSpec(memory_space=pltpu.SMEM),
            pl.BlockSpec((nsub * WINDOW, d), lambda n: (n, 0)),
            pl.BlockSpec((WINDOW, kvw), lambda n: (prev(n), kblk)),
            pl.BlockSpec((nsub * WINDOW, kvw), lambda n: (n, kblk)),
            pl.BlockSpec((WINDOW, kvw), lambda n: (prev(n), vblk)),
            pl.BlockSpec((nsub * WINDOW, kvw), lambda n: (n, vblk)),
            pl.BlockSpec((None, wrows, d), lambda n: (layer, n, 0)),
        ],
        out_specs=[pl.BlockSpec((d, nsub * WINDOW), lambda n: (0, n)), pl.BlockSpec((wrows, d), lambda n: (n, 0))],
        out_shape=[jax.ShapeDtypeStruct((d, qkv.shape[0]), BF16), jax.ShapeDtypeStruct((d, d), BF16)],
        compiler_params=_params(("arbitrary",)),
        name="swa_prompt",
    )(sinks, qkv, qkv, qkv, qkv, qkv, w_o)


def _swa_sample_body(sink_ref, qkvt_ref, k_ref, v_ref, o_any, ko_ref, vo_ref, o_ref, acc_ref, *, n_kv, bt):
    del o_any
    i = pl.program_id(0)
    nk = k_ref.shape[1]
    nb = qkvt_ref.shape[1]
    kvw = n_kv * HEAD_DIM
    n_heads = n_kv * GQA_GROUP
    nq = n_heads * HEAD_DIM
    sub = 8

    @pl.when(i == 0)
    def _():
        acc_ref[...] = jnp.zeros_like(acc_ref)

    sample_lane = lax.broadcasted_iota(jnp.int32, (1, nb), 1)
    key_lane = lax.broadcasted_iota(jnp.int32, (1, nk), 1)
    sublane = lax.broadcasted_iota(jnp.int32, (sub, nk), 0)
    head_row = lax.broadcasted_iota(jnp.int32, (n_heads, 1), 0)
    sink = jnp.zeros((n_heads, 1), F32)
    for hd in range(n_heads):
        sink = jnp.where(head_row == hd, sink_ref[hd], sink)
    qt = (qkvt_ref[:nq, :] * (1.0 / math.sqrt(HEAD_DIM))).astype(BF16)
    knt = qkvt_ref[nq:nq + kvw, :]
    vnt = qkvt_ref[nq + kvw:, :]
    gs = bt
    wide_row = lax.broadcasted_iota(jnp.int32, (nb, gs * nk), 0)
    wide_lane_group = lax.broadcasted_iota(jnp.int32, (nb, gs * nk), 1) // nk
    tall_lane_group = lax.broadcasted_iota(jnp.int32, (gs * nk, nb), 0) // nk
    tall_lane = lax.broadcasted_iota(jnp.int32, (gs * nk, nb), 1)
    qcols_of = [jnp.dot(qt, (wide_row == wide_lane_group + (i * bt + g0)).astype(BF16),
                        preferred_element_type=F32) for g0 in range(0, bt, gs)]
    kfulls, vfulls = [], []
    for j in range(bt):
        is_b = sample_lane == i * bt + j
        kn = jnp.sum(jnp.where(is_b, knt, 0.0), axis=1, keepdims=True)
        vn = jnp.sum(jnp.where(is_b, vnt, 0.0), axis=1, keepdims=True)
        rows = slice(j * kvw, (j + 1) * kvw)
        kfulls.append(jnp.where(key_lane == nk - 1, kn, pltpu.roll(k_ref[rows, :], nk - 1, 1)))
        vfulls.append(jnp.where(key_lane == nk - 1, vn, pltpu.roll(v_ref[rows, :], nk - 1, 1)))
        ko_ref[rows, :] = kfulls[j]
        vo_ref[rows, :] = vfulls[j]
    all_scores = []
    for j in range(bt):
        qcols = qcols_of[j // gs][:, (j % gs) * nk:(j % gs + 1) * nk]
        kfull = kfulls[j]
        tiles = []
        for tile in range(n_heads // sub):
            st = jnp.zeros((sub, nk), F32)
            for r in range(sub):
                hidx = tile * sub + r
                kv = hidx // GQA_GROUP
                prod = kfull[kv * HEAD_DIM:(kv + 1) * HEAD_DIM] * qcols[hidx * HEAD_DIM:(hidx + 1) * HEAD_DIM]
                red = jnp.sum(prod.reshape(HEAD_DIM // sub, sub, nk), axis=0)
                step = sub // 2
                while step >= 1:
                    red = red + pltpu.roll(red, step, 0)
                    step //= 2
                st = jnp.where(sublane == r, red, st)
            tiles.append(st)
        all_scores.append(jnp.concatenate(tiles, axis=0))
    probs = []
    for s in all_scores:
        mx = jnp.maximum(jnp.max(s, axis=1, keepdims=True), sink)
        p = jnp.exp(s - mx)
        probs.append(p / (jnp.sum(p, axis=1, keepdims=True) + jnp.exp(sink - mx)))
    pvs = []
    for j in range(bt):
        pn, vfull = probs[j], vfulls[j]
        pvs.append(jnp.concatenate(
            [vfull[(hidx // GQA_GROUP) * HEAD_DIM:(hidx // GQA_GROUP + 1) * HEAD_DIM] * pn[hidx:hidx + 1, :]
             for hidx in range(n_heads)], axis=0).astype(BF16))
        if len(pvs) == gs:
            first = i * bt + j + 1 - gs
            acc_ref[...] += jnp.dot(jnp.concatenate(pvs, axis=1),
                                    (tall_lane_group + first == tall_lane).astype(BF16),
                                    preferred_element_type=F32)
            pvs = []

    @pl.when(i == pl.num_programs(0) - 1)
    def _():
        o_ref[...] = acc_ref[...].astype(o_ref.dtype)


def swa_sample(qkvt, kt, vt, sinks, o_fm, n_kv):
    nrow, b = qkvt.shape
    kvw = n_kv * HEAD_DIM
    d = nrow - 2 * kvw
    nk = kt.shape[1]
    t = o_fm.shape[1] - b
    assert d == n_kv * GQA_GROUP * HEAD_DIM and (n_kv * GQA_GROUP) % 8 == 0 and t % b == 0
    bt = _pick(b, (8, 4, 2, 1))
    cache_spec = pl.BlockSpec((bt * kvw, nk), lambda i: (i, 0))
    return pl.pallas_call(
        functools.partial(_swa_sample_body, n_kv=n_kv, bt=bt),
        grid=(b // bt,),
        in_specs=[
            pl.BlockSpec(memory_space=pltpu.SMEM),
            pl.BlockSpec((nrow, b), lambda i: (0, 0)),
            cache_spec,
            cache_spec,
            pl.BlockSpec(memory_space=pl.ANY),
        ],
        out_specs=[cache_spec, cache_spec, pl.BlockSpec((d, b), lambda i: (0, t // b))],
        out_shape=[jax.ShapeDtypeStruct(kt.shape, F32), jax.ShapeDtypeStruct(vt.shape, F32),
                   jax.ShapeDtypeStruct(o_fm.shape, o_fm.dtype)],
        scratch_shapes=[pltpu.VMEM((d, b), F32)],
        input_output_aliases={4: 2},
        compiler_params=_params(("arbitrary",)),
        name="swa_sample",
    )(sinks, qkvt, kt, vt, o_fm)


def _pool_prompt_body(x_ref, g_ref, w_ref, sc_ref, gn_ref, o_ref, hn_ref, hl_ref, wb_ref, *, gc):
    i = pl.program_id(0)

    @pl.when(i == 0)
    def _():
        wb_ref[...] = w_ref[...].astype(BF16)
        hl_ref[...] = jnp.zeros_like(hl_ref)

    bm = x_ref.shape[0]
    x = x_ref[...]
    h = _rms(x, g_ref[...])
    hprev = hl_ref[...]
    hl_ref[...] = h[bm - POOL_MAXW:, :]
    pos1 = (lax.broadcasted_iota(jnp.int32, (bm, 1), 0) + i * bm + 1).astype(F32)
    for gi, w in enumerate(POOL_WINDOWS):
        cs = slice(gi * gc, (gi + 1) * gc)
        hg = h[:, cs]
        acc = jnp.concatenate([hprev[:, cs], hg], axis=0)
        step = 1
        while step < w:
            acc = acc + pltpu.roll(acc, step, 0)
            step *= 2
        inv_cnt = 1.0 / jnp.minimum(jnp.float32(w), pos1)
        pooled = (acc[POOL_MAXW:, :] * inv_cnt - hg).astype(BF16)
        y = jnp.dot(pooled, wb_ref[gi], preferred_element_type=F32)
        o_ref[:, cs] = x[:, cs] + y * sc_ref[:, cs]
    hn_ref[...] = _rms(o_ref[...], gn_ref[...]).astype(hn_ref.dtype)


def pool_prompt(x_all, g, w_pool, layer, scale, g_next, t):
    d = x_all.shape[1]
    _, ng, gc, _ = w_pool.shape
    bm = _pick(t, (512, 256, 128, 64, 32, 16))
    return pl.pallas_call(
        functools.partial(_pool_prompt_body, gc=gc),
        grid=(t // bm,),
        in_specs=[
            pl.BlockSpec((bm, d), lambda i: (i, 0)),
            pl.BlockSpec((1, d), lambda i: (0, 0)),
            pl.BlockSpec((None, ng, gc, gc), lambda i: (layer, 0, 0, 0)),
            pl.BlockSpec((1, d), lambda i: (0, 0)),
            pl.BlockSpec((1, d), lambda i: (0, 0)),
        ],
        out_specs=[
            pl.BlockSpec((bm, d), lambda i: (i, 0)),
            pl.BlockSpec((bm, d), lambda i: (i, 0)),
            pl.BlockSpec((POOL_MAXW, d), lambda i: (0, 0)),
        ],
        out_shape=[jax.ShapeDtypeStruct(x_all.shape, F32), jax.ShapeDtypeStruct(x_all.shape, BF16),
                   jax.ShapeDtypeStruct((POOL_MAXW, d), F32)],
        scratch_shapes=[pltpu.VMEM((ng, gc, gc), BF16)],
        input_output_aliases={0: 0},
        compiler_params=_params(("arbitrary",)),
        name="pool_prompt",
    )(x_all, g.reshape(1, d), w_pool, scale.reshape(1, d), g_next.reshape(1, d))


def _pool_sample_body(x_ref, st_ref, g_ref, w_ref, sc_ref, gn_ref, hn_any, o_ref, hn_ref, ns_ref, *, gc):
    del hn_any
    x = x_ref[...]
    h = _rms(x, g_ref[...])
    ns_ref[:POOL_BUF - 1] = st_ref[1:]
    ns_ref[POOL_BUF - 1] = h
    for gi, w in enumerate(POOL_WINDOWS):
        cs = slice(gi * gc, (gi + 1) * gc)
        hg = h[:, cs]
        tot = hg + jnp.sum(st_ref[POOL_BUF - (w - 1):, :, cs], axis=0)
        cnt = float(min(w, PAST_LEN + 1))
        pooled = (tot / cnt - hg).astype(BF16)
        y = jnp.dot(pooled, w_ref[gi].astype(BF16), preferred_element_type=F32)
        o_ref[:, cs] = x[:, cs] + y * sc_ref[:, cs]
    hn_ref[...] = _rms(o_ref[...], gn_ref[...]).astype(hn_ref.dtype)


def pool_sample(x_all, hn_all, state, layer, g, w_pool, scale, g_next, t):
    d = x_all.shape[1]
    b = state.shape[2]
    _, ng, gc, _ = w_pool.shape
    bt = _pick(b, (32, 16, 8))
    assert t % bt == 0
    off = t // bt
    return pl.pallas_call(
        functools.partial(_pool_sample_body, gc=gc),
        grid=(b // bt,),
        in_specs=[
            pl.BlockSpec((bt, d), lambda i: (off + i, 0)),
            pl.BlockSpec((None, POOL_BUF, bt, d), lambda i: (layer, 0, i, 0)),
            pl.BlockSpec((1, d), lambda i: (0, 0)),
            pl.BlockSpec((None, ng, gc, gc), lambda i: (layer, 0, 0, 0)),
            pl.BlockSpec((1, d), lambda i: (0, 0)),
            pl.BlockSpec((1, d), lambda i: (0, 0)),
            pl.BlockSpec(memory_space=pl.ANY),
        ],
        out_specs=[
            pl.BlockSpec((bt, d), lambda i: (off + i, 0)),
            pl.BlockSpec((bt, d), lambda i: (off + i, 0)),
            pl.BlockSpec((POOL_BUF, bt, d), lambda i: (0, i, 0)),
        ],
        out_shape=[jax.ShapeDtypeStruct(x_all.shape, F32), jax.ShapeDtypeStruct(hn_all.shape, hn_all.dtype),
                   jax.ShapeDtypeStruct((POOL_BUF, b, d), F32)],
        input_output_aliases={0: 0, 6: 1},
        compiler_params=_params(("arbitrary",)),
        name="pool_sample",
    )(x_all, state, g.reshape(1, d), w_pool, scale.reshape(1, d), g_next.reshape(1, d), hn_all)


def _mem_kv_body(m_ref, g_ref, w_ref, o_ref):
    h = _rms(m_ref[...], g_ref[...]).astype(BF16)
    o_ref[...] = jnp.dot(h, w_ref[...].astype(BF16), preferred_element_type=F32)


def mem_kv(mem, g, w_kv, layer):
    m, d = mem.shape
    n = w_kv.shape[2]
    bn = _pick(n, (512, 256, 128))
    return pl.pallas_call(
        _mem_kv_body,
        grid=(n // bn,),
        in_specs=[
            pl.BlockSpec((m, d), lambda j: (0, 0)),
            pl.BlockSpec((1, d), lambda j: (0, 0)),
            pl.BlockSpec((None, d, bn), lambda j: (layer, 0, j)),
        ],
        out_specs=pl.BlockSpec((m, bn), lambda j: (0, j)),
        out_shape=jax.ShapeDtypeStruct((m, n), F32),
        compiler_params=_params(("arbitrary",)),
        name="mem_kv",
    )(mem, g.reshape(1, d), w_kv)


def _xattn_prompt_body(h_ref, wq_ref, kv_ref, wo_ref, x_ref, gn_ref, o_ref, hn_ref, wqb_ref, wob_ref, kvb_ref):
    @pl.when(pl.program_id(0) == 0)
    def _():
        wqb_ref[...] = wq_ref[...].astype(BF16)
        wob_ref[...] = wo_ref[...].astype(BF16)
        kvb_ref[...] = kv_ref[...].astype(BF16)

    xw = X_HEADS * X_HEAD_DIM
    bm = h_ref.shape[0]
    nchunk = 2 if bm % 32 == 0 else 1
    rows = [slice(c * (bm // nchunk), (c + 1) * (bm // nchunk)) for c in range(nchunk)]
    qs = [jnp.dot(h_ref[r, :], wqb_ref[...], preferred_element_type=F32) / math.sqrt(X_HEAD_DIM) for r in rows]
    scores = [[lax.dot_general(q[:, hd * X_HEAD_DIM:(hd + 1) * X_HEAD_DIM].astype(BF16),
                               kvb_ref[:, hd * X_HEAD_DIM:(hd + 1) * X_HEAD_DIM], (((1,), (1,)), ((), ())),
                               preferred_element_type=F32) for hd in range(X_HEADS)] for q in qs]
    attn = []
    for c in range(nchunk):
        outs = []
        for hd in range(X_HEADS):
            s = scores[c][hd]
            p = jnp.exp(s - jnp.max(s, axis=-1, keepdims=True))
            l = jnp.sum(p, axis=-1, keepdims=True)
            vs = slice(xw + hd * X_HEAD_DIM, xw + (hd + 1) * X_HEAD_DIM)
            outs.append(jnp.dot(p.astype(BF16), kvb_ref[:, vs], preferred_element_type=F32) / l)
        attn.append(jnp.concatenate(outs, axis=1).astype(BF16))
    for c, r in enumerate(rows):
        xn = x_ref[r, :] + jnp.dot(attn[c], wob_ref[...], preferred_element_type=F32)
        o_ref[r, :] = xn
        hn_ref[r, :] = _rms(xn, gn_ref[...]).astype(hn_ref.dtype)


def xattn_prompt(hq_all, w_q, mkv, w_o, layer, x_all, g_next, t):
    d = x_all.shape[1]
    xw = w_q.shape[2]
    nm = mkv.shape[0]
    bm = _pick(t, (512, 256, 128, 64, 32, 16))
    return pl.pallas_call(
        _xattn_prompt_body,
        grid=(t // bm,),
        in_specs=[
            pl.BlockSpec((bm, d), lambda i: (i, 0)),
            pl.BlockSpec((None, d, xw), lambda i: (layer, 0, 0)),
            pl.BlockSpec((nm, 2 * xw), lambda i: (0, 0)),
            pl.BlockSpec((None, xw, d), lambda i: (layer, 0, 0)),
            pl.BlockSpec((bm, d), lambda i: (i, 0)),
            pl.BlockSpec((1, d), lambda i: (0, 0)),
        ],
        out_specs=[pl.BlockSpec((bm, d), lambda i: (i, 0)), pl.BlockSpec((bm, d), lambda i: (i, 0))],
        out_shape=[jax.ShapeDtypeStruct(x_all.shape, F32), jax.ShapeDtypeStruct(x_all.shape, BF16)],
        scratch_shapes=[pltpu.VMEM((d, xw), BF16), pltpu.VMEM((xw, d), BF16), pltpu.VMEM((nm, 2 * xw), BF16)],
        input_output_aliases={4: 0},
        compiler_params=_params(("arbitrary",)),
        name="xattn_prompt",
    )(hq_all, w_q, mkv, w_o, x_all, g_next.reshape(1, d))


def _xq_sample_body(h_ref, wq_ref, o_ref):
    o_ref[...] = jnp.dot(h_ref[...], wq_ref[...].astype(BF16), preferred_element_type=F32) / math.sqrt(X_HEAD_DIM)


def xq_sample(hq_all, w_q, layer, t, b):
    d = hq_all.shape[1]
    xw = w_q.shape[2]
    assert t % b == 0
    return pl.pallas_call(
        _xq_sample_body,
        grid=(1,),
        in_specs=[
            pl.BlockSpec((b, d), lambda i: (t // b, 0)),
            pl.BlockSpec((None, d, xw), lambda i: (layer, 0, 0)),
        ],
        out_specs=pl.BlockSpec((b, xw), lambda i: (0, 0)),
        out_shape=jax.ShapeDtypeStruct((b, xw), F32),
        compiler_params=_params(("arbitrary",)),
        name="xq_sample",
    )(hq_all, w_q)


def _xattn_core_body(q_ref, k_ref, v_ref, ones_ref, o_ref):
    bt, sub, hd = q_ref.shape
    nv = k_ref.shape[0] // (bt * sub)
    k = k_ref[...].reshape(bt, nv, sub, hd)
    prod = (k * q_ref[...][:, None]).reshape(bt * nv * sub, hd).astype(BF16)
    s = jnp.dot(prod, ones_ref[...], preferred_element_type=F32).reshape(bt, nv, sub, hd)

    def fold(x, op):
        step = X_HEADS
        while step < sub:
            x = op(x, pltpu.roll(x, step, 2))
            step *= 2
        return x

    mx = fold(jnp.max(s, axis=1, keepdims=True), jnp.maximum)
    p = jnp.exp(s - mx)
    l = fold(jnp.sum(p, axis=1, keepdims=True), jnp.add)
    acc = fold(jnp.sum(p * v_ref[...].reshape(bt, nv, sub, hd), axis=1, keepdims=True), jnp.add)
    o_ref[...] = (acc / l).reshape(bt, sub, hd)


def xattn_core_sample(q_s, mk_flat, mv_flat, layer, nm):
    b = q_s.shape[0]
    hd = X_HEAD_DIM
    sub = 8
    assert sub % X_HEADS == 0 and nm % (sub // X_HEADS) == 0
    bt = _pick(b, (8, 4, 2, 1))
    nblk = b // bt
    rows = bt * nm * X_HEADS
    q8 = jnp.tile(q_s.reshape(b, X_HEADS, hd), (1, sub // X_HEADS, 1))
    o8 = pl.pallas_call(
        _xattn_core_body,
        grid=(nblk,),
        in_specs=[
            pl.BlockSpec((bt, sub, hd), lambda i: (i, 0, 0)),
            pl.BlockSpec((rows, hd), lambda i: (layer * nblk + i, 0)),
            pl.BlockSpec((rows, hd), lambda i: (layer * nblk + i, 0)),
            pl.BlockSpec((hd, hd), lambda i: (0, 0)),
        ],
        out_specs=pl.BlockSpec((bt, sub, hd), lambda i: (i, 0, 0)),
        out_shape=jax.ShapeDtypeStruct((b, sub, hd), F32),
        compiler_params=_params(("arbitrary",)),
        name="xattn_core_sample",
    )(q8, mk_flat, mv_flat, jnp.ones((hd, hd), BF16))
    return o8[:, :X_HEADS].reshape(b, X_HEADS * hd)


def _xout_sample_body(a_ref, wo_ref, x_ref, gn_ref, hn_any, o_ref, hn_ref):
    del hn_any
    xn = x_ref[...] + jnp.dot(a_ref[...].astype(BF16), wo_ref[...].astype(BF16), preferred_element_type=F32)
    o_ref[...] = xn
    hn_ref[...] = _rms(xn, gn_ref[...]).astype(hn_ref.dtype)


def xout_sample(o_s, w_o, layer, x_all, hn_all, g_next, t):
    d = x_all.shape[1]
    b, xw = o_s.shape
    assert t % b == 0
    off = t // b
    return pl.pallas_call(
        _xout_sample_body,
        grid=(1,),
        in_specs=[
            pl.BlockSpec((b, xw), lambda i: (0, 0)),
            pl.BlockSpec((None, xw, d), lambda i: (layer, 0, 0)),
            pl.BlockSpec((b, d), lambda i: (off, 0)),
            pl.BlockSpec((1, d), lambda i: (0, 0)),
            pl.BlockSpec(memory_space=pl.ANY),
        ],
        out_specs=[pl.BlockSpec((b, d), lambda i: (off, 0)), pl.BlockSpec((b, d), lambda i: (off, 0))],
        out_shape=[jax.ShapeDtypeStruct(x_all.shape, F32), jax.ShapeDtypeStruct(hn_all.shape, hn_all.dtype)],
        input_output_aliases={2: 0, 4: 1},
        compiler_params=_params(("arbitrary",)),
        name="xout_sample",
    )(o_s, w_o, x_all, g_next.reshape(1, d), hn_all)


def kernel(x_prompt, x_sample, state_pool, cache_swa_k, cache_swa_v, cache_mem_k, cache_mem_v, mem_prompt,
           g_ffn1, w_ffn1_gu, w_ffn1_dn, g_mix, w_pool, pool_scale, w_qkv, w_o, sinks,
           g_xq, g_mem, w_xq, w_xkv, w_xo, g_ffn2, w_ffn2_gu, w_ffn2_dn, g_final):
    bp, t, d = x_prompt.shape
    b, s_len, _ = x_sample.shape
    assert bp == 1 and s_len == 1
    depth = g_ffn1.shape[0]
    xw = w_xq.shape[2]
    nm = mem_prompt.shape[1]
    n_kv = d // HEAD_DIM // GQA_GROUP
    kvw = n_kv * HEAD_DIM

    x, h = stack_and_norm(x_prompt[0], x_sample[:, 0], g_ffn1[0])
    cos_t, sin_t = rope_tables(t, b)
    mk_flat = cache_mem_k.reshape(-1, X_HEAD_DIM)
    mv_flat = cache_mem_v.reshape(-1, X_HEAD_DIM)
    state_sm = jnp.swapaxes(state_pool, 1, 2)

    pool_p, pool_s = [], []
    swa_kp, swa_vp, swa_ks, swa_vs = [], [], [], []
    mem_kp, mem_vp = [], []
    y_p = y_s = None
    for layer in range(depth):
        i = layer // 2
        act, wdb = gate_up(h, w_ffn1_gu, w_ffn1_dn, layer)
        x, h = down_norm(act, wdb, x, g_mix[layer], BF16)
        if layer % 2 == 0:
            x, hq, h_last = pool_prompt(x, g_mix[layer], w_pool, i, pool_scale[i], g_xq[layer], t)
            x, hq, new_state = pool_sample(x, hq, state_sm, i, g_mix[layer], w_pool, pool_scale[i], g_xq[layer], t)
            pool_p.append(h_last[None, POOL_MAXW - POOL_BUF:])
            pool_s.append(jnp.swapaxes(new_state, 0, 1))
        else:
            qkv = qkv_rope(h, w_qkv, i, cos_t, sin_t, d + kvw, kvw)
            buf = cache_swa_k.shape[2]
            keep = min(WINDOW, t)
            swa_kp.append(qkv[t - keep:t, d:d + kvw].reshape(1, keep, n_kv, HEAD_DIM))
            swa_vp.append(qkv[t - keep:t, d + kvw:].reshape(1, keep, n_kv, HEAD_DIM))
            assert buf == WINDOW
            to_fm = lambda c: jnp.transpose(c, (0, 2, 3, 1)).reshape(b * kvw, buf)
            from_fm = lambda c: jnp.transpose(c.reshape(b, n_kv, HEAD_DIM, buf), (0, 3, 1, 2))
            o_fm, wo_bf16 = swa_prompt(qkv, sinks[i], w_o, i, t, d)
            ks_fm, vs_fm, o_fm = swa_sample(qkv[t:].T, to_fm(cache_swa_k[i]), to_fm(cache_swa_v[i]), sinks[i],
                                            o_fm, n_kv)
            swa_ks.append(from_fm(ks_fm))
            swa_vs.append(from_fm(vs_fm))
            x, hq = proj_residual_norm(o_fm, wo_bf16, x, g_xq[layer])
        mkv = mem_kv(mem_prompt[0], g_mem[layer], w_xkv, layer)
        mem_kp.append(mkv[:, :xw].reshape(1, nm, X_HEADS, X_HEAD_DIM))
        mem_vp.append(mkv[:, xw:].reshape(1, nm, X_HEADS, X_HEAD_DIM))
        x, h = xattn_prompt(hq, w_xq, mkv, w_xo, layer, x, g_ffn2[layer], t)
        o_s = xattn_core_sample(xq_sample(hq, w_xq, layer, t, b), mk_flat, mv_flat, layer, nm)
        x, h = xout_sample(o_s, w_xo, layer, x, h, g_ffn2[layer], t)
        act, wdb = gate_up(h, w_ffn2_gu, w_ffn2_dn, layer)
        if layer + 1 < depth:
            x, h = down_norm(act, wdb, x, g_ffn1[layer + 1], BF16)
        else:
            (y_p,) = down_norm(act, wdb, x, g_final, F32, emit_x=False, row0=0, nrows=t)
            (y_s,) = down_norm(act, wdb, x, g_final, F32, emit_x=False, row0=t, nrows=b)
    return (y_p[None], y_s[:, None], jnp.stack(pool_p), jnp.stack(pool_s), jnp.stack(swa_kp), jnp.stack(swa_vp),
            jnp.stack(swa_ks), jnp.stack(swa_vs), jnp.stack(mem_kp), jnp.stack(mem_vp))
```

```python
import functools
import math

import jax
import jax.numpy as jnp
from jax import lax
from jax.experimental import pallas as pl
from jax.experimental.pallas import tpu as pltpu

F32 = jnp.float32
BF16 = jnp.bfloat16

RMS_EPS = 1e-6
PAST_LEN = 8192
POOL_WINDOWS = (2, 4, 8, 16)
POOL_MAXW = max(POOL_WINDOWS)
POOL_BUF = POOL_MAXW - 1
HEAD_DIM = 64
GQA_GROUP = 4
WINDOW = 128
ROPE_THETA = 10000.0
X_HEADS = 4
X_HEAD_DIM = 128
LANES = 128
V7X_VMEM_LIMIT = 60 * 1024 * 1024


def _params(sem):
    return pltpu.CompilerParams(dimension_semantics=sem, vmem_limit_bytes=V7X_VMEM_LIMIT)


def _pick(n, candidates):
    for c in candidates:
        if n % c == 0:
            return c
    raise ValueError(f"no block size in {candidates} divides {n}")


def _rms(x, g):
    return x * lax.rsqrt(jnp.mean(x * x, axis=-1, keepdims=True) + RMS_EPS) * g


def _stack_norm_body(xp_ref, xs_ref, g_ref, xo_ref, h_ref):
    nt = pl.num_programs(0) - 1
    b = xs_ref.shape[0]

    @pl.when(pl.program_id(0) < nt)
    def _():
        x = xp_ref[...]
        xo_ref[...] = x
        h_ref[...] = _rms(x, g_ref[...]).astype(h_ref.dtype)

    @pl.when(pl.program_id(0) == nt)
    def _():
        x = xs_ref[...]
        xo_ref[:b, :] = x
        h_ref[:b, :] = _rms(x, g_ref[...]).astype(h_ref.dtype)


def stack_and_norm(x_p, x_s, g):
    t, d = x_p.shape
    b = x_s.shape[0]
    bm = _pick(t, (512, 256, 128, 64, 32, 16))
    nt = t // bm
    assert b <= bm and b % 16 == 0
    row = lambda i: (i, 0)
    return pl.pallas_call(
        _stack_norm_body,
        grid=(nt + 1,),
        in_specs=[pl.BlockSpec((bm, d), lambda i: (jnp.minimum(i, nt - 1), 0)),
                  pl.BlockSpec((b, d), lambda i: (0, 0)),
                  pl.BlockSpec((1, d), lambda i: (0, 0))],
        out_specs=[pl.BlockSpec((bm, d), row), pl.BlockSpec((bm, d), row)],
        out_shape=[jax.ShapeDtypeStruct((t + b, d), F32), jax.ShapeDtypeStruct((t + b, d), BF16)],
        compiler_params=_params(("arbitrary",)),
        name="stack_norm",
    )(x_p, x_s, g.reshape(1, d))


def _gate_up_body(h_ref, wg_ref, wu_ref, wd_ref, o_ref, wdb_ref, wb_ref, *, bf):
    @pl.when(pl.program_id(1) == 0)
    def _():
        wb_ref[:, :bf] = wg_ref[...].astype(BF16)
        wb_ref[:, bf:] = wu_ref[...].astype(BF16)
        wdb_ref[...] = wd_ref[...].astype(BF16)

    bm = h_ref.shape[0]
    nchunk = 4 if bm % 64 == 0 else 1
    cr = bm // nchunk
    rs = [jnp.dot(h_ref[c * cr:(c + 1) * cr, :], wb_ref[...], preferred_element_type=F32) for c in range(nchunk)]
    for c in range(nchunk):
        a = rs[c][:, :bf]
        b = rs[c][:, bf:]
        o_ref[c * cr:(c + 1) * cr, :] = (a / (1.0 + jnp.exp(-a)) * b * 0.5).astype(o_ref.dtype)


def gate_up(h, w_gu, w_dn, layer):
    m, d = h.shape
    f = w_gu.shape[2] // 2
    bf = _pick(f, (512, 256, 128))
    bm = _pick(m, (1664, 1024, 512, 256, 128, 64, 32, 16))
    nf = f // bf
    return pl.pallas_call(
        functools.partial(_gate_up_body, bf=bf),
        grid=(nf, m // bm),
        in_specs=[
            pl.BlockSpec((bm, d), lambda j, i: (i, 0)),
            pl.BlockSpec((None, d, bf), lambda j, i: (layer, 0, j)),
            pl.BlockSpec((None, d, bf), lambda j, i: (layer, 0, j + nf)),
            pl.BlockSpec((None, bf, d), lambda j, i: (layer, j, 0)),
        ],
        out_specs=[
            pl.BlockSpec((bm, bf), lambda j, i: (i, j)),
            pl.BlockSpec((bf, d), lambda j, i: (j, 0)),
        ],
        out_shape=[jax.ShapeDtypeStruct((m, f), BF16), jax.ShapeDtypeStruct((f, d), BF16)],
        scratch_shapes=[pltpu.VMEM((d, 2 * bf), BF16)],
        compiler_params=_params(("arbitrary", "arbitrary")),
        name="gate_up",
    )(h, w_gu, w_gu, w_dn)


def _down_body(a_ref, w_ref, x_ref, g_ref, *out_refs, emit_x):
    xn = x_ref[...] + jnp.dot(a_ref[...], w_ref[...], preferred_element_type=F32)
    if emit_x:
        out_refs[0][...] = xn
    h_ref = out_refs[-1]
    h_ref[...] = _rms(xn, g_ref[...]).astype(h_ref.dtype)


def down_norm(a, w_bf16, x, g_next, h_dtype, emit_x=True, row0=0, nrows=None):
    m, k = a.shape
    d = w_bf16.shape[1]
    nrows = m if nrows is None else nrows
    assert emit_x is False or (row0 == 0 and nrows == m)
    bm = _pick(math.gcd(nrows, row0) if row0 else nrows, (320, 256, 128, 64, 32, 16))
    off = row0 // bm
    row = lambda i: (i, 0)
    src = lambda i: (i + off, 0)
    out_specs = [pl.BlockSpec((bm, d), row)]
    out_shape = [jax.ShapeDtypeStruct((nrows, d), h_dtype)]
    if emit_x:
        out_specs.insert(0, pl.BlockSpec((bm, d), row))
        out_shape.insert(0, jax.ShapeDtypeStruct((m, d), F32))
    return pl.pallas_call(
        functools.partial(_down_body, emit_x=emit_x),
        grid=(nrows // bm,),
        in_specs=[
            pl.BlockSpec((bm, k), src),
            pl.BlockSpec((k, d), lambda i: (0, 0), pipeline_mode=pl.Buffered(1)),
            pl.BlockSpec((bm, d), src),
            pl.BlockSpec((1, d), lambda i: (0, 0)),
        ],
        out_specs=out_specs,
        out_shape=out_shape,
        input_output_aliases={2: 0} if emit_x else {},
        compiler_params=_params(("arbitrary",)),
        name="down_norm",
    )(a, w_bf16, x, g_next.reshape(1, d))


def _proj_res_body(a_ref, w_ref, x_ref, g_ref, xo_ref, h_ref):
    xn = x_ref[...] + lax.dot_general(a_ref[...], w_ref[...], (((0,), (0,)), ((), ())),
                                      preferred_element_type=F32)
    xo_ref[...] = xn
    h_ref[...] = _rms(xn, g_ref[...]).astype(h_ref.dtype)


def proj_residual_norm(a_fm, w_bf16, x, g_next):
    k, m = a_fm.shape
    d = w_bf16.shape[1]
    bm = _pick(m, (640, 512, 256, 128))
    row = lambda i: (i, 0)
    return pl.pallas_call(
        _proj_res_body,
        grid=(m // bm,),
        in_specs=[
            pl.BlockSpec((k, bm), lambda i: (0, i)),
            pl.BlockSpec((k, d), lambda i: (0, 0), pipeline_mode=pl.Buffered(1)),
            pl.BlockSpec((bm, d), row),
            pl.BlockSpec((1, d), lambda i: (0, 0)),
        ],
        out_specs=[pl.BlockSpec((bm, d), row), pl.BlockSpec((bm, d), row)],
        out_shape=[jax.ShapeDtypeStruct((m, d), F32), jax.ShapeDtypeStruct((m, d), BF16)],
        input_output_aliases={2: 0},
        compiler_params=_params(("arbitrary",)),
        name="proj_residual_norm",
    )(a_fm, w_bf16, x, g_next.reshape(1, d))


def _qkv_body(h_ref, w_ref, c_ref, s_ref, o_ref, wb_ref, *, n_rope_tiles):
    @pl.when(pl.program_id(1) == 0)
    def _():
        wb_ref[...] = w_ref[...].astype(BF16)

    bm, bn = o_ref.shape
    is_rope = pl.program_id(0) < n_rope_tiles
    nchunk = 4 if bm % 64 == 0 else 1
    cr = bm // nchunk
    rs = [jnp.dot(h_ref[c * cr:(c + 1) * cr, :], wb_ref[...], preferred_element_type=F32) for c in range(nchunk)]
    lane = lax.broadcasted_iota(jnp.int32, (cr, LANES), 1)
    first_half = (lane % HEAD_DIM) < (HEAD_DIM // 2)
    for c in range(nchunk):
        rows = slice(c * cr, (c + 1) * cr)
        cos = c_ref[rows, :]
        sin = s_ref[rows, :]
        for ci in range(bn // LANES):
            blk = rs[c][:, ci * LANES:(ci + 1) * LANES]
            partner = jnp.where(first_half,
                                pltpu.roll(blk, LANES - HEAD_DIM // 2, 1),
                                pltpu.roll(blk, HEAD_DIM // 2, 1))
            o_ref[rows, ci * LANES:(ci + 1) * LANES] = jnp.where(is_rope, blk * cos + partner * sin, blk)


def qkv_rope(h, w_qkv, layer, cos_t, sin_t, n_rope_cols, bn):
    m, d = h.shape
    n = w_qkv.shape[2]
    assert n % bn == 0 and n_rope_cols % bn == 0 and bn % LANES == 0
    bm = _pick(m, (1664, 1024, 512, 256, 128, 64, 32, 16))
    return pl.pallas_call(
        functools.partial(_qkv_body, n_rope_tiles=n_rope_cols // bn),
        grid=(n // bn, m // bm),
        in_specs=[
            pl.BlockSpec((bm, d), lambda j, i: (i, 0)),
            pl.BlockSpec((None, d, bn), lambda j, i: (layer, 0, j)),
            pl.BlockSpec((bm, LANES), lambda j, i: (i, 0)),
            pl.BlockSpec((bm, LANES), lambda j, i: (i, 0)),
        ],
        out_specs=pl.BlockSpec((bm, bn), lambda j, i: (i, j)),
        out_shape=jax.ShapeDtypeStruct((m, n), F32),
        scratch_shapes=[pltpu.VMEM((d, bn), BF16)],
        compiler_params=_params(("arbitrary", "arbitrary")),
        name="qkv_rope",
    )(h, w_qkv, cos_t, sin_t)


def rope_tables(t_prompt, n_sample):
    half = HEAD_DIM // 2
    inv = ROPE_THETA ** (-jnp.arange(half, dtype=F32) / half)
    pos = jnp.concatenate([jnp.arange(t_prompt), jnp.full((n_sample,), PAST_LEN)]).astype(F32)
    ang = pos[:, None] * inv[None, :]
    cos, sin = jnp.cos(ang), jnp.sin(ang)
    reps = LANES // HEAD_DIM
    return (jnp.tile(jnp.concatenate([cos, cos], axis=1), (1, reps)),
            jnp.tile(jnp.concatenate([-sin, sin], axis=1), (1, reps)))


def _swa_prompt_body(sink_ref, q_ref, kp_ref, kc_ref, vp_ref, vc_ref, wo_ref, o_ref, wob_ref, *, n_kv, nsub):
    n = pl.program_id(0)
    blk = WINDOW
    hd = HEAD_DIM
    assert LANES == 2 * hd
    wob_ref[...] = wo_ref[...].astype(BF16)
    nq = GQA_GROUP * blk
    keys = lax.broadcasted_iota(jnp.int32, (2 * blk, nq), 0)
    qrow = lax.broadcasted_iota(jnp.int32, (2 * blk, nq), 1) % blk
    diff = qrow + blk - keys
    in_window = (diff >= 0) & (diff < WINDOW)
    masks = [in_window & ((keys >= blk) | (n > 0))] + [in_window] * (nsub - 1)
    head_of_col = lax.broadcasted_iota(jnp.int32, (1, nq), 1) // blk
    lane = lax.broadcasted_iota(jnp.int32, (1, LANES), 1)
    log2e = math.log2(math.e)
    qscale = log2e / math.sqrt(hd)
    def mine(kv):
        return (lane >= (kv % 2) * hd) & (lane < (kv % 2 + 1) * hd)

    def window_rows(prev_ref, cur_ref, sub, ps):
        prev = prev_ref[:, ps] if sub == 0 else cur_ref[(sub - 1) * blk:sub * blk, ps]
        return jnp.concatenate([prev, cur_ref[sub * blk:(sub + 1) * blk, ps]], axis=0)

    def scores(item):
        kv, sub = item
        ps = slice((kv // 2) * LANES, (kv // 2 + 1) * LANES)
        kpair = window_rows(kp_ref, kc_ref, sub, ps).astype(BF16)
        parts = []
        for g in range(GQA_GROUP):
            c0 = (kv * GQA_GROUP + g - (g % 2)) * hd
            src = q_ref[sub * blk:(sub + 1) * blk, c0:c0 + LANES] * qscale
            if g % 2 != kv % 2:
                src = pltpu.roll(src, hd, 1)
            parts.append(jnp.where(mine(kv), src, 0.0))
        qs = jnp.concatenate(parts, axis=0).astype(BF16)
        return lax.dot_general(kpair, qs, (((1,), (1,)), ((), ())), preferred_element_type=F32)

    def softmax(item, s):
        kv, sub = item
        s = jnp.where(masks[sub], s, -jnp.inf)
        sink = jnp.zeros((1, nq), F32)
        for g in range(GQA_GROUP):
            sink = jnp.where(head_of_col == g, sink_ref[kv * GQA_GROUP + g] * log2e, sink)
        mx = jnp.maximum(jnp.max(s, axis=0, keepdims=True), sink)
        return jnp.exp2(s - mx).astype(BF16), jnp.exp2(sink - mx)

    def weighted_values(item, pb, sink_term):
        kv, sub = item
        ps = slice((kv // 2) * LANES, (kv // 2 + 1) * LANES)
        vpair = window_rows(vp_ref, vc_ref, sub, ps)
        vaug = jnp.where(mine(kv), vpair, 1.0).astype(BF16)
        ot = lax.dot_general(vaug, pb, (((0,), (0,)), ((), ())), preferred_element_type=F32)
        half = kv % 2
        other = (1 - half) * hd
        denom = ot[other:other + 1, :] + sink_term
        on = (ot[half * hd:(half + 1) * hd, :] * (1.0 / denom)).astype(o_ref.dtype)
        for g in range(GQA_GROUP):
            r0 = (kv * GQA_GROUP + g) * hd
            o_ref[r0:r0 + hd, sub * blk:(sub + 1) * blk] = on[:, g * blk:(g + 1) * blk]

    @pl.when(n < pl.num_programs(0) - 1)
    def _():
        items = [(kv, sub) for sub in range(nsub) for kv in range(n_kv)]
        ahead = 2
        pending = {i: scores(items[i]) for i in range(min(ahead, len(items)))}
        for i, item in enumerate(items):
            pb, sink_term = softmax(item, pending.pop(i))
            if i + ahead < len(items):
                pending[i + ahead] = scores(items[i + ahead])
            weighted_values(item, pb, sink_term)

    @pl.when(n == pl.num_programs(0) - 1)
    def _():
        o_ref[...] = jnp.zeros_like(o_ref)


def swa_prompt(qkv, sinks, w_o, layer, t, d):
    n_kv = d // HEAD_DIM // GQA_GROUP
    kvw = n_kv * HEAD_DIM
    assert t % WINDOW == 0
    nsub = 2 if (t // WINDOW) % 2 == 0 else 1
    nblk = t // (nsub * WINDOW)
    assert d % kvw == 0 and n_kv % 2 == 0 and d % nblk == 0 and (d // nblk) % 16 == 0
    wrows = d // nblk
    kblk = d // kvw
    vblk = kblk + 1
    assert 0 < qkv.shape[0] - t <= nsub * WINDOW
    cur = lambda n: jnp.minimum(n, nblk - 1)
    prev = lambda n: jnp.maximum(cur(n) * nsub - 1, 0)
    return pl.pallas_call(
        functools.partial(_swa_prompt_body, n_kv=n_kv, nsub=nsub),
        grid=(nblk + 1,),
        in_specs=[
            pl.BlockSpec(memory_space=pltpu.SMEM),
            pl.BlockSpec((nsub * WINDOW, d), lambda n: (cur(n), 0)),
            pl.BlockSpec((WINDOW, kvw), lambda n: (prev(n), kblk)),
            pl.BlockSpec((nsub * WINDOW, kvw), lambda n: (cur(n), kblk)),
            pl.BlockSpec((WINDOW, kvw), lambda n: (prev(n), vblk)),
            pl.BlockSpec((nsub * WINDOW, kvw), lambda n: (cur(n), vblk)),
            pl.BlockSpec((None, wrows, d), lambda n: (layer, cur(n), 0)),
        ],
        out_specs=[pl.BlockSpec((d, nsub * WINDOW), lambda n: (0, n)),
                   pl.BlockSpec((wrows, d), lambda n: (cur(n), 0))],
        out_shape=[jax.ShapeDtypeStruct((d, qkv.shape[0]), BF16), jax.ShapeDtypeStruct((d, d), BF16)],
        compiler_params=_params(("arbitrary",)),
        name="swa_prompt",
    )(sinks, qkv, qkv, qkv, qkv, qkv, w_o)


def _swa_sample_body(sink_ref, qkvt_ref, k_ref, v_ref, o_any, ko_ref, vo_ref, o_ref, acc_ref, *, n_kv, bt):
    del o_any
    i = pl.program_id(0)
    nk = k_ref.shape[1]
    nb = qkvt_ref.shape[1]
    kvw = n_kv * HEAD_DIM
    n_heads = n_kv * GQA_GROUP
    nq = n_heads * HEAD_DIM
    sub = 8

    @pl.when(i == 0)
    def _():
        acc_ref[...] = jnp.zeros_like(acc_ref)

    sample_lane = lax.broadcasted_iota(jnp.int32, (1, nb), 1)
    key_lane = lax.broadcasted_iota(jnp.int32, (1, nk), 1)
    sublane = lax.broadcasted_iota(jnp.int32, (sub, nk), 0)
    head_row = lax.broadcasted_iota(jnp.int32, (n_heads, 1), 0)
    sink = jnp.zeros((n_heads, 1), F32)
    for hd in range(n_heads):
        sink = jnp.where(head_row == hd, sink_ref[hd], sink)
    qt = (qkvt_ref[:nq, :] * (1.0 / math.sqrt(HEAD_DIM))).astype(BF16)
    knt = qkvt_ref[nq:nq + kvw, :]
    vnt = qkvt_ref[nq + kvw:, :]
    gs = bt
    wide_row = lax.broadcasted_iota(jnp.int32, (nb, gs * nk), 0)
    wide_lane_group = lax.broadcasted_iota(jnp.int32, (nb, gs * nk), 1) // nk
    tall_lane_group = lax.broadcasted_iota(jnp.int32, (gs * nk, nb), 0) // nk
    tall_lane = lax.broadcasted_iota(jnp.int32, (gs * nk, nb), 1)
    qcols_of = [jnp.dot(qt, (wide_row == wide_lane_group + (i * bt + g0)).astype(BF16),
                        preferred_element_type=F32) for g0 in range(0, bt, gs)]
    kfulls, vfulls = [], []
    for j in range(bt):
        is_b = sample_lane == i * bt + j
        kn = jnp.sum(jnp.where(is_b, knt, 0.0), axis=1, keepdims=True)
        vn = jnp.sum(jnp.where(is_b, vnt, 0.0), axis=1, keepdims=True)
        rows = slice(j * kvw, (j + 1) * kvw)
        kfulls.append(jnp.where(key_lane == nk - 1, kn, pltpu.roll(k_ref[rows, :], nk - 1, 1)))
        vfulls.append(jnp.where(key_lane == nk - 1, vn, pltpu.roll(v_ref[rows, :], nk - 1, 1)))
        ko_ref[rows, :] = kfulls[j]
        vo_ref[rows, :] = vfulls[j]
    all_scores = []
    for j in range(bt):
        qcols = qcols_of[j // gs][:, (j % gs) * nk:(j % gs + 1) * nk]
        kfull = kfulls[j]
        tiles = []
        for tile in range(n_heads // sub):
            st = jnp.zeros((sub, nk), F32)
            for r in range(sub):
                hidx = tile * sub + r
                kv = hidx // GQA_GROUP
                prod = kfull[kv * HEAD_DIM:(kv + 1) * HEAD_DIM] * qcols[hidx * HEAD_DIM:(hidx + 1) * HEAD_DIM]
                red = jnp.sum(prod.reshape(HEAD_DIM // sub, sub, nk), axis=0)
                step = sub // 2
                while step >= 1:
                    red = red + pltpu.roll(red, step, 0)
                    step //= 2
                st = jnp.where(sublane == r, red, st)
            tiles.append(st)
        all_scores.append(jnp.concatenate(tiles, axis=0))
    probs = []
    for s in all_scores:
        mx = jnp.maximum(jnp.max(s, axis=1, keepdims=True), sink)
        p = jnp.exp(s - mx)
        probs.append(p / (jnp.sum(p, axis=1, keepdims=True) + jnp.exp(sink - mx)))
    pvs = []
    for j in range(bt):
        pn, vfull = probs[j], vfulls[j]
        pvs.append(jnp.concatenate(
            [vfull[(hidx // GQA_GROUP) * HEAD_DIM:(hidx // GQA_GROUP + 1) * HEAD_DIM] * pn[hidx:hidx + 1, :]
             for hidx in range(n_heads)], axis=0).astype(BF16))
        if len(pvs) == gs:
            first = i * bt + j + 1 - gs
            acc_ref[...] += jnp.dot(jnp.concatenate(pvs, axis=1),
                                    (tall_lane_group + first == tall_lane).astype(BF16),
                                    preferred_element_type=F32)
            pvs = []

    @pl.when(i == pl.num_programs(0) - 1)
    def _():
        o_ref[...] = acc_ref[...].astype(o_ref.dtype)


def swa_sample(qkvt, kt, vt, sinks, o_fm, n_kv):
    nrow, b = qkvt.shape
    kvw = n_kv * HEAD_DIM
    d = nrow - 2 * kvw
    nk = kt.shape[1]
    t = o_fm.shape[1] - b
    assert d == n_kv * GQA_GROUP * HEAD_DIM and (n_kv * GQA_GROUP) % 8 == 0 and t % b == 0
    bt = _pick(b, (8, 4, 2, 1))
    cache_spec = pl.BlockSpec((bt * kvw, nk), lambda i: (i, 0))
    return pl.pallas_call(
        functools.partial(_swa_sample_body, n_kv=n_kv, bt=bt),
        grid=(b // bt,),
        in_specs=[
            pl.BlockSpec(memory_space=pltpu.SMEM),
            pl.BlockSpec((nrow, b), lambda i: (0, 0)),
            cache_spec,
            cache_spec,
            pl.BlockSpec(memory_space=pl.ANY),
        ],
        out_specs=[cache_spec, cache_spec, pl.BlockSpec((d, b), lambda i: (0, t // b))],
        out_shape=[jax.ShapeDtypeStruct(kt.shape, F32), jax.ShapeDtypeStruct(vt.shape, F32),
                   jax.ShapeDtypeStruct(o_fm.shape, o_fm.dtype)],
        scratch_shapes=[pltpu.VMEM((d, b), F32)],
        input_output_aliases={4: 2},
        compiler_params=_params(("arbitrary",)),
        name="swa_sample",
    )(sinks, qkvt, kt, vt, o_fm)


def _pool_prompt_body(x_ref, g_ref, w_ref, sc_ref, gn_ref, hn_any, o_ref, hn_ref, hl_ref, wb_ref, *, gc):
    del hn_any
    i = pl.program_id(0)

    @pl.when(i == 0)
    def _():
        wb_ref[...] = w_ref[...].astype(BF16)
        hl_ref[...] = jnp.zeros_like(hl_ref)

    bm = x_ref.shape[0]
    x = x_ref[...]
    h = _rms(x, g_ref[...])
    hprev = hl_ref[...]
    hl_ref[...] = h[bm - POOL_MAXW:, :]
    pos1 = (lax.broadcasted_iota(jnp.int32, (bm, 1), 0) + i * bm + 1).astype(F32)
    for gi, w in enumerate(POOL_WINDOWS):
        cs = slice(gi * gc, (gi + 1) * gc)
        hg = h[:, cs]
        acc = jnp.concatenate([hprev[:, cs], hg], axis=0)
        step = 1
        while step < w:
            acc = acc + pltpu.roll(acc, step, 0)
            step *= 2
        inv_cnt = 1.0 / jnp.minimum(jnp.float32(w), pos1)
        pooled = (acc[POOL_MAXW:, :] * inv_cnt - hg).astype(BF16)
        y = jnp.dot(pooled, wb_ref[gi], preferred_element_type=F32)
        o_ref[:, cs] = x[:, cs] + y * sc_ref[:, cs]
    hn_ref[...] = _rms(o_ref[...], gn_ref[...]).astype(hn_ref.dtype)


def pool_prompt(x_all, hn_all, g, w_pool, layer, scale, g_next, t):
    d = x_all.shape[1]
    _, ng, gc, _ = w_pool.shape
    bm = _pick(t, (512, 256, 128, 64, 32, 16))
    return pl.pallas_call(
        functools.partial(_pool_prompt_body, gc=gc),
        grid=(t // bm,),
        in_specs=[
            pl.BlockSpec((bm, d), lambda i: (i, 0)),
            pl.BlockSpec((1, d), lambda i: (0, 0)),
            pl.BlockSpec((None, ng, gc, gc), lambda i: (layer, 0, 0, 0)),
            pl.BlockSpec((1, d), lambda i: (0, 0)),
            pl.BlockSpec((1, d), lambda i: (0, 0)),
            pl.BlockSpec(memory_space=pl.ANY),
        ],
        out_specs=[
            pl.BlockSpec((bm, d), lambda i: (i, 0)),
            pl.BlockSpec((bm, d), lambda i: (i, 0)),
            pl.BlockSpec((POOL_MAXW, d), lambda i: (0, 0)),
        ],
        out_shape=[jax.ShapeDtypeStruct(x_all.shape, F32), jax.ShapeDtypeStruct(hn_all.shape, hn_all.dtype),
                   jax.ShapeDtypeStruct((POOL_MAXW, d), F32)],
        scratch_shapes=[pltpu.VMEM((ng, gc, gc), BF16)],
        input_output_aliases={0: 0, 5: 1},
        compiler_params=_params(("arbitrary",)),
        name="pool_prompt",
    )(x_all, g.reshape(1, d), w_pool, scale.reshape(1, d), g_next.reshape(1, d), hn_all)


def _pool_sample_body(x_ref, st_ref, g_ref, w_ref, sc_ref, gn_ref, hn_any, o_ref, hn_ref, ns_ref, *, gc):
    del hn_any
    x = x_ref[...]
    h = _rms(x, g_ref[...])
    ns_ref[:POOL_BUF - 1] = st_ref[1:]
    ns_ref[POOL_BUF - 1] = h
    for gi, w in enumerate(POOL_WINDOWS):
        cs = slice(gi * gc, (gi + 1) * gc)
        hg = h[:, cs]
        tot = hg + jnp.sum(st_ref[POOL_BUF - (w - 1):, :, cs], axis=0)
        cnt = float(min(w, PAST_LEN + 1))
        pooled = (tot / cnt - hg).astype(BF16)
        y = jnp.dot(pooled, w_ref[gi].astype(BF16), preferred_element_type=F32)
        o_ref[:, cs] = x[:, cs] + y * sc_ref[:, cs]
    hn_ref[...] = _rms(o_ref[...], gn_ref[...]).astype(hn_ref.dtype)


def pool_sample(x_all, hn_all, state, layer, g, w_pool, scale, g_next, t):
    d = x_all.shape[1]
    b = state.shape[2]
    _, ng, gc, _ = w_pool.shape
    bt = _pick(b, (32, 16, 8))
    assert t % bt == 0
    off = t // bt
    return pl.pallas_call(
        functools.partial(_pool_sample_body, gc=gc),
        grid=(b // bt,),
        in_specs=[
            pl.BlockSpec((bt, d), lambda i: (off + i, 0)),
            pl.BlockSpec((None, POOL_BUF, bt, d), lambda i: (layer, 0, i, 0)),
            pl.BlockSpec((1, d), lambda i: (0, 0)),
            pl.BlockSpec((None, ng, gc, gc), lambda i: (layer, 0, 0, 0)),
            pl.BlockSpec((1, d), lambda i: (0, 0)),
            pl.BlockSpec((1, d), lambda i: (0, 0)),
            pl.BlockSpec(memory_space=pl.ANY),
        ],
        out_specs=[
            pl.BlockSpec((bt, d), lambda i: (off + i, 0)),
            pl.BlockSpec((bt, d), lambda i: (off + i, 0)),
            pl.BlockSpec((POOL_BUF, bt, d), lambda i: (0, i, 0)),
        ],
        out_shape=[jax.ShapeDtypeStruct(x_all.shape, F32), jax.ShapeDtypeStruct(hn_all.shape, hn_all.dtype),
                   jax.ShapeDtypeStruct((POOL_BUF, b, d), F32)],
        input_output_aliases={0: 0, 6: 1},
        compiler_params=_params(("arbitrary",)),
        name="pool_sample",
    )(x_all, state, g.reshape(1, d), w_pool, scale.reshape(1, d), g_next.reshape(1, d), hn_all)


def _mem_kv_body(m_ref, g_ref, w_ref, o_ref):
    h = _rms(m_ref[...], g_ref[...]).astype(BF16)
    o_ref[...] = jnp.dot(h, w_ref[...].astype(BF16), preferred_element_type=F32)


def mem_kv(mem, g, w_kv, layer):
    m, d = mem.shape
    n = w_kv.shape[2]
    bn = _pick(n, (512, 256, 128))
    return pl.pallas_call(
        _mem_kv_body,
        grid=(n // bn,),
        in_specs=[
            pl.BlockSpec((m, d), lambda j: (0, 0)),
            pl.BlockSpec((1, d), lambda j: (0, 0)),
            pl.BlockSpec((None, d, bn), lambda j: (layer, 0, j)),
        ],
        out_specs=pl.BlockSpec((m, bn), lambda j: (0, j)),
        out_shape=jax.ShapeDtypeStruct((m, n), F32),
        compiler_params=_params(("arbitrary",)),
        name="mem_kv",
    )(mem, g.reshape(1, d), w_kv)


def _xattn_prompt_body(h_ref, wq_ref, kv_ref, wo_ref, x_ref, gn_ref, o_ref, hn_ref, wqb_ref, wob_ref, kvb_ref):
    @pl.when(pl.program_id(0) == 0)
    def _():
        wqb_ref[...] = wq_ref[...].astype(BF16)
        wob_ref[...] = wo_ref[...].astype(BF16)
        kvb_ref[...] = kv_ref[...].astype(BF16)

    xw = X_HEADS * X_HEAD_DIM
    bm = h_ref.shape[0]
    nchunk = 2 if bm % 32 == 0 else 1
    rows = [slice(c * (bm // nchunk), (c + 1) * (bm // nchunk)) for c in range(nchunk)]
    qs = [jnp.dot(h_ref[r, :], wqb_ref[...], preferred_element_type=F32) / math.sqrt(X_HEAD_DIM) for r in rows]
    scores = [[lax.dot_general(q[:, hd * X_HEAD_DIM:(hd + 1) * X_HEAD_DIM].astype(BF16),
                               kvb_ref[:, hd * X_HEAD_DIM:(hd + 1) * X_HEAD_DIM], (((1,), (1,)), ((), ())),
                               preferred_element_type=F32) for hd in range(X_HEADS)] for q in qs]
    attn = []
    for c in range(nchunk):
        outs = []
        for hd in range(X_HEADS):
            s = scores[c][hd]
            p = jnp.exp(s - jnp.max(s, axis=-1, keepdims=True))
            l = jnp.sum(p, axis=-1, keepdims=True)
            vs = slice(xw + hd * X_HEAD_DIM, xw + (hd + 1) * X_HEAD_DIM)
            outs.append(jnp.dot(p.astype(BF16), kvb_ref[:, vs], preferred_element_type=F32) / l)
        attn.append(jnp.concatenate(outs, axis=1).astype(BF16))
    for c, r in enumerate(rows):
        xn = x_ref[r, :] + jnp.dot(attn[c], wob_ref[...], preferred_element_type=F32)
        o_ref[r, :] = xn
        hn_ref[r, :] = _rms(xn, gn_ref[...]).astype(hn_ref.dtype)


def xattn_prompt(hq_all, w_q, mkv, w_o, layer, x_all, g_next, t):
    d = x_all.shape[1]
    xw = w_q.shape[2]
    nm = mkv.shape[0]
    bm = _pick(t, (512, 256, 128, 64, 32, 16))
    return pl.pallas_call(
        _xattn_prompt_body,
        grid=(t // bm,),
        in_specs=[
            pl.BlockSpec((bm, d), lambda i: (i, 0)),
            pl.BlockSpec((None, d, xw), lambda i: (layer, 0, 0)),
            pl.BlockSpec((nm, 2 * xw), lambda i: (0, 0)),
            pl.BlockSpec((None, xw, d), lambda i: (layer, 0, 0)),
            pl.BlockSpec((bm, d), lambda i: (i, 0)),
            pl.BlockSpec((1, d), lambda i: (0, 0)),
        ],
        out_specs=[pl.BlockSpec((bm, d), lambda i: (i, 0)), pl.BlockSpec((bm, d), lambda i: (i, 0))],
        out_shape=[jax.ShapeDtypeStruct(x_all.shape, F32), jax.ShapeDtypeStruct(x_all.shape, BF16)],
        scratch_shapes=[pltpu.VMEM((d, xw), BF16), pltpu.VMEM((xw, d), BF16), pltpu.VMEM((nm, 2 * xw), BF16)],
        input_output_aliases={4: 0, 0: 1},
        compiler_params=_params(("arbitrary",)),
        name="xattn_prompt",
    )(hq_all, w_q, mkv, w_o, x_all, g_next.reshape(1, d))


def _xq_sample_body(h_ref, wq_ref, o_ref):
    o_ref[...] = jnp.dot(h_ref[...], wq_ref[...].astype(BF16), preferred_element_type=F32) / math.sqrt(X_HEAD_DIM)


def xq_sample(hq_all, w_q, layer, t, b):
    d = hq_all.shape[1]
    xw = w_q.shape[2]
    assert t % b == 0
    return pl.pallas_call(
        _xq_sample_body,
        grid=(1,),
        in_specs=[
            pl.BlockSpec((b, d), lambda i: (t // b, 0)),
            pl.BlockSpec((None, d, xw), lambda i: (layer, 0, 0)),
        ],
        out_specs=pl.BlockSpec((b, xw), lambda i: (0, 0)),
        out_shape=jax.ShapeDtypeStruct((b, xw), F32),
        compiler_params=_params(("arbitrary",)),
        name="xq_sample",
    )(hq_all, w_q)


def _xattn_core_body(q_ref, k_ref, v_ref, ones_ref, o_ref):
    bt, sub, hd = q_ref.shape
    nv = k_ref.shape[0] // (bt * sub)
    k = k_ref[...].reshape(bt, nv, sub, hd)
    prod = (k * q_ref[...][:, None]).reshape(bt * nv * sub, hd).astype(BF16)
    s = jnp.dot(prod, ones_ref[...], preferred_element_type=F32).reshape(bt, nv, sub, hd)

    def fold(x, op):
        step = X_HEADS
        while step < sub:
            x = op(x, pltpu.roll(x, step, 2))
            step *= 2
        return x

    mx = fold(jnp.max(s, axis=1, keepdims=True), jnp.maximum)
    p = jnp.exp(s - mx)
    l = fold(jnp.sum(p, axis=1, keepdims=True), jnp.add)
    acc = fold(jnp.sum(p * v_ref[...].reshape(bt, nv, sub, hd), axis=1, keepdims=True), jnp.add)
    o_ref[...] = (acc / l).reshape(bt, sub, hd)


def xattn_core_sample(q_s, mk_flat, mv_flat, layer, nm):
    b = q_s.shape[0]
    hd = X_HEAD_DIM
    sub = 8
    assert sub % X_HEADS == 0 and nm % (sub // X_HEADS) == 0
    bt = _pick(b, (8, 4, 2, 1))
    nblk = b // bt
    rows = bt * nm * X_HEADS
    q8 = jnp.tile(q_s.reshape(b, X_HEADS, hd), (1, sub // X_HEADS, 1))
    o8 = pl.pallas_call(
        _xattn_core_body,
        grid=(nblk,),
        in_specs=[
            pl.BlockSpec((bt, sub, hd), lambda i: (i, 0, 0)),
            pl.BlockSpec((rows, hd), lambda i: (layer * nblk + i, 0)),
            pl.BlockSpec((rows, hd), lambda i: (layer * nblk + i, 0)),
            pl.BlockSpec((hd, hd), lambda i: (0, 0)),
        ],
        out_specs=pl.BlockSpec((bt, sub, hd), lambda i: (i, 0, 0)),
        out_shape=jax.ShapeDtypeStruct((b, sub, hd), F32),
        compiler_params=_params(("arbitrary",)),
        name="xattn_core_sample",
    )(q8, mk_flat, mv_flat, jnp.ones((hd, hd), BF16))
    return o8[:, :X_HEADS].reshape(b, X_HEADS * hd)


def _xout_sample_body(a_ref, wo_ref, x_ref, gn_ref, hn_any, o_ref, hn_ref):
    del hn_any
    xn = x_ref[...] + jnp.dot(a_ref[...].astype(BF16), wo_ref[...].astype(BF16), preferred_element_type=F32)
    o_ref[...] = xn
    hn_ref[...] = _rms(xn, gn_ref[...]).astype(hn_ref.dtype)


def xout_sample(o_s, w_o, layer, x_all, hn_all, g_next, t):
    d = x_all.shape[1]
    b, xw = o_s.shape
    assert t % b == 0
    off = t // b
    return pl.pallas_call(
        _xout_sample_body,
        grid=(1,),
        in_specs=[
            pl.BlockSpec((b, xw), lambda i: (0, 0)),
            pl.BlockSpec((None, xw, d), lambda i: (layer, 0, 0)),
            pl.BlockSpec((b, d), lambda i: (off, 0)),
            pl.BlockSpec((1, d), lambda i: (0, 0)),
            pl.BlockSpec(memory_space=pl.ANY),
        ],
        out_specs=[pl.BlockSpec((b, d), lambda i: (off, 0)), pl.BlockSpec((b, d), lambda i: (off, 0))],
        out_shape=[jax.ShapeDtypeStruct(x_all.shape, F32), jax.ShapeDtypeStruct(hn_all.shape, hn_all.dtype)],
        input_output_aliases={2: 0, 4: 1},
        compiler_params=_params(("arbitrary",)),
        name="xout_sample",
    )(o_s, w_o, x_all, g_next.reshape(1, d), hn_all)


def kernel(x_prompt, x_sample, state_pool, cache_swa_k, cache_swa_v, cache_mem_k, cache_mem_v, mem_prompt,
           g_ffn1, w_ffn1_gu, w_ffn1_dn, g_mix, w_pool, pool_scale, w_qkv, w_o, sinks,
           g_xq, g_mem, w_xq, w_xkv, w_xo, g_ffn2, w_ffn2_gu, w_ffn2_dn, g_final):
    bp, t, d = x_prompt.shape
    b, s_len, _ = x_sample.shape
    assert bp == 1 and s_len == 1
    depth = g_ffn1.shape[0]
    xw = w_xq.shape[2]
    nm = mem_prompt.shape[1]
    n_kv = d // HEAD_DIM // GQA_GROUP
    kvw = n_kv * HEAD_DIM

    x, h = stack_and_norm(x_prompt[0], x_sample[:, 0], g_ffn1[0])
    cos_t, sin_t = rope_tables(t, b)
    mk_flat = cache_mem_k.reshape(-1, X_HEAD_DIM)
    mv_flat = cache_mem_v.reshape(-1, X_HEAD_DIM)
    state_sm = jnp.swapaxes(state_pool, 1, 2)

    pool_p, pool_s = [], []
    swa_kp, swa_vp, swa_ks, swa_vs = [], [], [], []
    mem_kp, mem_vp = [], []
    y_p = y_s = None
    for layer in range(depth):
        i = layer // 2
        act, wdb = gate_up(h, w_ffn1_gu, w_ffn1_dn, layer)
        x, h = down_norm(act, wdb, x, g_mix[layer], BF16)
        if layer % 2 == 0:
            x, hq, h_last = pool_prompt(x, h, g_mix[layer], w_pool, i, pool_scale[i], g_xq[layer], t)
            x, hq, new_state = pool_sample(x, hq, state_sm, i, g_mix[layer], w_pool, pool_scale[i], g_xq[layer], t)
            pool_p.append(h_last[None, POOL_MAXW - POOL_BUF:])
            pool_s.append(jnp.swapaxes(new_state, 0, 1))
        else:
            qkv = qkv_rope(h, w_qkv, i, cos_t, sin_t, d + kvw, kvw)
            buf = cache_swa_k.shape[2]
            keep = min(WINDOW, t)
            swa_kp.append(qkv[t - keep:t, d:d + kvw].reshape(1, keep, n_kv, HEAD_DIM))
            swa_vp.append(qkv[t - keep:t, d + kvw:].reshape(1, keep, n_kv, HEAD_DIM))
            assert buf == WINDOW
            to_fm = lambda c: jnp.transpose(c, (0, 2, 3, 1)).reshape(b * kvw, buf)
            from_fm = lambda c: jnp.transpose(c.reshape(b, n_kv, HEAD_DIM, buf), (0, 3, 1, 2))
            o_fm, wo_bf16 = swa_prompt(qkv, sinks[i], w_o, i, t, d)
            ks_fm, vs_fm, o_fm = swa_sample(qkv[t:].T, to_fm(cache_swa_k[i]), to_fm(cache_swa_v[i]), sinks[i],
                                            o_fm, n_kv)
            swa_ks.append(from_fm(ks_fm))
            swa_vs.append(from_fm(vs_fm))
            x, hq = proj_residual_norm(o_fm, wo_bf16, x, g_xq[layer])
        mkv = mem_kv(mem_prompt[0], g_mem[layer], w_xkv, layer)
        mem_kp.append(mkv[:, :xw].reshape(1, nm, X_HEADS, X_HEAD_DIM))
        mem_vp.append(mkv[:, xw:].reshape(1, nm, X_HEADS, X_HEAD_DIM))
        q_s = xq_sample(hq, w_xq, layer, t, b)
        x, h = xattn_prompt(hq, w_xq, mkv, w_xo, layer, x, g_ffn2[layer], t)
        o_s = xattn_core_sample(q_s, mk_flat, mv_flat, layer, nm)
        x, h = xout_sample(o_s, w_xo, layer, x, h, g_ffn2[layer], t)
        act, wdb = gate_up(h, w_ffn2_gu, w_ffn2_dn, layer)
        if layer + 1 < depth:
            x, h = down_norm(act, wdb, x, g_ffn1[layer + 1], BF16)
        else:
            (y_p,) = down_norm(act, wdb, x, g_final, F32, emit_x=False, row0=0, nrows=t)
            (y_s,) = down_norm(act, wdb, x, g_final, F32, emit_x=False, row0=t, nrows=b)
    return (y_p[None], y_s[:, None], jnp.stack(pool_p), jnp.stack(pool_s), jnp.stack(swa_kp), jnp.stack(swa_vp),
            jnp.stack(swa_ks), jnp.stack(swa_vs), jnp.stack(mem_kp), jnp.stack(mem_vp))
```

```python
import functools
import math

import jax
import jax.numpy as jnp
from jax import lax
from jax.experimental import pallas as pl
from jax.experimental.pallas import tpu as pltpu

F32 = jnp.float32
BF16 = jnp.bfloat16

RMS_EPS = 1e-6
PAST_LEN = 8192
POOL_WINDOWS = (2, 4, 8, 16)
POOL_MAXW = max(POOL_WINDOWS)
POOL_BUF = POOL_MAXW - 1
HEAD_DIM = 64
GQA_GROUP = 4
WINDOW = 128
ROPE_THETA = 10000.0
X_HEADS = 4
X_HEAD_DIM = 128
LANES = 128
V7X_VMEM_LIMIT = 60 * 1024 * 1024


def _params(sem):
    return pltpu.CompilerParams(dimension_semantics=sem, vmem_limit_bytes=V7X_VMEM_LIMIT)


def _pick(n, candidates):
    for c in candidates:
        if n % c == 0:
            return c
    raise ValueError(f"no block size in {candidates} divides {n}")


def _rms(x, g):
    return x * lax.rsqrt(jnp.mean(x * x, axis=-1, keepdims=True) + RMS_EPS) * g


def _stack_norm_body(xp_ref, xs_ref, g_ref, xo_ref, h_ref):
    nt = pl.num_programs(0) - 1
    b = xs_ref.shape[0]

    @pl.when(pl.program_id(0) < nt)
    def _():
        x = xp_ref[...]
        xo_ref[...] = x
        h_ref[...] = _rms(x, g_ref[...]).astype(h_ref.dtype)

    @pl.when(pl.program_id(0) == nt)
    def _():
        x = xs_ref[...]
        xo_ref[:b, :] = x
        h_ref[:b, :] = _rms(x, g_ref[...]).astype(h_ref.dtype)


def stack_and_norm(x_p, x_s, g):
    t, d = x_p.shape
    b = x_s.shape[0]
    bm = _pick(t, (512, 256, 128, 64, 32, 16))
    nt = t // bm
    assert b <= bm and b % 16 == 0
    row = lambda i: (i, 0)
    return pl.pallas_call(
        _stack_norm_body,
        grid=(nt + 1,),
        in_specs=[pl.BlockSpec((bm, d), lambda i: (jnp.minimum(i, nt - 1), 0)),
                  pl.BlockSpec((b, d), lambda i: (0, 0)),
                  pl.BlockSpec((1, d), lambda i: (0, 0))],
        out_specs=[pl.BlockSpec((bm, d), row), pl.BlockSpec((bm, d), row)],
        out_shape=[jax.ShapeDtypeStruct((t + b, d), F32), jax.ShapeDtypeStruct((t + b, d), BF16)],
        compiler_params=_params(("arbitrary",)),
        name="stack_norm",
    )(x_p, x_s, g.reshape(1, d))


def _gate_up_body(h_ref, wg_ref, wu_ref, wd_ref, o_ref, wdb_ref, wb_ref, *, bf):
    @pl.when(pl.program_id(1) == 0)
    def _():
        wb_ref[:, :bf] = wg_ref[...].astype(BF16)
        wb_ref[:, bf:] = wu_ref[...].astype(BF16)
        wdb_ref[...] = wd_ref[...].astype(BF16)

    bm = h_ref.shape[0]
    nchunk = 4 if bm % 64 == 0 else 1
    cr = bm // nchunk
    rs = [jnp.dot(h_ref[c * cr:(c + 1) * cr, :], wb_ref[...], preferred_element_type=F32) for c in range(nchunk)]
    for c in range(nchunk):
        a = rs[c][:, :bf]
        b = rs[c][:, bf:]
        o_ref[c * cr:(c + 1) * cr, :] = (a / (1.0 + jnp.exp(-a)) * b * 0.5).astype(o_ref.dtype)


def gate_up(h, w_gu, w_dn, layer):
    m, d = h.shape
    f = w_gu.shape[2] // 2
    bf = _pick(f, (512, 256, 128))
    bm = _pick(m, (1664, 1024, 512, 256, 128, 64, 32, 16))
    nf = f // bf
    return pl.pallas_call(
        functools.partial(_gate_up_body, bf=bf),
        grid=(nf, m // bm),
        in_specs=[
            pl.BlockSpec((bm, d), lambda j, i: (i, 0)),
            pl.BlockSpec((None, d, bf), lambda j, i: (layer, 0, j)),
            pl.BlockSpec((None, d, bf), lambda j, i: (layer, 0, j + nf)),
            pl.BlockSpec((None, bf, d), lambda j, i: (layer, j, 0)),
        ],
        out_specs=[
            pl.BlockSpec((bm, bf), lambda j, i: (i, j)),
            pl.BlockSpec((bf, d), lambda j, i: (j, 0)),
        ],
        out_shape=[jax.ShapeDtypeStruct((m, f), BF16), jax.ShapeDtypeStruct((f, d), BF16)],
        scratch_shapes=[pltpu.VMEM((d, 2 * bf), BF16)],
        compiler_params=_params(("arbitrary", "arbitrary")),
        name="gate_up",
    )(h, w_gu, w_gu, w_dn)


def _down_body(a_ref, w_ref, x_ref, g_ref, *out_refs, emit_x):
    xn = x_ref[...] + jnp.dot(a_ref[...], w_ref[...], preferred_element_type=F32)
    if emit_x:
        out_refs[0][...] = xn
    h_ref = out_refs[-1]
    h_ref[...] = _rms(xn, g_ref[...]).astype(h_ref.dtype)


def down_norm(a, w_bf16, x, g_next, h_dtype, emit_x=True, row0=0, nrows=None):
    m, k = a.shape
    d = w_bf16.shape[1]
    nrows = m if nrows is None else nrows
    assert emit_x is False or (row0 == 0 and nrows == m)
    bm = _pick(math.gcd(nrows, row0) if row0 else nrows, (416, 320, 256, 128, 64, 32, 16))
    off = row0 // bm
    row = lambda i: (i, 0)
    src = lambda i: (i + off, 0)
    out_specs = [pl.BlockSpec((bm, d), row)]
    out_shape = [jax.ShapeDtypeStruct((nrows, d), h_dtype)]
    if emit_x:
        out_specs.insert(0, pl.BlockSpec((bm, d), row))
        out_shape.insert(0, jax.ShapeDtypeStruct((m, d), F32))
    return pl.pallas_call(
        functools.partial(_down_body, emit_x=emit_x),
        grid=(nrows // bm,),
        in_specs=[
            pl.BlockSpec((bm, k), src),
            pl.BlockSpec((k, d), lambda i: (0, 0), pipeline_mode=pl.Buffered(1)),
            pl.BlockSpec((bm, d), src),
            pl.BlockSpec((1, d), lambda i: (0, 0)),
        ],
        out_specs=out_specs,
        out_shape=out_shape,
        input_output_aliases={2: 0} if emit_x else {},
        compiler_params=_params(("arbitrary",)),
        name="down_norm",
    )(a, w_bf16, x, g_next.reshape(1, d))


def _proj_res_body(a_ref, w_ref, x_ref, g_ref, xo_ref, h_ref):
    xn = x_ref[...] + lax.dot_general(a_ref[...], w_ref[...], (((0,), (0,)), ((), ())),
                                      preferred_element_type=F32)
    xo_ref[...] = xn
    h_ref[...] = _rms(xn, g_ref[...]).astype(h_ref.dtype)


def proj_residual_norm(a_fm, w_bf16, x, g_next):
    k, m = a_fm.shape
    d = w_bf16.shape[1]
    bm = _pick(m, (640, 512, 256, 128))
    row = lambda i: (i, 0)
    return pl.pallas_call(
        _proj_res_body,
        grid=(m // bm,),
        in_specs=[
            pl.BlockSpec((k, bm), lambda i: (0, i)),
            pl.BlockSpec((k, d), lambda i: (0, 0), pipeline_mode=pl.Buffered(1)),
            pl.BlockSpec((bm, d), row),
            pl.BlockSpec((1, d), lambda i: (0, 0)),
        ],
        out_specs=[pl.BlockSpec((bm, d), row), pl.BlockSpec((bm, d), row)],
        out_shape=[jax.ShapeDtypeStruct((m, d), F32), jax.ShapeDtypeStruct((m, d), BF16)],
        input_output_aliases={2: 0},
        compiler_params=_params(("arbitrary",)),
        name="proj_residual_norm",
    )(a_fm, w_bf16, x, g_next.reshape(1, d))


def _qkv_body(h_ref, w_ref, c_ref, s_ref, o_ref, wb_ref, *, n_rope_tiles):
    @pl.when(pl.program_id(1) == 0)
    def _():
        wb_ref[...] = w_ref[...].astype(BF16)

    bm, bn = o_ref.shape
    is_rope = pl.program_id(0) < n_rope_tiles
    nchunk = 4 if bm % 64 == 0 else 1
    cr = bm // nchunk
    rs = [jnp.dot(h_ref[c * cr:(c + 1) * cr, :], wb_ref[...], preferred_element_type=F32) for c in range(nchunk)]
    lane = lax.broadcasted_iota(jnp.int32, (cr, LANES), 1)
    first_half = (lane % HEAD_DIM) < (HEAD_DIM // 2)
    for c in range(nchunk):
        rows = slice(c * cr, (c + 1) * cr)
        cos = c_ref[rows, :]
        sin = s_ref[rows, :]
        for ci in range(bn // LANES):
            blk = rs[c][:, ci * LANES:(ci + 1) * LANES]
            partner = jnp.where(first_half,
                                pltpu.roll(blk, LANES - HEAD_DIM // 2, 1),
                                pltpu.roll(blk, HEAD_DIM // 2, 1))
            o_ref[rows, ci * LANES:(ci + 1) * LANES] = jnp.where(is_rope, blk * cos + partner * sin, blk)


def qkv_rope(h, w_qkv, layer, cos_t, sin_t, n_rope_cols, bn):
    m, d = h.shape
    n = w_qkv.shape[2]
    assert n % bn == 0 and n_rope_cols % bn == 0 and bn % LANES == 0
    bm = _pick(m, (1664, 1024, 512, 256, 128, 64, 32, 16))
    return pl.pallas_call(
        functools.partial(_qkv_body, n_rope_tiles=n_rope_cols // bn),
        grid=(n // bn, m // bm),
        in_specs=[
            pl.BlockSpec((bm, d), lambda j, i: (i, 0)),
            pl.BlockSpec((None, d, bn), lambda j, i: (layer, 0, j)),
            pl.BlockSpec((bm, LANES), lambda j, i: (i, 0)),
            pl.BlockSpec((bm, LANES), lambda j, i: (i, 0)),
        ],
        out_specs=pl.BlockSpec((bm, bn), lambda j, i: (i, j)),
        out_shape=jax.ShapeDtypeStruct((m, n), F32),
        scratch_shapes=[pltpu.VMEM((d, bn), BF16)],
        compiler_params=_params(("arbitrary", "arbitrary")),
        name="qkv_rope",
    )(h, w_qkv, cos_t, sin_t)


def rope_tables(t_prompt, n_sample):
    half = HEAD_DIM // 2
    inv = ROPE_THETA ** (-jnp.arange(half, dtype=F32) / half)
    pos = jnp.concatenate([jnp.arange(t_prompt), jnp.full((n_sample,), PAST_LEN)]).astype(F32)
    ang = pos[:, None] * inv[None, :]
    cos, sin = jnp.cos(ang), jnp.sin(ang)
    reps = LANES // HEAD_DIM
    return (jnp.tile(jnp.concatenate([cos, cos], axis=1), (1, reps)),
            jnp.tile(jnp.concatenate([-sin, sin], axis=1), (1, reps)))


def _swa_prompt_body(sink_ref, q_ref, kp_ref, kc_ref, vp_ref, vc_ref, wo_ref, o_ref, wob_ref, *, n_kv, nsub):
    n = pl.program_id(0)
    blk = WINDOW
    hd = HEAD_DIM
    assert LANES == 2 * hd
    wob_ref[...] = wo_ref[...].astype(BF16)
    nq = GQA_GROUP * blk
    keys = lax.broadcasted_iota(jnp.int32, (2 * blk, nq), 0)
    qrow = lax.broadcasted_iota(jnp.int32, (2 * blk, nq), 1) % blk
    diff = qrow + blk - keys
    in_window = (diff >= 0) & (diff < WINDOW)
    masks = [in_window & ((keys >= blk) | (n > 0))] + [in_window] * (nsub - 1)
    head_of_col = lax.broadcasted_iota(jnp.int32, (1, nq), 1) // blk
    lane = lax.broadcasted_iota(jnp.int32, (1, LANES), 1)
    log2e = math.log2(math.e)
    qscale = log2e / math.sqrt(hd)
    def mine(kv):
        return (lane >= (kv % 2) * hd) & (lane < (kv % 2 + 1) * hd)

    def window_rows(prev_ref, cur_ref, sub, ps):
        prev = prev_ref[:, ps] if sub == 0 else cur_ref[(sub - 1) * blk:sub * blk, ps]
        return jnp.concatenate([prev, cur_ref[sub * blk:(sub + 1) * blk, ps]], axis=0)

    def scores(item):
        kv, sub = item
        ps = slice((kv // 2) * LANES, (kv // 2 + 1) * LANES)
        kpair = window_rows(kp_ref, kc_ref, sub, ps).astype(BF16)
        parts = []
        for g in range(GQA_GROUP):
            c0 = (kv * GQA_GROUP + g - (g % 2)) * hd
            src = q_ref[sub * blk:(sub + 1) * blk, c0:c0 + LANES] * qscale
            if g % 2 != kv % 2:
                src = pltpu.roll(src, hd, 1)
            parts.append(jnp.where(mine(kv), src, 0.0))
        qs = jnp.concatenate(parts, axis=0).astype(BF16)
        return lax.dot_general(kpair, qs, (((1,), (1,)), ((), ())), preferred_element_type=F32)

    def softmax(item, s):
        kv, sub = item
        s = jnp.where(masks[sub], s, -jnp.inf)
        sink = jnp.zeros((1, nq), F32)
        for g in range(GQA_GROUP):
            sink = jnp.where(head_of_col == g, sink_ref[kv * GQA_GROUP + g] * log2e, sink)
        mx = jnp.maximum(jnp.max(s, axis=0, keepdims=True), sink)
        return jnp.exp2(s - mx).astype(BF16), jnp.exp2(sink - mx)

    def weighted_values(item, pb, sink_term):
        kv, sub = item
        ps = slice((kv // 2) * LANES, (kv // 2 + 1) * LANES)
        vpair = window_rows(vp_ref, vc_ref, sub, ps)
        vaug = jnp.where(mine(kv), vpair, 1.0).astype(BF16)
        ot = lax.dot_general(vaug, pb, (((0,), (0,)), ((), ())), preferred_element_type=F32)
        half = kv % 2
        other = (1 - half) * hd
        denom = ot[other:other + 1, :] + sink_term
        on = (ot[half * hd:(half + 1) * hd, :] * (1.0 / denom)).astype(o_ref.dtype)
        for g in range(GQA_GROUP):
            r0 = (kv * GQA_GROUP + g) * hd
            o_ref[r0:r0 + hd, sub * blk:(sub + 1) * blk] = on[:, g * blk:(g + 1) * blk]

    @pl.when(n < pl.num_programs(0) - 1)
    def _():
        items = [(kv, sub) for sub in range(nsub) for kv in range(n_kv)]
        ahead = 2
        pending = {i: scores(items[i]) for i in range(min(ahead, len(items)))}
        for i, item in enumerate(items):
            pb, sink_term = softmax(item, pending.pop(i))
            if i + ahead < len(items):
                pending[i + ahead] = scores(items[i + ahead])
            weighted_values(item, pb, sink_term)

    @pl.when(n == pl.num_programs(0) - 1)
    def _():
        o_ref[...] = jnp.zeros_like(o_ref)


def swa_prompt(qkv, sinks, w_o, layer, t, d):
    n_kv = d // HEAD_DIM // GQA_GROUP
    kvw = n_kv * HEAD_DIM
    assert t % WINDOW == 0
    nsub = 2 if (t // WINDOW) % 2 == 0 else 1
    nblk = t // (nsub * WINDOW)
    assert d % kvw == 0 and n_kv % 2 == 0 and d % nblk == 0 and (d // nblk) % 16 == 0
    wrows = d // nblk
    kblk = d // kvw
    vblk = kblk + 1
    assert 0 < qkv.shape[0] - t <= nsub * WINDOW
    cur = lambda n: jnp.minimum(n, nblk - 1)
    prev = lambda n: jnp.maximum(cur(n) * nsub - 1, 0)
    return pl.pallas_call(
        functools.partial(_swa_prompt_body, n_kv=n_kv, nsub=nsub),
        grid=(nblk + 1,),
        in_specs=[
            pl.BlockSpec(memory_space=pltpu.SMEM),
            pl.BlockSpec((nsub * WINDOW, d), lambda n: (cur(n), 0)),
            pl.BlockSpec((WINDOW, kvw), lambda n: (prev(n), kblk)),
            pl.BlockSpec((nsub * WINDOW, kvw), lambda n: (cur(n), kblk)),
            pl.BlockSpec((WINDOW, kvw), lambda n: (prev(n), vblk)),
            pl.BlockSpec((nsub * WINDOW, kvw), lambda n: (cur(n), vblk)),
            pl.BlockSpec((None, wrows, d), lambda n: (layer, cur(n), 0)),
        ],
        out_specs=[pl.BlockSpec((d, nsub * WINDOW), lambda n: (0, n)),
                   pl.BlockSpec((wrows, d), lambda n: (cur(n), 0))],
        out_shape=[jax.ShapeDtypeStruct((d, qkv.shape[0]), BF16), jax.ShapeDtypeStruct((d, d), BF16)],
        compiler_params=_params(("arbitrary",)),
        name="swa_prompt",
    )(sinks, qkv, qkv, qkv, qkv, qkv, w_o)


def _swa_sample_body(sink_ref, qkvt_ref, k_ref, v_ref, o_any, ko_ref, vo_ref, o_ref, acc_ref, *, n_kv, bt):
    del o_any
    i = pl.program_id(0)
    nk = k_ref.shape[1]
    nb = qkvt_ref.shape[1]
    kvw = n_kv * HEAD_DIM
    n_heads = n_kv * GQA_GROUP
    nq = n_heads * HEAD_DIM
    sub = 8

    @pl.when(i == 0)
    def _():
        acc_ref[...] = jnp.zeros_like(acc_ref)

    key_lane = lax.broadcasted_iota(jnp.int32, (1, nk), 1)
    sublane = lax.broadcasted_iota(jnp.int32, (sub, nk), 0)
    head_row = lax.broadcasted_iota(jnp.int32, (n_heads, 1), 0)
    sink = jnp.zeros((n_heads, 1), F32)
    for hd in range(n_heads):
        sink = jnp.where(head_row == hd, sink_ref[hd], sink)
    qt = (qkvt_ref[:nq, :] * (1.0 / math.sqrt(HEAD_DIM))).astype(BF16)
    knt = qkvt_ref[nq:nq + kvw, :]
    vnt = qkvt_ref[nq + kvw:, :]
    gs = bt
    wide_row = lax.broadcasted_iota(jnp.int32, (nb, gs * nk), 0)
    wide_lane_group = lax.broadcasted_iota(jnp.int32, (nb, gs * nk), 1) // nk
    tall_lane_group = lax.broadcasted_iota(jnp.int32, (gs * nk, nb), 0) // nk
    tall_lane = lax.broadcasted_iota(jnp.int32, (gs * nk, nb), 1)
    qcols_of = [jnp.dot(qt, (wide_row == wide_lane_group + (i * bt + g0)).astype(BF16),
                        preferred_element_type=F32) for g0 in range(0, bt, gs)]
    kfulls, vfulls = [], []
    assert nb == nk
    for j in range(bt):
        to_last = nk - 1 - (i * bt + j)
        kn = pltpu.roll(knt, to_last, 1)
        vn = pltpu.roll(vnt, to_last, 1)
        rows = slice(j * kvw, (j + 1) * kvw)
        kfulls.append(jnp.where(key_lane == nk - 1, kn, pltpu.roll(k_ref[rows, :], nk - 1, 1)))
        vfulls.append(jnp.where(key_lane == nk - 1, vn, pltpu.roll(v_ref[rows, :], nk - 1, 1)))
        ko_ref[rows, :] = kfulls[j]
        vo_ref[rows, :] = vfulls[j]
    all_scores = []
    for j in range(bt):
        qcols = qcols_of[j // gs][:, (j % gs) * nk:(j % gs + 1) * nk]
        kfull = kfulls[j]
        tiles = []
        for tile in range(n_heads // sub):
            st = jnp.zeros((sub, nk), F32)
            for r in range(sub):
                hidx = tile * sub + r
                kv = hidx // GQA_GROUP
                prod = kfull[kv * HEAD_DIM:(kv + 1) * HEAD_DIM] * qcols[hidx * HEAD_DIM:(hidx + 1) * HEAD_DIM]
                red = jnp.sum(prod.reshape(HEAD_DIM // sub, sub, nk), axis=0)
                step = sub // 2
                while step >= 1:
                    red = red + pltpu.roll(red, step, 0)
                    step //= 2
                st = jnp.where(sublane == r, red, st)
            tiles.append(st)
        all_scores.append(jnp.concatenate(tiles, axis=0))
    probs = []
    for s in all_scores:
        mx = jnp.maximum(jnp.max(s, axis=1, keepdims=True), sink)
        p = jnp.exp(s - mx)
        probs.append(p / (jnp.sum(p, axis=1, keepdims=True) + jnp.exp(sink - mx)))
    pvs = []
    for j in range(bt):
        pn, vfull = probs[j], vfulls[j]
        pvs.append(jnp.concatenate(
            [vfull[(hidx // GQA_GROUP) * HEAD_DIM:(hidx // GQA_GROUP + 1) * HEAD_DIM] * pn[hidx:hidx + 1, :]
             for hidx in range(n_heads)], axis=0).astype(BF16))
        if len(pvs) == gs:
            first = i * bt + j + 1 - gs
            acc_ref[...] += jnp.dot(jnp.concatenate(pvs, axis=1),
                                    (tall_lane_group + first == tall_lane).astype(BF16),
                                    preferred_element_type=F32)
            pvs = []

    @pl.when(i == pl.num_programs(0) - 1)
    def _():
        o_ref[...] = acc_ref[...].astype(o_ref.dtype)


def swa_sample(qkvt, kt, vt, sinks, o_fm, n_kv):
    nrow, b = qkvt.shape
    kvw = n_kv * HEAD_DIM
    d = nrow - 2 * kvw
    nk = kt.shape[1]
    t = o_fm.shape[1] - b
    assert d == n_kv * GQA_GROUP * HEAD_DIM and (n_kv * GQA_GROUP) % 8 == 0 and t % b == 0
    bt = _pick(b, (8, 4, 2, 1))
    cache_spec = pl.BlockSpec((bt * kvw, nk), lambda i: (i, 0))
    return pl.pallas_call(
        functools.partial(_swa_sample_body, n_kv=n_kv, bt=bt),
        grid=(b // bt,),
        in_specs=[
            pl.BlockSpec(memory_space=pltpu.SMEM),
            pl.BlockSpec((nrow, b), lambda i: (0, 0)),
            cache_spec,
            cache_spec,
            pl.BlockSpec(memory_space=pl.ANY),
        ],
        out_specs=[cache_spec, cache_spec, pl.BlockSpec((d, b), lambda i: (0, t // b))],
        out_shape=[jax.ShapeDtypeStruct(kt.shape, F32), jax.ShapeDtypeStruct(vt.shape, F32),
                   jax.ShapeDtypeStruct(o_fm.shape, o_fm.dtype)],
        scratch_shapes=[pltpu.VMEM((d, b), F32)],
        input_output_aliases={4: 2},
        compiler_params=_params(("arbitrary",)),
        name="swa_sample",
    )(sinks, qkvt, kt, vt, o_fm)


def _pool_prompt_body(x_ref, g_ref, w_ref, sc_ref, gn_ref, hn_any, o_ref, hn_ref, hl_ref, wb_ref, *, gc):
    del hn_any
    i = pl.program_id(0)

    @pl.when(i == 0)
    def _():
        wb_ref[...] = w_ref[...].astype(BF16)
        hl_ref[...] = jnp.zeros_like(hl_ref)

    bm = x_ref.shape[0]
    x = x_ref[...]
    h = _rms(x, g_ref[...])
    hprev = hl_ref[...]
    hl_ref[...] = h[bm - POOL_MAXW:, :]
    pos1 = (lax.broadcasted_iota(jnp.int32, (bm, 1), 0) + i * bm + 1).astype(F32)
    for gi, w in enumerate(POOL_WINDOWS):
        cs = slice(gi * gc, (gi + 1) * gc)
        hg = h[:, cs]
        acc = jnp.concatenate([hprev[:, cs], hg], axis=0)
        step = 1
        while step < w:
            acc = acc + pltpu.roll(acc, step, 0)
            step *= 2
        inv_cnt = 1.0 / jnp.minimum(jnp.float32(w), pos1)
        pooled = (acc[POOL_MAXW:, :] * inv_cnt - hg).astype(BF16)
        y = jnp.dot(pooled, wb_ref[gi], preferred_element_type=F32)
        o_ref[:, cs] = x[:, cs] + y * sc_ref[:, cs]
    hn_ref[...] = _rms(o_ref[...], gn_ref[...]).astype(hn_ref.dtype)


def pool_prompt(x_all, hn_all, g, w_pool, layer, scale, g_next, t):
    d = x_all.shape[1]
    _, ng, gc, _ = w_pool.shape
    bm = _pick(t, (512, 256, 128, 64, 32, 16))
    return pl.pallas_call(
        functools.partial(_pool_prompt_body, gc=gc),
        grid=(t // bm,),
        in_specs=[
            pl.BlockSpec((bm, d), lambda i: (i, 0)),
            pl.BlockSpec((1, d), lambda i: (0, 0)),
            pl.BlockSpec((None, ng, gc, gc), lambda i: (layer, 0, 0, 0)),
            pl.BlockSpec((1, d), lambda i: (0, 0)),
            pl.BlockSpec((1, d), lambda i: (0, 0)),
            pl.BlockSpec(memory_space=pl.ANY),
        ],
        out_specs=[
            pl.BlockSpec((bm, d), lambda i: (i, 0)),
            pl.BlockSpec((bm, d), lambda i: (i, 0)),
            pl.BlockSpec((POOL_MAXW, d), lambda i: (0, 0)),
        ],
        out_shape=[jax.ShapeDtypeStruct(x_all.shape, F32), jax.ShapeDtypeStruct(hn_all.shape, hn_all.dtype),
                   jax.ShapeDtypeStruct((POOL_MAXW, d), F32)],
        scratch_shapes=[pltpu.VMEM((ng, gc, gc), BF16)],
        input_output_aliases={0: 0, 5: 1},
        compiler_params=_params(("arbitrary",)),
        name="pool_prompt",
    )(x_all, g.reshape(1, d), w_pool, scale.reshape(1, d), g_next.reshape(1, d), hn_all)


def _pool_sample_body(x_ref, st_ref, g_ref, w_ref, sc_ref, gn_ref, hn_any, o_ref, hn_ref, ns_ref, *, gc):
    del hn_any
    x = x_ref[...]
    h = _rms(x, g_ref[...])
    ns_ref[:POOL_BUF - 1] = st_ref[1:]
    ns_ref[POOL_BUF - 1] = h
    for gi, w in enumerate(POOL_WINDOWS):
        cs = slice(gi * gc, (gi + 1) * gc)
        hg = h[:, cs]
        tot = hg + jnp.sum(st_ref[POOL_BUF - (w - 1):, :, cs], axis=0)
        cnt = float(min(w, PAST_LEN + 1))
        pooled = (tot / cnt - hg).astype(BF16)
        y = jnp.dot(pooled, w_ref[gi].astype(BF16), preferred_element_type=F32)
        o_ref[:, cs] = x[:, cs] + y * sc_ref[:, cs]
    hn_ref[...] = _rms(o_ref[...], gn_ref[...]).astype(hn_ref.dtype)


def pool_sample(x_all, hn_all, state, layer, g, w_pool, scale, g_next, t):
    d = x_all.shape[1]
    b = state.shape[2]
    _, ng, gc, _ = w_pool.shape
    bt = _pick(b, (32, 16, 8))
    assert t % bt == 0
    off = t // bt
    return pl.pallas_call(
        functools.partial(_pool_sample_body, gc=gc),
        grid=(b // bt,),
        in_specs=[
            pl.BlockSpec((bt, d), lambda i: (off + i, 0)),
            pl.BlockSpec((None, POOL_BUF, bt, d), lambda i: (layer, 0, i, 0)),
            pl.BlockSpec((1, d), lambda i: (0, 0)),
            pl.BlockSpec((None, ng, gc, gc), lambda i: (layer, 0, 0, 0)),
            pl.BlockSpec((1, d), lambda i: (0, 0)),
            pl.BlockSpec((1, d), lambda i: (0, 0)),
            pl.BlockSpec(memory_space=pl.ANY),
        ],
        out_specs=[
            pl.BlockSpec((bt, d), lambda i: (off + i, 0)),
            pl.BlockSpec((bt, d), lambda i: (off + i, 0)),
            pl.BlockSpec((POOL_BUF, bt, d), lambda i: (0, i, 0)),
        ],
        out_shape=[jax.ShapeDtypeStruct(x_all.shape, F32), jax.ShapeDtypeStruct(hn_all.shape, hn_all.dtype),
                   jax.ShapeDtypeStruct((POOL_BUF, b, d), F32)],
        input_output_aliases={0: 0, 6: 1},
        compiler_params=_params(("arbitrary",)),
        name="pool_sample",
    )(x_all, state, g.reshape(1, d), w_pool, scale.reshape(1, d), g_next.reshape(1, d), hn_all)


def _mem_kv_body(m_ref, g_ref, w_ref, o_ref):
    h = _rms(m_ref[...], g_ref[...]).astype(BF16)
    o_ref[...] = jnp.dot(h, w_ref[...].astype(BF16), preferred_element_type=F32)


def mem_kv(mem, g, w_kv, layer):
    m, d = mem.shape
    n = w_kv.shape[2]
    bn = _pick(n, (512, 256, 128))
    return pl.pallas_call(
        _mem_kv_body,
        grid=(n // bn,),
        in_specs=[
            pl.BlockSpec((m, d), lambda j: (0, 0)),
            pl.BlockSpec((1, d), lambda j: (0, 0)),
            pl.BlockSpec((None, d, bn), lambda j: (layer, 0, j)),
        ],
        out_specs=pl.BlockSpec((m, bn), lambda j: (0, j)),
        out_shape=jax.ShapeDtypeStruct((m, n), F32),
        compiler_params=_params(("arbitrary",)),
        name="mem_kv",
    )(mem, g.reshape(1, d), w_kv)


def _xattn_prompt_body(h_ref, wq_ref, kv_ref, wo_ref, x_ref, gn_ref, o_ref, hn_ref, wqb_ref, wob_ref, kvb_ref):
    @pl.when(pl.program_id(0) == 0)
    def _():
        wqb_ref[...] = wq_ref[...].astype(BF16)
        wob_ref[...] = wo_ref[...].astype(BF16)
        kvb_ref[...] = kv_ref[...].astype(BF16)

    xw = X_HEADS * X_HEAD_DIM
    bm = h_ref.shape[0]
    nchunk = 2 if bm % 32 == 0 else 1
    rows = [slice(c * (bm // nchunk), (c + 1) * (bm // nchunk)) for c in range(nchunk)]
    qs = [jnp.dot(h_ref[r, :], wqb_ref[...], preferred_element_type=F32) / math.sqrt(X_HEAD_DIM) for r in rows]
    scores = [[lax.dot_general(kvb_ref[:, hd * X_HEAD_DIM:(hd + 1) * X_HEAD_DIM],
                               q[:, hd * X_HEAD_DIM:(hd + 1) * X_HEAD_DIM].astype(BF16), (((1,), (1,)), ((), ())),
                               preferred_element_type=F32) for hd in range(X_HEADS)] for q in qs]
    attn = []
    for c in range(nchunk):
        outs = []
        for hd in range(X_HEADS):
            s = scores[c][hd]
            p = jnp.exp(s - jnp.max(s, axis=0, keepdims=True))
            l = jnp.sum(p, axis=0, keepdims=True)
            vs = slice(xw + hd * X_HEAD_DIM, xw + (hd + 1) * X_HEAD_DIM)
            outs.append(lax.dot_general(kvb_ref[:, vs], p.astype(BF16), (((0,), (0,)), ((), ())),
                                        preferred_element_type=F32) / l)
        attn.append(jnp.concatenate(outs, axis=0).astype(BF16))
    for c, r in enumerate(rows):
        xn = x_ref[r, :] + lax.dot_general(attn[c], wob_ref[...], (((0,), (0,)), ((), ())),
                                           preferred_element_type=F32)
        o_ref[r, :] = xn
        hn_ref[r, :] = _rms(xn, gn_ref[...]).astype(hn_ref.dtype)


def xattn_prompt(hq_all, w_q, mkv, w_o, layer, x_all, g_next, t):
    d = x_all.shape[1]
    xw = w_q.shape[2]
    nm = mkv.shape[0]
    bm = _pick(t, (512, 256, 128, 64, 32, 16))
    return pl.pallas_call(
        _xattn_prompt_body,
        grid=(t // bm,),
        in_specs=[
            pl.BlockSpec((bm, d), lambda i: (i, 0)),
            pl.BlockSpec((None, d, xw), lambda i: (layer, 0, 0)),
            pl.BlockSpec((nm, 2 * xw), lambda i: (0, 0)),
            pl.BlockSpec((None, xw, d), lambda i: (layer, 0, 0)),
            pl.BlockSpec((bm, d), lambda i: (i, 0)),
            pl.BlockSpec((1, d), lambda i: (0, 0)),
        ],
        out_specs=[pl.BlockSpec((bm, d), lambda i: (i, 0)), pl.BlockSpec((bm, d), lambda i: (i, 0))],
        out_shape=[jax.ShapeDtypeStruct(x_all.shape, F32), jax.ShapeDtypeStruct(x_all.shape, BF16)],
        scratch_shapes=[pltpu.VMEM((d, xw), BF16), pltpu.VMEM((xw, d), BF16), pltpu.VMEM((nm, 2 * xw), BF16)],
        input_output_aliases={4: 0, 0: 1},
        compiler_params=_params(("arbitrary",)),
        name="xattn_prompt",
    )(hq_all, w_q, mkv, w_o, x_all, g_next.reshape(1, d))


def _xq_sample_body(h_ref, wq_ref, o_ref):
    o_ref[...] = jnp.dot(h_ref[...], wq_ref[...].astype(BF16), preferred_element_type=F32) / math.sqrt(X_HEAD_DIM)


def xq_sample(hq_all, w_q, layer, t, b):
    d = hq_all.shape[1]
    xw = w_q.shape[2]
    assert t % b == 0
    return pl.pallas_call(
        _xq_sample_body,
        grid=(1,),
        in_specs=[
            pl.BlockSpec((b, d), lambda i: (t // b, 0)),
            pl.BlockSpec((None, d, xw), lambda i: (layer, 0, 0)),
        ],
        out_specs=pl.BlockSpec((b, xw), lambda i: (0, 0)),
        out_shape=jax.ShapeDtypeStruct((b, xw), F32),
        compiler_params=_params(("arbitrary",)),
        name="xq_sample",
    )(hq_all, w_q)


def _xattn_core_body(q_ref, k_ref, v_ref, ones_ref, o_ref):
    bt, sub, hd = q_ref.shape
    nv = k_ref.shape[0] // (bt * sub)
    k = k_ref[...].reshape(bt, nv, sub, hd)
    prod = (k * q_ref[...][:, None]).reshape(bt * nv * sub, hd).astype(BF16)
    s = jnp.dot(prod, ones_ref[...], preferred_element_type=F32).reshape(bt, nv, sub, hd)

    def fold(x, op):
        step = X_HEADS
        while step < sub:
            x = op(x, pltpu.roll(x, step, 2))
            step *= 2
        return x

    mx = fold(jnp.max(s, axis=1, keepdims=True), jnp.maximum)
    p = jnp.exp(s - mx)
    l = fold(jnp.sum(p, axis=1, keepdims=True), jnp.add)
    acc = fold(jnp.sum(p * v_ref[...].reshape(bt, nv, sub, hd), axis=1, keepdims=True), jnp.add)
    o_ref[...] = (acc / l).reshape(bt, sub, hd)


def xattn_core_sample(q_s, mk_flat, mv_flat, layer, nm):
    b = q_s.shape[0]
    hd = X_HEAD_DIM
    sub = 8
    assert sub % X_HEADS == 0 and nm % (sub // X_HEADS) == 0
    bt = _pick(b, (8, 4, 2, 1))
    nblk = b // bt
    rows = bt * nm * X_HEADS
    q8 = jnp.tile(q_s.reshape(b, X_HEADS, hd), (1, sub // X_HEADS, 1))
    o8 = pl.pallas_call(
        _xattn_core_body,
        grid=(nblk,),
        in_specs=[
            pl.BlockSpec((bt, sub, hd), lambda i: (i, 0, 0)),
            pl.BlockSpec((rows, hd), lambda i: (layer * nblk + i, 0)),
            pl.BlockSpec((rows, hd), lambda i: (layer * nblk + i, 0)),
            pl.BlockSpec((hd, hd), lambda i: (0, 0)),
        ],
        out_specs=pl.BlockSpec((bt, sub, hd), lambda i: (i, 0, 0)),
        out_shape=jax.ShapeDtypeStruct((b, sub, hd), F32),
        compiler_params=_params(("arbitrary",)),
        name="xattn_core_sample",
    )(q8, mk_flat, mv_flat, jnp.ones((hd, hd), BF16))
    return o8[:, :X_HEADS].reshape(b, X_HEADS * hd)


def _xout_sample_body(a_ref, wo_ref, x_ref, gn_ref, hn_any, o_ref, hn_ref):
    del hn_any
    xn = x_ref[...] + jnp.dot(a_ref[...].astype(BF16), wo_ref[...].astype(BF16), preferred_element_type=F32)
    o_ref[...] = xn
    hn_ref[...] = _rms(xn, gn_ref[...]).astype(hn_ref.dtype)


def xout_sample(o_s, w_o, layer, x_all, hn_all, g_next, t):
    d = x_all.shape[1]
    b, xw = o_s.shape
    assert t % b == 0
    off = t // b
    return pl.pallas_call(
        _xout_sample_body,
        grid=(1,),
        in_specs=[
            pl.BlockSpec((b, xw), lambda i: (0, 0)),
            pl.BlockSpec((None, xw, d), lambda i: (layer, 0, 0)),
            pl.BlockSpec((b, d), lambda i: (off, 0)),
            pl.BlockSpec((1, d), lambda i: (0, 0)),
            pl.BlockSpec(memory_space=pl.ANY),
        ],
        out_specs=[pl.BlockSpec((b, d), lambda i: (off, 0)), pl.BlockSpec((b, d), lambda i: (off, 0))],
        out_shape=[jax.ShapeDtypeStruct(x_all.shape, F32), jax.ShapeDtypeStruct(hn_all.shape, hn_all.dtype)],
        input_output_aliases={2: 0, 4: 1},
        compiler_params=_params(("arbitrary",)),
        name="xout_sample",
    )(o_s, w_o, x_all, g_next.reshape(1, d), hn_all)


def kernel(x_prompt, x_sample, state_pool, cache_swa_k, cache_swa_v, cache_mem_k, cache_mem_v, mem_prompt,
           g_ffn1, w_ffn1_gu, w_ffn1_dn, g_mix, w_pool, pool_scale, w_qkv, w_o, sinks,
           g_xq, g_mem, w_xq, w_xkv, w_xo, g_ffn2, w_ffn2_gu, w_ffn2_dn, g_final):
    bp, t, d = x_prompt.shape
    b, s_len, _ = x_sample.shape
    assert bp == 1 and s_len == 1
    depth = g_ffn1.shape[0]
    xw = w_xq.shape[2]
    nm = mem_prompt.shape[1]
    n_kv = d // HEAD_DIM // GQA_GROUP
    kvw = n_kv * HEAD_DIM

    x, h = stack_and_norm(x_prompt[0], x_sample[:, 0], g_ffn1[0])
    cos_t, sin_t = rope_tables(t, b)
    mk_flat = cache_mem_k.reshape(-1, X_HEAD_DIM)
    mv_flat = cache_mem_v.reshape(-1, X_HEAD_DIM)
    state_sm = jnp.swapaxes(state_pool, 1, 2)

    pool_p, pool_s = [], []
    swa_kp, swa_vp, swa_ks, swa_vs = [], [], [], []
    mem_kp, mem_vp = [], []
    y_p = y_s = None
    for layer in range(depth):
        i = layer // 2
        act, wdb = gate_up(h, w_ffn1_gu, w_ffn1_dn, layer)
        x, h = down_norm(act, wdb, x, g_mix[layer], BF16)
        if layer % 2 == 0:
            x, hq, h_last = pool_prompt(x, h, g_mix[layer], w_pool, i, pool_scale[i], g_xq[layer], t)
            x, hq, new_state = pool_sample(x, hq, state_sm, i, g_mix[layer], w_pool, pool_scale[i], g_xq[layer], t)
            pool_p.append(h_last[None, POOL_MAXW - POOL_BUF:])
            pool_s.append(jnp.swapaxes(new_state, 0, 1))
        else:
            qkv = qkv_rope(h, w_qkv, i, cos_t, sin_t, d + kvw, kvw)
            buf = cache_swa_k.shape[2]
            keep = min(WINDOW, t)
            swa_kp.append(qkv[t - keep:t, d:d + kvw].reshape(1, keep, n_kv, HEAD_DIM))
            swa_vp.append(qkv[t - keep:t, d + kvw:].reshape(1, keep, n_kv, HEAD_DIM))
            assert buf == WINDOW
            to_fm = lambda c: jnp.transpose(c, (0, 2, 3, 1)).reshape(b * kvw, buf)
            from_fm = lambda c: jnp.transpose(c.reshape(b, n_kv, HEAD_DIM, buf), (0, 3, 1, 2))
            o_fm, wo_bf16 = swa_prompt(qkv, sinks[i], w_o, i, t, d)
            ks_fm, vs_fm, o_fm = swa_sample(qkv[t:].T, to_fm(cache_swa_k[i]), to_fm(cache_swa_v[i]), sinks[i],
                                            o_fm, n_kv)
            swa_ks.append(from_fm(ks_fm))
            swa_vs.append(from_fm(vs_fm))
            x, hq = proj_residual_norm(o_fm, wo_bf16, x, g_xq[layer])
        mkv = mem_kv(mem_prompt[0], g_mem[layer], w_xkv, layer)
        mem_kp.append(mkv[:, :xw].reshape(1, nm, X_HEADS, X_HEAD_DIM))
        mem_vp.append(mkv[:, xw:].reshape(1, nm, X_HEADS, X_HEAD_DIM))
        q_s = xq_sample(hq, w_xq, layer, t, b)
        x, h = xattn_prompt(hq, w_xq, mkv, w_xo, layer, x, g_ffn2[layer], t)
        o_s = xattn_core_sample(q_s, mk_flat, mv_flat, layer, nm)
        x, h = xout_sample(o_s, w_xo, layer, x, h, g_ffn2[layer], t)
        act, wdb = gate_up(h, w_ffn2_gu, w_ffn2_dn, layer)
        if layer + 1 < depth:
            x, h = down_norm(act, wdb, x, g_ffn1[layer + 1], BF16)
        else:
            (y_p,) = down_norm(act, wdb, x, g_final, F32, emit_x=False, row0=0, nrows=t)
            (y_s,) = down_norm(act, wdb, x, g_final, F32, emit_x=False, row0=t, nrows=b)
    return (y_p[None], y_s[:, None], jnp.stack(pool_p), jnp.stack(pool_s), jnp.stack(swa_kp), jnp.stack(swa_vp),
            jnp.stack(swa_ks), jnp.stack(swa_vs), jnp.stack(mem_kp), jnp.stack(mem_vp))
```

```python
import functools
import math

import jax
import jax.numpy as jnp
from jax import lax
from jax.experimental import pallas as pl
from jax.experimental.pallas import tpu as pltpu

F32 = jnp.float32
BF16 = jnp.bfloat16

RMS_EPS = 1e-6
PAST_LEN = 8192
POOL_WINDOWS = (2, 4, 8, 16)
POOL_MAXW = max(POOL_WINDOWS)
POOL_BUF = POOL_MAXW - 1
HEAD_DIM = 64
GQA_GROUP = 4
WINDOW = 128
ROPE_THETA = 10000.0
X_HEADS = 4
X_HEAD_DIM = 128
LANES = 128
V7X_VMEM_LIMIT = 60 * 1024 * 1024


def _params(sem):
    return pltpu.CompilerParams(dimension_semantics=sem, vmem_limit_bytes=V7X_VMEM_LIMIT)


def _pick(n, candidates):
    for c in candidates:
        if n % c == 0:
            return c
    raise ValueError(f"no block size in {candidates} divides {n}")


def _rms(x, g):
    return x * lax.rsqrt(jnp.mean(x * x, axis=-1, keepdims=True) + RMS_EPS) * g


def _stack_norm_body(xp_ref, xs_ref, g_ref, xo_ref, h_ref):
    nt = pl.num_programs(0) - 1
    b = xs_ref.shape[0]

    @pl.when(pl.program_id(0) < nt)
    def _():
        x = xp_ref[...]
        xo_ref[...] = x
        h_ref[...] = _rms(x, g_ref[...]).astype(h_ref.dtype)

    @pl.when(pl.program_id(0) == nt)
    def _():
        x = xs_ref[...]
        xo_ref[:b, :] = x
        h_ref[:b, :] = _rms(x, g_ref[...]).astype(h_ref.dtype)


def stack_and_norm(x_p, x_s, g):
    t, d = x_p.shape
    b = x_s.shape[0]
    bm = _pick(t, (512, 256, 128, 64, 32, 16))
    nt = t // bm
    assert b <= bm and b % 16 == 0
    row = lambda i: (i, 0)
    return pl.pallas_call(
        _stack_norm_body,
        grid=(nt + 1,),
        in_specs=[pl.BlockSpec((bm, d), lambda i: (jnp.minimum(i, nt - 1), 0)),
                  pl.BlockSpec((b, d), lambda i: (0, 0)),
                  pl.BlockSpec((1, d), lambda i: (0, 0))],
        out_specs=[pl.BlockSpec((bm, d), row), pl.BlockSpec((bm, d), row)],
        out_shape=[jax.ShapeDtypeStruct((t + b, d), F32), jax.ShapeDtypeStruct((t + b, d), BF16)],
        compiler_params=_params(("arbitrary",)),
        name="stack_norm",
    )(x_p, x_s, g.reshape(1, d))


def _gate_up_body(h_ref, wg_ref, wu_ref, wd_ref, o_ref, wdb_ref, wb_ref, *, bf):
    @pl.when(pl.program_id(1) == 0)
    def _():
        wb_ref[:, :bf] = wg_ref[...].astype(BF16)
        wb_ref[:, bf:] = wu_ref[...].astype(BF16)
        wdb_ref[...] = wd_ref[...].astype(BF16)

    bm = h_ref.shape[0]
    nchunk = 8 if bm % 128 == 0 else (4 if bm % 64 == 0 else 1)
    cr = bm // nchunk
    rs = [jnp.dot(h_ref[c * cr:(c + 1) * cr, :], wb_ref[...], preferred_element_type=F32) for c in range(nchunk)]
    for c in range(nchunk):
        a = rs[c][:, :bf]
        b = rs[c][:, bf:]
        o_ref[c * cr:(c + 1) * cr, :] = (a / (1.0 + jnp.exp(-a)) * b * 0.5).astype(o_ref.dtype)


def gate_up(h, w_gu, w_dn, layer):
    m, d = h.shape
    f = w_gu.shape[2] // 2
    bf = _pick(f, (512, 256, 128))
    bm = _pick(m, (1664, 1024, 512, 256, 128, 64, 32, 16))
    nf = f // bf
    return pl.pallas_call(
        functools.partial(_gate_up_body, bf=bf),
        grid=(nf, m // bm),
        in_specs=[
            pl.BlockSpec((bm, d), lambda j, i: (i, 0)),
            pl.BlockSpec((None, d, bf), lambda j, i: (layer, 0, j)),
            pl.BlockSpec((None, d, bf), lambda j, i: (layer, 0, j + nf)),
            pl.BlockSpec((None, bf, d), lambda j, i: (layer, j, 0)),
        ],
        out_specs=[
            pl.BlockSpec((bm, bf), lambda j, i: (i, j)),
            pl.BlockSpec((bf, d), lambda j, i: (j, 0)),
        ],
        out_shape=[jax.ShapeDtypeStruct((m, f), BF16), jax.ShapeDtypeStruct((f, d), BF16)],
        scratch_shapes=[pltpu.VMEM((d, 2 * bf), BF16)],
        compiler_params=_params(("arbitrary", "arbitrary")),
        name="gate_up",
    )(h, w_gu, w_gu, w_dn)


def _down_body(a_ref, w_ref, x_ref, g_ref, *out_refs, emit_x):
    xn = x_ref[...] + jnp.dot(a_ref[...], w_ref[...], preferred_element_type=F32)
    if emit_x:
        out_refs[0][...] = xn
    h_ref = out_refs[-1]
    h_ref[...] = _rms(xn, g_ref[...]).astype(h_ref.dtype)


def down_norm(a, w_bf16, x, g_next, h_dtype, emit_x=True, row0=0, nrows=None):
    m, k = a.shape
    d = w_bf16.shape[1]
    nrows = m if nrows is None else nrows
    bm = _pick(math.gcd(nrows, row0) if row0 else nrows, (416, 320, 256, 128, 64, 32, 16))
    off = row0 // bm
    row = lambda i: (i, 0)
    src = lambda i: (i + off, 0)
    out_specs = [pl.BlockSpec((bm, d), row)]
    out_shape = [jax.ShapeDtypeStruct((nrows, d), h_dtype)]
    if emit_x:
        out_specs.insert(0, pl.BlockSpec((bm, d), src))
        out_shape.insert(0, jax.ShapeDtypeStruct((m, d), F32))
    return pl.pallas_call(
        functools.partial(_down_body, emit_x=emit_x),
        grid=(nrows // bm,),
        in_specs=[
            pl.BlockSpec((bm, k), src),
            pl.BlockSpec((k, d), lambda i: (0, 0), pipeline_mode=pl.Buffered(1)),
            pl.BlockSpec((bm, d), src),
            pl.BlockSpec((1, d), lambda i: (0, 0)),
        ],
        out_specs=out_specs,
        out_shape=out_shape,
        input_output_aliases={2: 0} if emit_x else {},
        compiler_params=_params(("arbitrary",)),
        name="down_norm",
    )(a, w_bf16, x, g_next.reshape(1, d))


def _proj_res_body(a_ref, w_ref, x_ref, g_ref, xo_ref, h_ref):
    xn = x_ref[...] + lax.dot_general(a_ref[...], w_ref[...], (((0,), (0,)), ((), ())),
                                      preferred_element_type=F32)
    xo_ref[...] = xn
    h_ref[...] = _rms(xn, g_ref[...]).astype(h_ref.dtype)


def proj_residual_norm(a_fm, w_bf16, x, g_next):
    k, m = a_fm.shape
    d = w_bf16.shape[1]
    bm = _pick(m, (640, 512, 256, 128))
    row = lambda i: (i, 0)
    return pl.pallas_call(
        _proj_res_body,
        grid=(m // bm,),
        in_specs=[
            pl.BlockSpec((k, bm), lambda i: (0, i)),
            pl.BlockSpec((k, d), lambda i: (0, 0), pipeline_mode=pl.Buffered(1)),
            pl.BlockSpec((bm, d), row),
            pl.BlockSpec((1, d), lambda i: (0, 0)),
        ],
        out_specs=[pl.BlockSpec((bm, d), row), pl.BlockSpec((bm, d), row)],
        out_shape=[jax.ShapeDtypeStruct((m, d), F32), jax.ShapeDtypeStruct((m, d), BF16)],
        input_output_aliases={2: 0},
        compiler_params=_params(("arbitrary",)),
        name="proj_residual_norm",
    )(a_fm, w_bf16, x, g_next.reshape(1, d))


def _qkv_body(h_ref, w_ref, c_ref, s_ref, o_ref, wb_ref, *, n_rope_tiles):
    @pl.when(pl.program_id(1) == 0)
    def _():
        wb_ref[...] = w_ref[...].astype(BF16)

    bm, bn = o_ref.shape
    is_rope = pl.program_id(0) < n_rope_tiles
    nchunk = 8 if bm % 128 == 0 else (4 if bm % 64 == 0 else 1)
    cr = bm // nchunk
    rs = [jnp.dot(h_ref[c * cr:(c + 1) * cr, :], wb_ref[...], preferred_element_type=F32) for c in range(nchunk)]
    lane = lax.broadcasted_iota(jnp.int32, (cr, LANES), 1)
    first_half = (lane % HEAD_DIM) < (HEAD_DIM // 2)
    for c in range(nchunk):
        rows = slice(c * cr, (c + 1) * cr)
        cos = c_ref[rows, :]
        sin = s_ref[rows, :]
        for ci in range(bn // LANES):
            blk = rs[c][:, ci * LANES:(ci + 1) * LANES]
            partner = jnp.where(first_half,
                                pltpu.roll(blk, LANES - HEAD_DIM // 2, 1),
                                pltpu.roll(blk, HEAD_DIM // 2, 1))
            o_ref[rows, ci * LANES:(ci + 1) * LANES] = jnp.where(is_rope, blk * cos + partner * sin, blk)


def qkv_rope(h, w_qkv, layer, cos_t, sin_t, n_rope_cols, bn):
    m, d = h.shape
    n = w_qkv.shape[2]
    assert n % bn == 0 and n_rope_cols % bn == 0 and bn % LANES == 0
    bm = _pick(m, (1664, 1024, 512, 256, 128, 64, 32, 16))
    return pl.pallas_call(
        functools.partial(_qkv_body, n_rope_tiles=n_rope_cols // bn),
        grid=(n // bn, m // bm),
        in_specs=[
            pl.BlockSpec((bm, d), lambda j, i: (i, 0)),
            pl.BlockSpec((None, d, bn), lambda j, i: (layer, 0, j)),
            pl.BlockSpec((bm, LANES), lambda j, i: (i, 0)),
            pl.BlockSpec((bm, LANES), lambda j, i: (i, 0)),
        ],
        out_specs=pl.BlockSpec((bm, bn), lambda j, i: (i, j)),
        out_shape=jax.ShapeDtypeStruct((m, n), F32),
        scratch_shapes=[pltpu.VMEM((d, bn), BF16)],
        compiler_params=_params(("arbitrary", "arbitrary")),
        name="qkv_rope",
    )(h, w_qkv, cos_t, sin_t)


def rope_tables(t_prompt, n_sample):
    half = HEAD_DIM // 2
    inv = ROPE_THETA ** (-jnp.arange(half, dtype=F32) / half)
    pos = jnp.concatenate([jnp.arange(t_prompt), jnp.full((n_sample,), PAST_LEN)]).astype(F32)
    ang = pos[:, None] * inv[None, :]
    cos, sin = jnp.cos(ang), jnp.sin(ang)
    reps = LANES // HEAD_DIM
    return (jnp.tile(jnp.concatenate([cos, cos], axis=1), (1, reps)),
            jnp.tile(jnp.concatenate([-sin, sin], axis=1), (1, reps)))


def _swa_prompt_body(sink_ref, q_ref, kp_ref, kc_ref, vp_ref, vc_ref, wo_ref, o_ref, wob_ref, *, n_kv, nsub):
    n = pl.program_id(0)
    blk = WINDOW
    hd = HEAD_DIM
    assert LANES == 2 * hd
    wob_ref[...] = wo_ref[...].astype(BF16)
    nq = GQA_GROUP * blk
    keys = lax.broadcasted_iota(jnp.int32, (2 * blk, nq), 0)
    qrow = lax.broadcasted_iota(jnp.int32, (2 * blk, nq), 1) % blk
    diff = qrow + blk - keys
    in_window = (diff >= 0) & (diff < WINDOW)
    masks = [in_window & ((keys >= blk) | (n > 0))] + [in_window] * (nsub - 1)
    head_of_col = lax.broadcasted_iota(jnp.int32, (1, nq), 1) // blk
    lane = lax.broadcasted_iota(jnp.int32, (1, LANES), 1)
    log2e = math.log2(math.e)
    qscale = log2e / math.sqrt(hd)
    def mine(kv):
        return (lane >= (kv % 2) * hd) & (lane < (kv % 2 + 1) * hd)

    def window_rows(prev_ref, cur_ref, sub, ps):
        prev = prev_ref[:, ps] if sub == 0 else cur_ref[(sub - 1) * blk:sub * blk, ps]
        return jnp.concatenate([prev, cur_ref[sub * blk:(sub + 1) * blk, ps]], axis=0)

    def scores(item):
        kv, sub = item
        ps = slice((kv // 2) * LANES, (kv // 2 + 1) * LANES)
        kpair = window_rows(kp_ref, kc_ref, sub, ps).astype(BF16)
        parts = []
        for g in range(GQA_GROUP):
            c0 = (kv * GQA_GROUP + g - (g % 2)) * hd
            src = q_ref[sub * blk:(sub + 1) * blk, c0:c0 + LANES] * qscale
            if g % 2 != kv % 2:
                src = pltpu.roll(src, hd, 1)
            parts.append(jnp.where(mine(kv), src, 0.0))
        qs = jnp.concatenate(parts, axis=0).astype(BF16)
        return lax.dot_general(kpair, qs, (((1,), (1,)), ((), ())), preferred_element_type=F32)

    def softmax(item, s):
        kv, sub = item
        s = jnp.where(masks[sub], s, -jnp.inf)
        sink = jnp.zeros((1, nq), F32)
        for g in range(GQA_GROUP):
            sink = jnp.where(head_of_col == g, sink_ref[kv * GQA_GROUP + g] * log2e, sink)
        mx = jnp.maximum(jnp.max(s, axis=0, keepdims=True), sink)
        return jnp.exp2(s - mx).astype(BF16), jnp.exp2(sink - mx)

    def weighted_values(item, pb, sink_term):
        kv, sub = item
        ps = slice((kv // 2) * LANES, (kv // 2 + 1) * LANES)
        vpair = window_rows(vp_ref, vc_ref, sub, ps)
        vaug = jnp.where(mine(kv), vpair, 1.0).astype(BF16)
        ot = lax.dot_general(vaug, pb, (((0,), (0,)), ((), ())), preferred_element_type=F32)
        half = kv % 2
        other = (1 - half) * hd
        denom = ot[other:other + 1, :] + sink_term
        on = (ot[half * hd:(half + 1) * hd, :] * (1.0 / denom)).astype(o_ref.dtype)
        for g in range(GQA_GROUP):
            r0 = (kv * GQA_GROUP + g) * hd
            o_ref[r0:r0 + hd, sub * blk:(sub + 1) * blk] = on[:, g * blk:(g + 1) * blk]

    @pl.when(n < pl.num_programs(0) - 1)
    def _():
        items = [(kv, sub) for sub in range(nsub) for kv in range(n_kv)]
        ahead = 2
        pending = {i: scores(items[i]) for i in range(min(ahead, len(items)))}
        for i, item in enumerate(items):
            pb, sink_term = softmax(item, pending.pop(i))
            if i + ahead < len(items):
                pending[i + ahead] = scores(items[i + ahead])
            weighted_values(item, pb, sink_term)

    @pl.when(n == pl.num_programs(0) - 1)
    def _():
        o_ref[...] = jnp.zeros_like(o_ref)


def swa_prompt(qkv, sinks, w_o, layer, t, d):
    n_kv = d // HEAD_DIM // GQA_GROUP
    kvw = n_kv * HEAD_DIM
    assert t % WINDOW == 0
    nsub = 2 if (t // WINDOW) % 2 == 0 else 1
    nblk = t // (nsub * WINDOW)
    assert d % kvw == 0 and n_kv % 2 == 0 and d % nblk == 0 and (d // nblk) % 16 == 0
    wrows = d // nblk
    kblk = d // kvw
    vblk = kblk + 1
    assert 0 < qkv.shape[0] - t <= nsub * WINDOW
    cur = lambda n: jnp.minimum(n, nblk - 1)
    prev = lambda n: jnp.maximum(cur(n) * nsub - 1, 0)
    return pl.pallas_call(
        functools.partial(_swa_prompt_body, n_kv=n_kv, nsub=nsub),
        grid=(nblk + 1,),
        in_specs=[
            pl.BlockSpec(memory_space=pltpu.SMEM),
            pl.BlockSpec((nsub * WINDOW, d), lambda n: (cur(n), 0)),
            pl.BlockSpec((WINDOW, kvw), lambda n: (prev(n), kblk)),
            pl.BlockSpec((nsub * WINDOW, kvw), lambda n: (cur(n), kblk)),
            pl.BlockSpec((WINDOW, kvw), lambda n: (prev(n), vblk)),
            pl.BlockSpec((nsub * WINDOW, kvw), lambda n: (cur(n), vblk)),
            pl.BlockSpec((None, wrows, d), lambda n: (layer, cur(n), 0)),
        ],
        out_specs=[pl.BlockSpec((d, nsub * WINDOW), lambda n: (0, n)),
                   pl.BlockSpec((wrows, d), lambda n: (cur(n), 0))],
        out_shape=[jax.ShapeDtypeStruct((d, qkv.shape[0]), BF16), jax.ShapeDtypeStruct((d, d), BF16)],
        compiler_params=_params(("arbitrary",)),
        name="swa_prompt",
    )(sinks, qkv, qkv, qkv, qkv, qkv, w_o)


def _swa_sample_body(sink_ref, qkvt_ref, k_ref, v_ref, o_any, ko_ref, vo_ref, o_ref, acc_ref, *, n_kv, bt):
    del o_any
    i = pl.program_id(0)
    nk = k_ref.shape[1]
    nb = qkvt_ref.shape[1]
    kvw = n_kv * HEAD_DIM
    n_heads = n_kv * GQA_GROUP
    nq = n_heads * HEAD_DIM
    sub = 8

    @pl.when(i == 0)
    def _():
        acc_ref[...] = jnp.zeros_like(acc_ref)

    key_lane = lax.broadcasted_iota(jnp.int32, (1, nk), 1)
    sublane = lax.broadcasted_iota(jnp.int32, (sub, nk), 0)
    head_row = lax.broadcasted_iota(jnp.int32, (n_heads, 1), 0)
    sink = jnp.zeros((n_heads, 1), F32)
    for hd in range(n_heads):
        sink = jnp.where(head_row == hd, sink_ref[hd], sink)
    qt = (qkvt_ref[:nq, :] * (1.0 / math.sqrt(HEAD_DIM))).astype(BF16)
    knt = qkvt_ref[nq:nq + kvw, :]
    vnt = qkvt_ref[nq + kvw:, :]
    gs = bt
    wide_row = lax.broadcasted_iota(jnp.int32, (nb, gs * nk), 0)
    wide_lane_group = lax.broadcasted_iota(jnp.int32, (nb, gs * nk), 1) // nk
    tall_lane_group = lax.broadcasted_iota(jnp.int32, (gs * nk, nb), 0) // nk
    tall_lane = lax.broadcasted_iota(jnp.int32, (gs * nk, nb), 1)
    qcols_of = [jnp.dot(qt, (wide_row == wide_lane_group + (i * bt + g0)).astype(BF16),
                        preferred_element_type=F32) for g0 in range(0, bt, gs)]
    kfulls, vfulls = [], []
    assert nb == nk
    for j in range(bt):
        to_last = nk - 1 - (i * bt + j)
        kn = pltpu.roll(knt, to_last, 1)
        vn = pltpu.roll(vnt, to_last, 1)
        rows = slice(j * kvw, (j + 1) * kvw)
        kfulls.append(jnp.where(key_lane == nk - 1, kn, pltpu.roll(k_ref[rows, :], nk - 1, 1)))
        vfulls.append(jnp.where(key_lane == nk - 1, vn, pltpu.roll(v_ref[rows, :], nk - 1, 1)))
        ko_ref[rows, :] = kfulls[j]
        vo_ref[rows, :] = vfulls[j]
    all_scores = []
    for j in range(bt):
        qcols = qcols_of[j // gs][:, (j % gs) * nk:(j % gs + 1) * nk]
        kfull = kfulls[j]
        tiles = []
        for tile in range(n_heads // sub):
            st = jnp.zeros((sub, nk), F32)
            for r in range(sub):
                hidx = tile * sub + r
                kv = hidx // GQA_GROUP
                prod = kfull[kv * HEAD_DIM:(kv + 1) * HEAD_DIM] * qcols[hidx * HEAD_DIM:(hidx + 1) * HEAD_DIM]
                red = jnp.sum(prod.reshape(HEAD_DIM // sub, sub, nk), axis=0)
                step = sub // 2
                while step >= 1:
                    red = red + pltpu.roll(red, step, 0)
                    step //= 2
                st = jnp.where(sublane == r, red, st)
            tiles.append(st)
        all_scores.append(jnp.concatenate(tiles, axis=0))
    probs = []
    for s in all_scores:
        mx = jnp.maximum(jnp.max(s, axis=1, keepdims=True), sink)
        p = jnp.exp(s - mx)
        probs.append(p / (jnp.sum(p, axis=1, keepdims=True) + jnp.exp(sink - mx)))
    pvs = []
    for j in range(bt):
        pn, vfull = probs[j], vfulls[j]
        pvs.append(jnp.concatenate(
            [vfull[(hidx // GQA_GROUP) * HEAD_DIM:(hidx // GQA_GROUP + 1) * HEAD_DIM] * pn[hidx:hidx + 1, :]
             for hidx in range(n_heads)], axis=0).astype(BF16))
        if len(pvs) == gs:
            first = i * bt + j + 1 - gs
            acc_ref[...] += jnp.dot(jnp.concatenate(pvs, axis=1),
                                    (tall_lane_group + first == tall_lane).astype(BF16),
                                    preferred_element_type=F32)
            pvs = []

    @pl.when(i == pl.num_programs(0) - 1)
    def _():
        o_ref[...] = acc_ref[...].astype(o_ref.dtype)


def swa_sample(qkvt, kt, vt, sinks, o_fm, n_kv):
    nrow, b = qkvt.shape
    kvw = n_kv * HEAD_DIM
    d = nrow - 2 * kvw
    nk = kt.shape[1]
    t = o_fm.shape[1] - b
    assert d == n_kv * GQA_GROUP * HEAD_DIM and (n_kv * GQA_GROUP) % 8 == 0 and t % b == 0
    bt = _pick(b, (8, 4, 2, 1))
    cache_spec = pl.BlockSpec((bt * kvw, nk), lambda i: (i, 0))
    return pl.pallas_call(
        functools.partial(_swa_sample_body, n_kv=n_kv, bt=bt),
        grid=(b // bt,),
        in_specs=[
            pl.BlockSpec(memory_space=pltpu.SMEM),
            pl.BlockSpec((nrow, b), lambda i: (0, 0)),
            cache_spec,
            cache_spec,
            pl.BlockSpec(memory_space=pl.ANY),
        ],
        out_specs=[cache_spec, cache_spec, pl.BlockSpec((d, b), lambda i: (0, t // b))],
        out_shape=[jax.ShapeDtypeStruct(kt.shape, F32), jax.ShapeDtypeStruct(vt.shape, F32),
                   jax.ShapeDtypeStruct(o_fm.shape, o_fm.dtype)],
        scratch_shapes=[pltpu.VMEM((d, b), F32)],
        input_output_aliases={4: 2},
        compiler_params=_params(("arbitrary",)),
        name="swa_sample",
    )(sinks, qkvt, kt, vt, o_fm)


def _pool_prompt_body(x_ref, g_ref, w_ref, sc_ref, gn_ref, hn_any, o_ref, hn_ref, hl_ref, wb_ref, *, gc):
    del hn_any
    i = pl.program_id(0)

    @pl.when(i == 0)
    def _():
        wb_ref[...] = w_ref[...].astype(BF16)
        hl_ref[...] = jnp.zeros_like(hl_ref)

    bm = x_ref.shape[0]
    x = x_ref[...]
    h = _rms(x, g_ref[...])
    hprev = hl_ref[...]
    hl_ref[...] = h[bm - POOL_MAXW:, :]
    pos1 = (lax.broadcasted_iota(jnp.int32, (bm, 1), 0) + i * bm + 1).astype(F32)
    for gi, w in enumerate(POOL_WINDOWS):
        cs = slice(gi * gc, (gi + 1) * gc)
        hg = h[:, cs]
        acc = jnp.concatenate([hprev[:, cs], hg], axis=0)
        step = 1
        while step < w:
            acc = acc + pltpu.roll(acc, step, 0)
            step *= 2
        inv_cnt = 1.0 / jnp.minimum(jnp.float32(w), pos1)
        pooled = (acc[POOL_MAXW:, :] * inv_cnt - hg).astype(BF16)
        y = jnp.dot(pooled, wb_ref[gi], preferred_element_type=F32)
        o_ref[:, cs] = x[:, cs] + y * sc_ref[:, cs]
    hn_ref[...] = _rms(o_ref[...], gn_ref[...]).astype(hn_ref.dtype)


def pool_prompt(x_all, hn_all, g, w_pool, layer, scale, g_next, t):
    d = x_all.shape[1]
    _, ng, gc, _ = w_pool.shape
    bm = _pick(t, (512, 256, 128, 64, 32, 16))
    return pl.pallas_call(
        functools.partial(_pool_prompt_body, gc=gc),
        grid=(t // bm,),
        in_specs=[
            pl.BlockSpec((bm, d), lambda i: (i, 0)),
            pl.BlockSpec((1, d), lambda i: (0, 0)),
            pl.BlockSpec((None, ng, gc, gc), lambda i: (layer, 0, 0, 0)),
            pl.BlockSpec((1, d), lambda i: (0, 0)),
            pl.BlockSpec((1, d), lambda i: (0, 0)),
            pl.BlockSpec(memory_space=pl.ANY),
        ],
        out_specs=[
            pl.BlockSpec((bm, d), lambda i: (i, 0)),
            pl.BlockSpec((bm, d), lambda i: (i, 0)),
            pl.BlockSpec((POOL_MAXW, d), lambda i: (0, 0)),
        ],
        out_shape=[jax.ShapeDtypeStruct(x_all.shape, F32), jax.ShapeDtypeStruct(hn_all.shape, hn_all.dtype),
                   jax.ShapeDtypeStruct((POOL_MAXW, d), F32)],
        scratch_shapes=[pltpu.VMEM((ng, gc, gc), BF16)],
        input_output_aliases={0: 0, 5: 1},
        compiler_params=_params(("arbitrary",)),
        name="pool_prompt",
    )(x_all, g.reshape(1, d), w_pool, scale.reshape(1, d), g_next.reshape(1, d), hn_all)


def _pool_sample_body(x_ref, st_ref, g_ref, w_ref, sc_ref, gn_ref, hn_any, o_ref, hn_ref, ns_ref, *, gc):
    del hn_any
    x = x_ref[...]
    h = _rms(x, g_ref[...])
    ns_ref[:POOL_BUF - 1] = st_ref[1:]
    ns_ref[POOL_BUF - 1] = h
    for gi, w in enumerate(POOL_WINDOWS):
        cs = slice(gi * gc, (gi + 1) * gc)
        hg = h[:, cs]
        tot = hg + jnp.sum(st_ref[POOL_BUF - (w - 1):, :, cs], axis=0)
        cnt = float(min(w, PAST_LEN + 1))
        pooled = (tot / cnt - hg).astype(BF16)
        y = jnp.dot(pooled, w_ref[gi].astype(BF16), preferred_element_type=F32)
        o_ref[:, cs] = x[:, cs] + y * sc_ref[:, cs]
    hn_ref[...] = _rms(o_ref[...], gn_ref[...]).astype(hn_ref.dtype)


def pool_sample(x_all, hn_all, state, layer, g, w_pool, scale, g_next, t):
    d = x_all.shape[1]
    b = state.shape[2]
    _, ng, gc, _ = w_pool.shape
    bt = _pick(b, (32, 16, 8))
    assert t % bt == 0
    off = t // bt
    return pl.pallas_call(
        functools.partial(_pool_sample_body, gc=gc),
        grid=(b // bt,),
        in_specs=[
            pl.BlockSpec((bt, d), lambda i: (off + i, 0)),
            pl.BlockSpec((None, POOL_BUF, bt, d), lambda i: (layer, 0, i, 0)),
            pl.BlockSpec((1, d), lambda i: (0, 0)),
            pl.BlockSpec((None, ng, gc, gc), lambda i: (layer, 0, 0, 0)),
            pl.BlockSpec((1, d), lambda i: (0, 0)),
            pl.BlockSpec((1, d), lambda i: (0, 0)),
            pl.BlockSpec(memory_space=pl.ANY),
        ],
        out_specs=[
            pl.BlockSpec((bt, d), lambda i: (off + i, 0)),
            pl.BlockSpec((bt, d), lambda i: (off + i, 0)),
            pl.BlockSpec((POOL_BUF, bt, d), lambda i: (0, i, 0)),
        ],
        out_shape=[jax.ShapeDtypeStruct(x_all.shape, F32), jax.ShapeDtypeStruct(hn_all.shape, hn_all.dtype),
                   jax.ShapeDtypeStruct((POOL_BUF, b, d), F32)],
        input_output_aliases={0: 0, 6: 1},
        compiler_params=_params(("arbitrary",)),
        name="pool_sample",
    )(x_all, state, g.reshape(1, d), w_pool, scale.reshape(1, d), g_next.reshape(1, d), hn_all)


def _mem_kv_body(m_ref, g_ref, w_ref, o_ref):
    h = _rms(m_ref[...], g_ref[...]).astype(BF16)
    o_ref[...] = jnp.dot(h, w_ref[...].astype(BF16), preferred_element_type=F32)


def mem_kv(mem, g, w_kv, layer):
    m, d = mem.shape
    n = w_kv.shape[2]
    bn = _pick(n, (512, 256, 128))
    return pl.pallas_call(
        _mem_kv_body,
        grid=(n // bn,),
        in_specs=[
            pl.BlockSpec((m, d), lambda j: (0, 0)),
            pl.BlockSpec((1, d), lambda j: (0, 0)),
            pl.BlockSpec((None, d, bn), lambda j: (layer, 0, j)),
        ],
        out_specs=pl.BlockSpec((m, bn), lambda j: (0, j)),
        out_shape=jax.ShapeDtypeStruct((m, n), F32),
        compiler_params=_params(("arbitrary",)),
        name="mem_kv",
    )(mem, g.reshape(1, d), w_kv)


def _xattn_prompt_body(h_ref, wq_ref, kv_ref, wo_ref, x_ref, gn_ref, o_ref, hn_ref, wqb_ref, wob_ref, kvb_ref):
    @pl.when(pl.program_id(0) == 0)
    def _():
        wqb_ref[...] = wq_ref[...].astype(BF16)
        wob_ref[...] = wo_ref[...].astype(BF16)
        kvb_ref[...] = kv_ref[...].astype(BF16)

    xw = X_HEADS * X_HEAD_DIM
    bm = h_ref.shape[0]
    nchunk = 2 if bm % 32 == 0 else 1
    rows = [slice(c * (bm // nchunk), (c + 1) * (bm // nchunk)) for c in range(nchunk)]
    qs = [jnp.dot(h_ref[r, :], wqb_ref[...], preferred_element_type=F32) / math.sqrt(X_HEAD_DIM) for r in rows]
    scores = [[lax.dot_general(kvb_ref[:, hd * X_HEAD_DIM:(hd + 1) * X_HEAD_DIM],
                               q[:, hd * X_HEAD_DIM:(hd + 1) * X_HEAD_DIM].astype(BF16), (((1,), (1,)), ((), ())),
                               preferred_element_type=F32) for hd in range(X_HEADS)] for q in qs]
    attn = []
    for c in range(nchunk):
        outs = []
        for hd in range(X_HEADS):
            s = scores[c][hd]
            p = jnp.exp(s - jnp.max(s, axis=0, keepdims=True))
            l = jnp.sum(p, axis=0, keepdims=True)
            vs = slice(xw + hd * X_HEAD_DIM, xw + (hd + 1) * X_HEAD_DIM)
            outs.append(lax.dot_general(kvb_ref[:, vs], p.astype(BF16), (((0,), (0,)), ((), ())),
                                        preferred_element_type=F32) / l)
        attn.append(jnp.concatenate(outs, axis=0).astype(BF16))
    for c, r in enumerate(rows):
        xn = x_ref[r, :] + lax.dot_general(attn[c], wob_ref[...], (((0,), (0,)), ((), ())),
                                           preferred_element_type=F32)
        o_ref[r, :] = xn
        hn_ref[r, :] = _rms(xn, gn_ref[...]).astype(hn_ref.dtype)


def xattn_prompt(hq_all, w_q, mkv, w_o, layer, x_all, g_next, t):
    d = x_all.shape[1]
    xw = w_q.shape[2]
    nm = mkv.shape[0]
    bm = _pick(t, (512, 256, 128, 64, 32, 16))
    return pl.pallas_call(
        _xattn_prompt_body,
        grid=(t // bm,),
        in_specs=[
            pl.BlockSpec((bm, d), lambda i: (i, 0)),
            pl.BlockSpec((None, d, xw), lambda i: (layer, 0, 0)),
            pl.BlockSpec((nm, 2 * xw), lambda i: (0, 0)),
            pl.BlockSpec((None, xw, d), lambda i: (layer, 0, 0)),
            pl.BlockSpec((bm, d), lambda i: (i, 0)),
            pl.BlockSpec((1, d), lambda i: (0, 0)),
        ],
        out_specs=[pl.BlockSpec((bm, d), lambda i: (i, 0)), pl.BlockSpec((bm, d), lambda i: (i, 0))],
        out_shape=[jax.ShapeDtypeStruct(x_all.shape, F32), jax.ShapeDtypeStruct(x_all.shape, BF16)],
        scratch_shapes=[pltpu.VMEM((d, xw), BF16), pltpu.VMEM((xw, d), BF16), pltpu.VMEM((nm, 2 * xw), BF16)],
        input_output_aliases={4: 0, 0: 1},
        compiler_params=_params(("arbitrary",)),
        name="xattn_prompt",
    )(hq_all, w_q, mkv, w_o, x_all, g_next.reshape(1, d))


def _xq_sample_body(h_ref, wq_ref, o_ref):
    o_ref[...] = jnp.dot(h_ref[...], wq_ref[...].astype(BF16), preferred_element_type=F32) / math.sqrt(X_HEAD_DIM)


def xq_sample(hq_all, w_q, layer, t, b):
    d = hq_all.shape[1]
    xw = w_q.shape[2]
    assert t % b == 0
    return pl.pallas_call(
        _xq_sample_body,
        grid=(1,),
        in_specs=[
            pl.BlockSpec((b, d), lambda i: (t // b, 0)),
            pl.BlockSpec((None, d, xw), lambda i: (layer, 0, 0)),
        ],
        out_specs=pl.BlockSpec((b, xw), lambda i: (0, 0)),
        out_shape=jax.ShapeDtypeStruct((b, xw), F32),
        compiler_params=_params(("arbitrary",)),
        name="xq_sample",
    )(hq_all, w_q)


def _xattn_core_body(q_ref, k_ref, v_ref, ones_ref, o_ref):
    bt, sub, hd = q_ref.shape
    nv = k_ref.shape[0] // (bt * sub)
    k = k_ref[...].reshape(bt, nv, sub, hd)
    prod = (k * q_ref[...][:, None]).reshape(bt * nv * sub, hd).astype(BF16)
    s = jnp.dot(prod, ones_ref[...], preferred_element_type=F32).reshape(bt, nv, sub, hd)

    def fold(x, op):
        step = X_HEADS
        while step < sub:
            x = op(x, pltpu.roll(x, step, 2))
            step *= 2
        return x

    mx = fold(jnp.max(s, axis=1, keepdims=True), jnp.maximum)
    p = jnp.exp(s - mx)
    l = fold(jnp.sum(p, axis=1, keepdims=True), jnp.add)
    acc = fold(jnp.sum(p * v_ref[...].reshape(bt, nv, sub, hd), axis=1, keepdims=True), jnp.add)
    o_ref[...] = (acc / l).reshape(bt, sub, hd)


def xattn_core_sample(q_s, mk_flat, mv_flat, layer, nm):
    b = q_s.shape[0]
    hd = X_HEAD_DIM
    sub = 8
    assert sub % X_HEADS == 0 and nm % (sub // X_HEADS) == 0
    bt = _pick(b, (8, 4, 2, 1))
    nblk = b // bt
    rows = bt * nm * X_HEADS
    q8 = jnp.tile(q_s.reshape(b, X_HEADS, hd), (1, sub // X_HEADS, 1))
    o8 = pl.pallas_call(
        _xattn_core_body,
        grid=(nblk,),
        in_specs=[
            pl.BlockSpec((bt, sub, hd), lambda i: (i, 0, 0)),
            pl.BlockSpec((rows, hd), lambda i: (layer * nblk + i, 0)),
            pl.BlockSpec((rows, hd), lambda i: (layer * nblk + i, 0)),
            pl.BlockSpec((hd, hd), lambda i: (0, 0)),
        ],
        out_specs=pl.BlockSpec((bt, sub, hd), lambda i: (i, 0, 0)),
        out_shape=jax.ShapeDtypeStruct((b, sub, hd), F32),
        compiler_params=_params(("arbitrary",)),
        name="xattn_core_sample",
    )(q8, mk_flat, mv_flat, jnp.ones((hd, hd), BF16))
    return o8[:, :X_HEADS].reshape(b, X_HEADS * hd)


def _xout_sample_body(a_ref, wo_ref, x_ref, gn_ref, hn_any, o_ref, hn_ref):
    del hn_any
    xn = x_ref[...] + jnp.dot(a_ref[...].astype(BF16), wo_ref[...].astype(BF16), preferred_element_type=F32)
    o_ref[...] = xn
    hn_ref[...] = _rms(xn, gn_ref[...]).astype(hn_ref.dtype)


def xout_sample(o_s, w_o, layer, x_all, hn_all, g_next, t):
    d = x_all.shape[1]
    b, xw = o_s.shape
    assert t % b == 0
    off = t // b
    return pl.pallas_call(
        _xout_sample_body,
        grid=(1,),
        in_specs=[
            pl.BlockSpec((b, xw), lambda i: (0, 0)),
            pl.BlockSpec((None, xw, d), lambda i: (layer, 0, 0)),
            pl.BlockSpec((b, d), lambda i: (off, 0)),
            pl.BlockSpec((1, d), lambda i: (0, 0)),
            pl.BlockSpec(memory_space=pl.ANY),
        ],
        out_specs=[pl.BlockSpec((b, d), lambda i: (off, 0)), pl.BlockSpec((b, d), lambda i: (off, 0))],
        out_shape=[jax.ShapeDtypeStruct(x_all.shape, F32), jax.ShapeDtypeStruct(hn_all.shape, hn_all.dtype)],
        input_output_aliases={2: 0, 4: 1},
        compiler_params=_params(("arbitrary",)),
        name="xout_sample",
    )(o_s, w_o, x_all, g_next.reshape(1, d), hn_all)


def kernel(x_prompt, x_sample, state_pool, cache_swa_k, cache_swa_v, cache_mem_k, cache_mem_v, mem_prompt,
           g_ffn1, w_ffn1_gu, w_ffn1_dn, g_mix, w_pool, pool_scale, w_qkv, w_o, sinks,
           g_xq, g_mem, w_xq, w_xkv, w_xo, g_ffn2, w_ffn2_gu, w_ffn2_dn, g_final):
    bp, t, d = x_prompt.shape
    b, s_len, _ = x_sample.shape
    assert bp == 1 and s_len == 1
    depth = g_ffn1.shape[0]
    xw = w_xq.shape[2]
    nm = mem_prompt.shape[1]
    n_kv = d // HEAD_DIM // GQA_GROUP
    kvw = n_kv * HEAD_DIM

    x, h = stack_and_norm(x_prompt[0], x_sample[:, 0], g_ffn1[0])
    cos_t, sin_t = rope_tables(t, b)
    mk_flat = cache_mem_k.reshape(-1, X_HEAD_DIM)
    mv_flat = cache_mem_v.reshape(-1, X_HEAD_DIM)
    state_sm = jnp.swapaxes(state_pool, 1, 2)

    pool_p, pool_s = [], []
    swa_kp, swa_vp, swa_ks, swa_vs = [], [], [], []
    mem_kp, mem_vp = [], []
    y_p = y_s = None
    for layer in range(depth):
        i = layer // 2
        act, wdb = gate_up(h, w_ffn1_gu, w_ffn1_dn, layer)
        x, h = down_norm(act, wdb, x, g_mix[layer], BF16)
        if layer % 2 == 0:
            x, hq, h_last = pool_prompt(x, h, g_mix[layer], w_pool, i, pool_scale[i], g_xq[layer], t)
            x, hq, new_state = pool_sample(x, hq, state_sm, i, g_mix[layer], w_pool, pool_scale[i], g_xq[layer], t)
            pool_p.append(h_last[None, POOL_MAXW - POOL_BUF:])
            pool_s.append(jnp.swapaxes(new_state, 0, 1))
        else:
            qkv = qkv_rope(h, w_qkv, i, cos_t, sin_t, d + kvw, kvw)
            buf = cache_swa_k.shape[2]
            keep = min(WINDOW, t)
            swa_kp.append(qkv[t - keep:t, d:d + kvw].reshape(1, keep, n_kv, HEAD_DIM))
            swa_vp.append(qkv[t - keep:t, d + kvw:].reshape(1, keep, n_kv, HEAD_DIM))
            assert buf == WINDOW
            to_fm = lambda c: jnp.transpose(c, (0, 2, 3, 1)).reshape(b * kvw, buf)
            from_fm = lambda c: jnp.transpose(c.reshape(b, n_kv, HEAD_DIM, buf), (0, 3, 1, 2))
            o_fm, wo_bf16 = swa_prompt(qkv, sinks[i], w_o, i, t, d)
            ks_fm, vs_fm, o_fm = swa_sample(qkv[t:].T, to_fm(cache_swa_k[i]), to_fm(cache_swa_v[i]), sinks[i],
                                            o_fm, n_kv)
            swa_ks.append(from_fm(ks_fm))
            swa_vs.append(from_fm(vs_fm))
            x, hq = proj_residual_norm(o_fm, wo_bf16, x, g_xq[layer])
        mkv = mem_kv(mem_prompt[0], g_mem[layer], w_xkv, layer)
        mem_kp.append(mkv[:, :xw].reshape(1, nm, X_HEADS, X_HEAD_DIM))
        mem_vp.append(mkv[:, xw:].reshape(1, nm, X_HEADS, X_HEAD_DIM))
        q_s = xq_sample(hq, w_xq, layer, t, b)
        x, h = xattn_prompt(hq, w_xq, mkv, w_xo, layer, x, g_ffn2[layer], t)
        o_s = xattn_core_sample(q_s, mk_flat, mv_flat, layer, nm)
        x, h = xout_sample(o_s, w_xo, layer, x, h, g_ffn2[layer], t)
        act, wdb = gate_up(h, w_ffn2_gu, w_ffn2_dn, layer)
        if layer + 1 < depth:
            x, h = down_norm(act, wdb, x, g_ffn1[layer + 1], BF16)
        else:
            (y_p,) = down_norm(act, wdb, x, g_final, F32, emit_x=False, row0=0, nrows=t)
            (y_s,) = down_norm(act, wdb, x, g_final, F32, emit_x=False, row0=t, nrows=b)
    return (y_p[None], y_s[:, None], jnp.stack(pool_p), jnp.stack(pool_s), jnp.stack(swa_kp), jnp.stack(swa_vp),
            jnp.stack(swa_ks), jnp.stack(swa_vs), jnp.stack(mem_kp), jnp.stack(mem_vp))
```

```python
import functools
import math

import jax
import jax.numpy as jnp
from jax import lax
from jax.experimental import pallas as pl
from jax.experimental.pallas import tpu as pltpu

F32 = jnp.float32
BF16 = jnp.bfloat16

RMS_EPS = 1e-6
PAST_LEN = 8192
POOL_WINDOWS = (2, 4, 8, 16)
POOL_MAXW = max(POOL_WINDOWS)
POOL_BUF = POOL_MAXW - 1
HEAD_DIM = 64
GQA_GROUP = 4
WINDOW = 128
ROPE_THETA = 10000.0
X_HEADS = 4
X_HEAD_DIM = 128
LANES = 128
V7X_VMEM_LIMIT = 60 * 1024 * 1024


def _params(sem):
    return pltpu.CompilerParams(dimension_semantics=sem, vmem_limit_bytes=V7X_VMEM_LIMIT)


def _pick(n, candidates):
    for c in candidates:
        if n % c == 0:
            return c
    raise ValueError(f"no block size in {candidates} divides {n}")


def _rms(x, g):
    return x * lax.rsqrt(jnp.mean(x * x, axis=-1, keepdims=True) + RMS_EPS) * g


def _stack_norm_body(xp_ref, xs_ref, g_ref, h_ref):
    nt = pl.num_programs(0) - 1
    b = xs_ref.shape[0]

    @pl.when(pl.program_id(0) < nt)
    def _():
        h_ref[...] = _rms(xp_ref[...], g_ref[...]).astype(h_ref.dtype)

    @pl.when(pl.program_id(0) == nt)
    def _():
        h_ref[:b, :] = _rms(xs_ref[...], g_ref[...]).astype(h_ref.dtype)


def stacked_norm(x_p, x_s, g):
    t, d = x_p.shape
    b = x_s.shape[0]
    bm = _pick(t, (512, 256, 128, 64, 32, 16))
    nt = t // bm
    assert b <= bm and b % 16 == 0
    return pl.pallas_call(
        _stack_norm_body,
        grid=(nt + 1,),
        in_specs=[pl.BlockSpec((bm, d), lambda i: (jnp.minimum(i, nt - 1), 0)),
                  pl.BlockSpec((b, d), lambda i: (0, 0)),
                  pl.BlockSpec((1, d), lambda i: (0, 0))],
        out_specs=pl.BlockSpec((bm, d), lambda i: (i, 0)),
        out_shape=jax.ShapeDtypeStruct((t + b, d), BF16),
        compiler_params=_params(("arbitrary",)),
        name="stack_norm",
    )(x_p, x_s, g.reshape(1, d))


def _gate_up_body(h_ref, wg_ref, wu_ref, wd_ref, o_ref, wdb_ref, wb_ref, *, bf):
    @pl.when(pl.program_id(1) == 0)
    def _():
        wb_ref[:, :bf] = wg_ref[...].astype(BF16)
        wb_ref[:, bf:] = wu_ref[...].astype(BF16)
        wdb_ref[...] = wd_ref[...].astype(BF16)

    bm = h_ref.shape[0]
    nchunk = 8 if bm % 128 == 0 else (4 if bm % 64 == 0 else 1)
    cr = bm // nchunk
    rs = [jnp.dot(h_ref[c * cr:(c + 1) * cr, :], wb_ref[...], preferred_element_type=F32) for c in range(nchunk)]
    for c in range(nchunk):
        a = rs[c][:, :bf]
        b = rs[c][:, bf:]
        o_ref[c * cr:(c + 1) * cr, :] = (a / (1.0 + jnp.exp(-a)) * b * 0.5).astype(o_ref.dtype)


def gate_up(h, w_gu, w_dn, layer):
    m, d = h.shape
    f = w_gu.shape[2] // 2
    bf = _pick(f, (512, 256, 128))
    bm = _pick(m, (1664, 1024, 512, 256, 128, 64, 32, 16))
    nf = f // bf
    return pl.pallas_call(
        functools.partial(_gate_up_body, bf=bf),
        grid=(nf, m // bm),
        in_specs=[
            pl.BlockSpec((bm, d), lambda j, i: (i, 0)),
            pl.BlockSpec((None, d, bf), lambda j, i: (layer, 0, j)),
            pl.BlockSpec((None, d, bf), lambda j, i: (layer, 0, j + nf)),
            pl.BlockSpec((None, bf, d), lambda j, i: (layer, j, 0)),
        ],
        out_specs=[
            pl.BlockSpec((bm, bf), lambda j, i: (i, j)),
            pl.BlockSpec((bf, d), lambda j, i: (j, 0)),
        ],
        out_shape=[jax.ShapeDtypeStruct((m, f), BF16), jax.ShapeDtypeStruct((f, d), BF16)],
        scratch_shapes=[pltpu.VMEM((d, 2 * bf), BF16)],
        compiler_params=_params(("arbitrary", "arbitrary")),
        name="gate_up",
    )(h, w_gu, w_gu, w_dn)


def _down_body(a_ref, w_ref, x_ref, g_ref, *out_refs, emit_x):
    xn = x_ref[...] + jnp.dot(a_ref[...], w_ref[...], preferred_element_type=F32)
    if emit_x:
        out_refs[0][...] = xn
    h_ref = out_refs[-1]
    h_ref[...] = _rms(xn, g_ref[...]).astype(h_ref.dtype)


def down_norm(a, w_bf16, x, g_next, h_dtype, emit_x=True, row0=0, nrows=None):
    m, k = a.shape
    d = w_bf16.shape[1]
    nrows = m if nrows is None else nrows
    bm = _pick(math.gcd(nrows, row0) if row0 else nrows, (416, 320, 256, 128, 64, 32, 16))
    off = row0 // bm
    row = lambda i: (i, 0)
    src = lambda i: (i + off, 0)
    out_specs = [pl.BlockSpec((bm, d), row)]
    out_shape = [jax.ShapeDtypeStruct((nrows, d), h_dtype)]
    if emit_x:
        out_specs.insert(0, pl.BlockSpec((bm, d), src))
        out_shape.insert(0, jax.ShapeDtypeStruct((m, d), F32))
    return pl.pallas_call(
        functools.partial(_down_body, emit_x=emit_x),
        grid=(nrows // bm,),
        in_specs=[
            pl.BlockSpec((bm, k), src),
            pl.BlockSpec((k, d), lambda i: (0, 0), pipeline_mode=pl.Buffered(1)),
            pl.BlockSpec((bm, d), src),
            pl.BlockSpec((1, d), lambda i: (0, 0)),
        ],
        out_specs=out_specs,
        out_shape=out_shape,
        input_output_aliases={2: 0} if emit_x else {},
        compiler_params=_params(("arbitrary",)),
        name="down_norm",
    )(a, w_bf16, x, g_next.reshape(1, d))


def _down_first_body(a_ref, w_ref, xp_ref, xs_ref, g_ref, xo_ref, h_ref, *, split):
    bm = a_ref.shape[0]
    last = pl.num_programs(0) - 1

    def finish(rows, x):
        xn = x + jnp.dot(a_ref[rows, :], w_ref[...], preferred_element_type=F32)
        xo_ref[rows, :] = xn
        h_ref[rows, :] = _rms(xn, g_ref[...]).astype(h_ref.dtype)

    @pl.when(pl.program_id(0) < last)
    def _():
        finish(slice(0, bm), xp_ref[...])

    @pl.when(pl.program_id(0) == last)
    def _():
        finish(slice(0, split), xp_ref[:split, :])
        finish(slice(split, bm), xs_ref[...])


def down_norm_stacking(a, w_bf16, x_p, x_s, g_next, h_dtype):
    m, k = a.shape
    d = w_bf16.shape[1]
    t, b = x_p.shape[0], x_s.shape[0]
    bm = _pick(m, (416, 320, 256, 128, 64, 32, 16))
    nt = m // bm
    split = t - (nt - 1) * bm
    assert m == t + b and bm - split == b and split > 0 and split % 16 == 0
    row = lambda i: (i, 0)
    return pl.pallas_call(
        functools.partial(_down_first_body, split=split),
        grid=(nt,),
        in_specs=[
            pl.BlockSpec((bm, k), row),
            pl.BlockSpec((k, d), lambda i: (0, 0), pipeline_mode=pl.Buffered(1)),
            pl.BlockSpec((bm, d), row),
            pl.BlockSpec((b, d), lambda i: (0, 0)),
            pl.BlockSpec((1, d), lambda i: (0, 0)),
        ],
        out_specs=[pl.BlockSpec((bm, d), row), pl.BlockSpec((bm, d), row)],
        out_shape=[jax.ShapeDtypeStruct((m, d), F32), jax.ShapeDtypeStruct((m, d), h_dtype)],
        compiler_params=_params(("arbitrary",)),
        name="down_norm_stacking",
    )(a, w_bf16, x_p, x_s, g_next.reshape(1, d))


def _proj_res_body(a_ref, w_ref, x_ref, g_ref, xo_ref, h_ref):
    xn = x_ref[...] + lax.dot_general(a_ref[...], w_ref[...], (((0,), (0,)), ((), ())),
                                      preferred_element_type=F32)
    xo_ref[...] = xn
    h_ref[...] = _rms(xn, g_ref[...]).astype(h_ref.dtype)


def proj_residual_norm(a_fm, w_bf16, x, g_next):
    k, m = a_fm.shape
    d = w_bf16.shape[1]
    bm = _pick(m, (640, 512, 256, 128))
    row = lambda i: (i, 0)
    return pl.pallas_call(
        _proj_res_body,
        grid=(m // bm,),
        in_specs=[
            pl.BlockSpec((k, bm), lambda i: (0, i)),
            pl.BlockSpec((k, d), lambda i: (0, 0), pipeline_mode=pl.Buffered(1)),
            pl.BlockSpec((bm, d), row),
            pl.BlockSpec((1, d), lambda i: (0, 0)),
        ],
        out_specs=[pl.BlockSpec((bm, d), row), pl.BlockSpec((bm, d), row)],
        out_shape=[jax.ShapeDtypeStruct((m, d), F32), jax.ShapeDtypeStruct((m, d), BF16)],
        input_output_aliases={2: 0},
        compiler_params=_params(("arbitrary",)),
        name="proj_residual_norm",
    )(a_fm, w_bf16, x, g_next.reshape(1, d))


def _qkv_body(h_ref, w_ref, c_ref, s_ref, o_ref, wb_ref, *, n_rope_tiles):
    @pl.when(pl.program_id(1) == 0)
    def _():
        wb_ref[...] = w_ref[...].astype(BF16)

    bm, bn = o_ref.shape
    is_rope = pl.program_id(0) < n_rope_tiles
    nchunk = 8 if bm % 128 == 0 else (4 if bm % 64 == 0 else 1)
    cr = bm // nchunk
    rs = [jnp.dot(h_ref[c * cr:(c + 1) * cr, :], wb_ref[...], preferred_element_type=F32) for c in range(nchunk)]
    lane = lax.broadcasted_iota(jnp.int32, (cr, LANES), 1)
    first_half = (lane % HEAD_DIM) < (HEAD_DIM // 2)
    for c in range(nchunk):
        rows = slice(c * cr, (c + 1) * cr)
        cos = c_ref[rows, :]
        sin = s_ref[rows, :]
        for ci in range(bn // LANES):
            blk = rs[c][:, ci * LANES:(ci + 1) * LANES]
            partner = jnp.where(first_half,
                                pltpu.roll(blk, LANES - HEAD_DIM // 2, 1),
                                pltpu.roll(blk, HEAD_DIM // 2, 1))
            o_ref[rows, ci * LANES:(ci + 1) * LANES] = jnp.where(is_rope, blk * cos + partner * sin, blk)


def qkv_rope(h, w_qkv, layer, cos_t, sin_t, n_rope_cols, bn):
    m, d = h.shape
    n = w_qkv.shape[2]
    assert n % bn == 0 and n_rope_cols % bn == 0 and bn % LANES == 0
    bm = _pick(m, (1664, 1024, 512, 256, 128, 64, 32, 16))
    return pl.pallas_call(
        functools.partial(_qkv_body, n_rope_tiles=n_rope_cols // bn),
        grid=(n // bn, m // bm),
        in_specs=[
            pl.BlockSpec((bm, d), lambda j, i: (i, 0)),
            pl.BlockSpec((None, d, bn), lambda j, i: (layer, 0, j)),
            pl.BlockSpec((bm, LANES), lambda j, i: (i, 0)),
            pl.BlockSpec((bm, LANES), lambda j, i: (i, 0)),
        ],
        out_specs=pl.BlockSpec((bm, bn), lambda j, i: (i, j)),
        out_shape=jax.ShapeDtypeStruct((m, n), F32),
        scratch_shapes=[pltpu.VMEM((d, bn), BF16)],
        compiler_params=_params(("arbitrary", "arbitrary")),
        name="qkv_rope",
    )(h, w_qkv, cos_t, sin_t)


def rope_tables(t_prompt, n_sample):
    half = HEAD_DIM // 2
    inv = ROPE_THETA ** (-jnp.arange(half, dtype=F32) / half)
    pos = jnp.concatenate([jnp.arange(t_prompt), jnp.full((n_sample,), PAST_LEN)]).astype(F32)
    ang = pos[:, None] * inv[None, :]
    cos, sin = jnp.cos(ang), jnp.sin(ang)
    reps = LANES // HEAD_DIM
    return (jnp.tile(jnp.concatenate([cos, cos], axis=1), (1, reps)),
            jnp.tile(jnp.concatenate([-sin, sin], axis=1), (1, reps)))


def _swa_prompt_body(sink_ref, q_ref, kp_ref, kc_ref, vp_ref, vc_ref, wo_ref, o_ref, wob_ref, *, n_kv, nsub):
    n = pl.program_id(0)
    blk = WINDOW
    hd = HEAD_DIM
    assert LANES == 2 * hd
    wob_ref[...] = wo_ref[...].astype(BF16)
    nq = GQA_GROUP * blk
    keys = lax.broadcasted_iota(jnp.int32, (2 * blk, nq), 0)
    qrow = lax.broadcasted_iota(jnp.int32, (2 * blk, nq), 1) % blk
    diff = qrow + blk - keys
    in_window = (diff >= 0) & (diff < WINDOW)
    masks = [in_window & ((keys >= blk) | (n > 0))] + [in_window] * (nsub - 1)
    head_of_col = lax.broadcasted_iota(jnp.int32, (1, nq), 1) // blk
    lane = lax.broadcasted_iota(jnp.int32, (1, LANES), 1)
    log2e = math.log2(math.e)
    qscale = log2e / math.sqrt(hd)
    def mine(kv):
        return (lane >= (kv % 2) * hd) & (lane < (kv % 2 + 1) * hd)

    def window_rows(prev_ref, cur_ref, sub, ps):
        prev = prev_ref[:, ps] if sub == 0 else cur_ref[(sub - 1) * blk:sub * blk, ps]
        return jnp.concatenate([prev, cur_ref[sub * blk:(sub + 1) * blk, ps]], axis=0)

    def scores(item):
        kv, sub = item
        ps = slice((kv // 2) * LANES, (kv // 2 + 1) * LANES)
        kpair = window_rows(kp_ref, kc_ref, sub, ps).astype(BF16)
        parts = []
        for g in range(GQA_GROUP):
            c0 = (kv * GQA_GROUP + g - (g % 2)) * hd
            src = q_ref[sub * blk:(sub + 1) * blk, c0:c0 + LANES] * qscale
            if g % 2 != kv % 2:
                src = pltpu.roll(src, hd, 1)
            parts.append(jnp.where(mine(kv), src, 0.0))
        qs = jnp.concatenate(parts, axis=0).astype(BF16)
        return lax.dot_general(kpair, qs, (((1,), (1,)), ((), ())), preferred_element_type=F32)

    def softmax(item, s):
        kv, sub = item
        s = jnp.where(masks[sub], s, -jnp.inf)
        sink = jnp.zeros((1, nq), F32)
        for g in range(GQA_GROUP):
            sink = jnp.where(head_of_col == g, sink_ref[kv * GQA_GROUP + g] * log2e, sink)
        mx = jnp.maximum(jnp.max(s, axis=0, keepdims=True), sink)
        return jnp.exp2(s - mx).astype(BF16), jnp.exp2(sink - mx)

    def weighted_values(item, pb, sink_term):
        kv, sub = item
        ps = slice((kv // 2) * LANES, (kv // 2 + 1) * LANES)
        vpair = window_rows(vp_ref, vc_ref, sub, ps)
        vaug = jnp.where(mine(kv), vpair, 1.0).astype(BF16)
        ot = lax.dot_general(vaug, pb, (((0,), (0,)), ((), ())), preferred_element_type=F32)
        half = kv % 2
        other = (1 - half) * hd
        denom = ot[other:other + 1, :] + sink_term
        on = (ot[half * hd:(half + 1) * hd, :] * (1.0 / denom)).astype(o_ref.dtype)
        for g in range(GQA_GROUP):
            r0 = (kv * GQA_GROUP + g) * hd
            o_ref[r0:r0 + hd, sub * blk:(sub + 1) * blk] = on[:, g * blk:(g + 1) * blk]

    @pl.when(n < pl.num_programs(0) - 1)
    def _():
        items = [(kv, sub) for sub in range(nsub) for kv in range(n_kv)]
        ahead = 2
        pending = {i: scores(items[i]) for i in range(min(ahead, len(items)))}
        for i, item in enumerate(items):
            pb, sink_term = softmax(item, pending.pop(i))
            if i + ahead < len(items):
                pending[i + ahead] = scores(items[i + ahead])
            weighted_values(item, pb, sink_term)

    @pl.when(n == pl.num_programs(0) - 1)
    def _():
        o_ref[...] = jnp.zeros_like(o_ref)


def swa_prompt(qkv, sinks, w_o, layer, t, d):
    n_kv = d // HEAD_DIM // GQA_GROUP
    kvw = n_kv * HEAD_DIM
    assert t % WINDOW == 0
    nsub = 2 if (t // WINDOW) % 2 == 0 else 1
    nblk = t // (nsub * WINDOW)
    assert d % kvw == 0 and n_kv % 2 == 0 and d % nblk == 0 and (d // nblk) % 16 == 0
    wrows = d // nblk
    kblk = d // kvw
    vblk = kblk + 1
    assert 0 < qkv.shape[0] - t <= nsub * WINDOW
    cur = lambda n: jnp.minimum(n, nblk - 1)
    prev = lambda n: jnp.maximum(cur(n) * nsub - 1, 0)
    return pl.pallas_call(
        functools.partial(_swa_prompt_body, n_kv=n_kv, nsub=nsub),
        grid=(nblk + 1,),
        in_specs=[
            pl.BlockSpec(memory_space=pltpu.SMEM),
            pl.BlockSpec((nsub * WINDOW, d), lambda n: (cur(n), 0)),
            pl.BlockSpec((WINDOW, kvw), lambda n: (prev(n), kblk)),
            pl.BlockSpec((nsub * WINDOW, kvw), lambda n: (cur(n), kblk)),
            pl.BlockSpec((WINDOW, kvw), lambda n: (prev(n), vblk)),
            pl.BlockSpec((nsub * WINDOW, kvw), lambda n: (cur(n), vblk)),
            pl.BlockSpec((None, wrows, d), lambda n: (layer, cur(n), 0)),
        ],
        out_specs=[pl.BlockSpec((d, nsub * WINDOW), lambda n: (0, n)),
                   pl.BlockSpec((wrows, d), lambda n: (cur(n), 0))],
        out_shape=[jax.ShapeDtypeStruct((d, qkv.shape[0]), BF16), jax.ShapeDtypeStruct((d, d), BF16)],
        compiler_params=_params(("arbitrary",)),
        name="swa_prompt",
    )(sinks, qkv, qkv, qkv, qkv, qkv, w_o)


def _swa_sample_body(sink_ref, qkvt_ref, k_ref, v_ref, o_any, ko_ref, vo_ref, o_ref, acc_ref, *, n_kv, bt):
    del o_any
    i = pl.program_id(0)
    nk = k_ref.shape[1]
    nb = qkvt_ref.shape[1]
    kvw = n_kv * HEAD_DIM
    n_heads = n_kv * GQA_GROUP
    nq = n_heads * HEAD_DIM
    sub = 8

    @pl.when(i == 0)
    def _():
        acc_ref[...] = jnp.zeros_like(acc_ref)

    key_lane = lax.broadcasted_iota(jnp.int32, (1, nk), 1)
    sublane = lax.broadcasted_iota(jnp.int32, (sub, nk), 0)
    head_row = lax.broadcasted_iota(jnp.int32, (n_heads, 1), 0)
    sink = jnp.zeros((n_heads, 1), F32)
    for hd in range(n_heads):
        sink = jnp.where(head_row == hd, sink_ref[hd], sink)
    qt = (qkvt_ref[:nq, :] * (1.0 / math.sqrt(HEAD_DIM))).astype(BF16)
    knt = qkvt_ref[nq:nq + kvw, :]
    vnt = qkvt_ref[nq + kvw:, :]
    gs = bt
    wide_row = lax.broadcasted_iota(jnp.int32, (nb, gs * nk), 0)
    wide_lane_group = lax.broadcasted_iota(jnp.int32, (nb, gs * nk), 1) // nk
    tall_lane_group = lax.broadcasted_iota(jnp.int32, (gs * nk, nb), 0) // nk
    tall_lane = lax.broadcasted_iota(jnp.int32, (gs * nk, nb), 1)
    qcols_of = [jnp.dot(qt, (wide_row == wide_lane_group + (i * bt + g0)).astype(BF16),
                        preferred_element_type=F32) for g0 in range(0, bt, gs)]
    kfulls, vfulls = [], []
    assert nb == nk
    for j in range(bt):
        to_last = nk - 1 - (i * bt + j)
        kn = pltpu.roll(knt, to_last, 1)
        vn = pltpu.roll(vnt, to_last, 1)
        rows = slice(j * kvw, (j + 1) * kvw)
        kfulls.append(jnp.where(key_lane == nk - 1, kn, pltpu.roll(k_ref[rows, :], nk - 1, 1)))
        vfulls.append(jnp.where(key_lane == nk - 1, vn, pltpu.roll(v_ref[rows, :], nk - 1, 1)))
        ko_ref[rows, :] = kfulls[j]
        vo_ref[rows, :] = vfulls[j]
    all_scores = []
    for j in range(bt):
        qcols = qcols_of[j // gs][:, (j % gs) * nk:(j % gs + 1) * nk]
        kfull = kfulls[j]
        tiles = []
        for tile in range(n_heads // sub):
            st = jnp.zeros((sub, nk), F32)
            for r in range(sub):
                hidx = tile * sub + r
                kv = hidx // GQA_GROUP
                prod = kfull[kv * HEAD_DIM:(kv + 1) * HEAD_DIM] * qcols[hidx * HEAD_DIM:(hidx + 1) * HEAD_DIM]
                red = jnp.sum(prod.reshape(HEAD_DIM // sub, sub, nk), axis=0)
                step = sub // 2
                while step >= 1:
                    red = red + pltpu.roll(red, step, 0)
                    step //= 2
                st = jnp.where(sublane == r, red, st)
            tiles.append(st)
        all_scores.append(jnp.concatenate(tiles, axis=0))
    probs = []
    for s in all_scores:
        mx = jnp.maximum(jnp.max(s, axis=1, keepdims=True), sink)
        p = jnp.exp(s - mx)
        probs.append(p / (jnp.sum(p, axis=1, keepdims=True) + jnp.exp(sink - mx)))
    pvs = []
    for j in range(bt):
        pn, vfull = probs[j], vfulls[j]
        pvs.append(jnp.concatenate(
            [vfull[(hidx // GQA_GROUP) * HEAD_DIM:(hidx // GQA_GROUP + 1) * HEAD_DIM] * pn[hidx:hidx + 1, :]
             for hidx in range(n_heads)], axis=0).astype(BF16))
        if len(pvs) == gs:
            first = i * bt + j + 1 - gs
            acc_ref[...] += jnp.dot(jnp.concatenate(pvs, axis=1),
                                    (tall_lane_group + first == tall_lane).astype(BF16),
                                    preferred_element_type=F32)
            pvs = []

    @pl.when(i == pl.num_programs(0) - 1)
    def _():
        o_ref[...] = acc_ref[...].astype(o_ref.dtype)


def swa_sample(qkvt, kt, vt, sinks, o_fm, n_kv):
    nrow, b = qkvt.shape
    kvw = n_kv * HEAD_DIM
    d = nrow - 2 * kvw
    nk = kt.shape[1]
    t = o_fm.shape[1] - b
    assert d == n_kv * GQA_GROUP * HEAD_DIM and (n_kv * GQA_GROUP) % 8 == 0 and t % b == 0
    bt = _pick(b, (8, 4, 2, 1))
    cache_spec = pl.BlockSpec((bt * kvw, nk), lambda i: (i, 0))
    return pl.pallas_call(
        functools.partial(_swa_sample_body, n_kv=n_kv, bt=bt),
        grid=(b // bt,),
        in_specs=[
            pl.BlockSpec(memory_space=pltpu.SMEM),
            pl.BlockSpec((nrow, b), lambda i: (0, 0)),
            cache_spec,
            cache_spec,
            pl.BlockSpec(memory_space=pl.ANY),
        ],
        out_specs=[cache_spec, cache_spec, pl.BlockSpec((d, b), lambda i: (0, t // b))],
        out_shape=[jax.ShapeDtypeStruct(kt.shape, F32), jax.ShapeDtypeStruct(vt.shape, F32),
                   jax.ShapeDtypeStruct(o_fm.shape, o_fm.dtype)],
        scratch_shapes=[pltpu.VMEM((d, b), F32)],
        input_output_aliases={4: 2},
        compiler_params=_params(("arbitrary",)),
        name="swa_sample",
    )(sinks, qkvt, kt, vt, o_fm)


def _pool_prompt_body(x_ref, g_ref, w_ref, sc_ref, gn_ref, hn_any, o_ref, hn_ref, hl_ref, wb_ref, *, gc):
    del hn_any
    i = pl.program_id(0)

    @pl.when(i == 0)
    def _():
        wb_ref[...] = w_ref[...].astype(BF16)
        hl_ref[...] = jnp.zeros_like(hl_ref)

    bm = x_ref.shape[0]
    x = x_ref[...]
    h = _rms(x, g_ref[...])
    hprev = hl_ref[...]
    hl_ref[...] = h[bm - POOL_MAXW:, :]
    pos1 = (lax.broadcasted_iota(jnp.int32, (bm, 1), 0) + i * bm + 1).astype(F32)
    for gi, w in enumerate(POOL_WINDOWS):
        cs = slice(gi * gc, (gi + 1) * gc)
        hg = h[:, cs]
        acc = jnp.concatenate([hprev[:, cs], hg], axis=0)
        step = 1
        while step < w:
            acc = acc + pltpu.roll(acc, step, 0)
            step *= 2
        inv_cnt = 1.0 / jnp.minimum(jnp.float32(w), pos1)
        pooled = (acc[POOL_MAXW:, :] * inv_cnt - hg).astype(BF16)
        y = jnp.dot(pooled, wb_ref[gi], preferred_element_type=F32)
        o_ref[:, cs] = x[:, cs] + y * sc_ref[:, cs]
    hn_ref[...] = _rms(o_ref[...], gn_ref[...]).astype(hn_ref.dtype)


def pool_prompt(x_all, hn_all, g, w_pool, layer, scale, g_next, t):
    d = x_all.shape[1]
    _, ng, gc, _ = w_pool.shape
    bm = _pick(t, (512, 256, 128, 64, 32, 16))
    return pl.pallas_call(
        functools.partial(_pool_prompt_body, gc=gc),
        grid=(t // bm,),
        in_specs=[
            pl.BlockSpec((bm, d), lambda i: (i, 0)),
            pl.BlockSpec((1, d), lambda i: (0, 0)),
            pl.BlockSpec((None, ng, gc, gc), lambda i: (layer, 0, 0, 0)),
            pl.BlockSpec((1, d), lambda i: (0, 0)),
            pl.BlockSpec((1, d), lambda i: (0, 0)),
            pl.BlockSpec(memory_space=pl.ANY),
        ],
        out_specs=[
            pl.BlockSpec((bm, d), lambda i: (i, 0)),
            pl.BlockSpec((bm, d), lambda i: (i, 0)),
            pl.BlockSpec((POOL_MAXW, d), lambda i: (0, 0)),
        ],
        out_shape=[jax.ShapeDtypeStruct(x_all.shape, F32), jax.ShapeDtypeStruct(hn_all.shape, hn_all.dtype),
                   jax.ShapeDtypeStruct((POOL_MAXW, d), F32)],
        scratch_shapes=[pltpu.VMEM((ng, gc, gc), BF16)],
        input_output_aliases={0: 0, 5: 1},
        compiler_params=_params(("arbitrary",)),
        name="pool_prompt",
    )(x_all, g.reshape(1, d), w_pool, scale.reshape(1, d), g_next.reshape(1, d), hn_all)


def _pool_sample_body(x_ref, st_ref, g_ref, w_ref, sc_ref, gn_ref, hn_any, o_ref, hn_ref, ns_ref, *, gc):
    del hn_any
    x = x_ref[...]
    h = _rms(x, g_ref[...])
    ns_ref[:POOL_BUF - 1] = st_ref[1:]
    ns_ref[POOL_BUF - 1] = h
    for gi, w in enumerate(POOL_WINDOWS):
        cs = slice(gi * gc, (gi + 1) * gc)
        hg = h[:, cs]
        tot = hg + jnp.sum(st_ref[POOL_BUF - (w - 1):, :, cs], axis=0)
        cnt = float(min(w, PAST_LEN + 1))
        pooled = (tot / cnt - hg).astype(BF16)
        y = jnp.dot(pooled, w_ref[gi].astype(BF16), preferred_element_type=F32)
        o_ref[:, cs] = x[:, cs] + y * sc_ref[:, cs]
    hn_ref[...] = _rms(o_ref[...], gn_ref[...]).astype(hn_ref.dtype)


def pool_sample(x_all, hn_all, state, layer, g, w_pool, scale, g_next, t):
    d = x_all.shape[1]
    b = state.shape[2]
    _, ng, gc, _ = w_pool.shape
    bt = _pick(b, (32, 16, 8))
    assert t % bt == 0
    off = t // bt
    return pl.pallas_call(
        functools.partial(_pool_sample_body, gc=gc),
        grid=(b // bt,),
        in_specs=[
            pl.BlockSpec((bt, d), lambda i: (off + i, 0)),
            pl.BlockSpec((None, POOL_BUF, bt, d), lambda i: (layer, 0, i, 0)),
            pl.BlockSpec((1, d), lambda i: (0, 0)),
            pl.BlockSpec((None, ng, gc, gc), lambda i: (layer, 0, 0, 0)),
            pl.BlockSpec((1, d), lambda i: (0, 0)),
            pl.BlockSpec((1, d), lambda i: (0, 0)),
            pl.BlockSpec(memory_space=pl.ANY),
        ],
        out_specs=[
            pl.BlockSpec((bt, d), lambda i: (off + i, 0)),
            pl.BlockSpec((bt, d), lambda i: (off + i, 0)),
            pl.BlockSpec((POOL_BUF, bt, d), lambda i: (0, i, 0)),
        ],
        out_shape=[jax.ShapeDtypeStruct(x_all.shape, F32), jax.ShapeDtypeStruct(hn_all.shape, hn_all.dtype),
                   jax.ShapeDtypeStruct((POOL_BUF, b, d), F32)],
        input_output_aliases={0: 0, 6: 1},
        compiler_params=_params(("arbitrary",)),
        name="pool_sample",
    )(x_all, state, g.reshape(1, d), w_pool, scale.reshape(1, d), g_next.reshape(1, d), hn_all)


def _mem_kv_body(m_ref, g_ref, w_ref, o_ref):
    h = _rms(m_ref[...], g_ref[...]).astype(BF16)
    o_ref[...] = jnp.dot(h, w_ref[...].astype(BF16), preferred_element_type=F32)


def mem_kv(mem, g, w_kv, layer):
    m, d = mem.shape
    n = w_kv.shape[2]
    bn = _pick(n, (512, 256, 128))
    return pl.pallas_call(
        _mem_kv_body,
        grid=(n // bn,),
        in_specs=[
            pl.BlockSpec((m, d), lambda j: (0, 0)),
            pl.BlockSpec((1, d), lambda j: (0, 0)),
            pl.BlockSpec((None, d, bn), lambda j: (layer, 0, j)),
        ],
        out_specs=pl.BlockSpec((m, bn), lambda j: (0, j)),
        out_shape=jax.ShapeDtypeStruct((m, n), F32),
        compiler_params=_params(("arbitrary",)),
        name="mem_kv",
    )(mem, g.reshape(1, d), w_kv)


def _xattn_prompt_body(h_ref, wq_ref, kv_ref, wo_ref, x_ref, gn_ref, o_ref, hn_ref, wqb_ref, wob_ref, kvb_ref):
    @pl.when(pl.program_id(0) == 0)
    def _():
        wqb_ref[...] = wq_ref[...].astype(BF16)
        wob_ref[...] = wo_ref[...].astype(BF16)
        kvb_ref[...] = kv_ref[...].astype(BF16)

    xw = X_HEADS * X_HEAD_DIM
    bm = h_ref.shape[0]
    nchunk = 2 if bm % 32 == 0 else 1
    rows = [slice(c * (bm // nchunk), (c + 1) * (bm // nchunk)) for c in range(nchunk)]
    qs = [jnp.dot(h_ref[r, :], wqb_ref[...], preferred_element_type=F32) / math.sqrt(X_HEAD_DIM) for r in rows]
    scores = [[lax.dot_general(kvb_ref[:, hd * X_HEAD_DIM:(hd + 1) * X_HEAD_DIM],
                               q[:, hd * X_HEAD_DIM:(hd + 1) * X_HEAD_DIM].astype(BF16), (((1,), (1,)), ((), ())),
                               preferred_element_type=F32) for hd in range(X_HEADS)] for q in qs]
    attn = []
    for c in range(nchunk):
        outs = []
        for hd in range(X_HEADS):
            s = scores[c][hd]
            p = jnp.exp(s - jnp.max(s, axis=0, keepdims=True))
            l = jnp.sum(p, axis=0, keepdims=True)
            vs = slice(xw + hd * X_HEAD_DIM, xw + (hd + 1) * X_HEAD_DIM)
            outs.append(lax.dot_general(kvb_ref[:, vs], p.astype(BF16), (((0,), (0,)), ((), ())),
                                        preferred_element_type=F32) / l)
        attn.append(jnp.concatenate(outs, axis=0).astype(BF16))
    for c, r in enumerate(rows):
        xn = x_ref[r, :] + lax.dot_general(attn[c], wob_ref[...], (((0,), (0,)), ((), ())),
                                           preferred_element_type=F32)
        o_ref[r, :] = xn
        hn_ref[r, :] = _rms(xn, gn_ref[...]).astype(hn_ref.dtype)


def xattn_prompt(hq_all, w_q, mkv, w_o, layer, x_all, g_next, t):
    d = x_all.shape[1]
    xw = w_q.shape[2]
    nm = mkv.shape[0]
    bm = _pick(t, (512, 256, 128, 64, 32, 16))
    return pl.pallas_call(
        _xattn_prompt_body,
        grid=(t // bm,),
        in_specs=[
            pl.BlockSpec((bm, d), lambda i: (i, 0)),
            pl.BlockSpec((None, d, xw), lambda i: (layer, 0, 0)),
            pl.BlockSpec((nm, 2 * xw), lambda i: (0, 0)),
            pl.BlockSpec((None, xw, d), lambda i: (layer, 0, 0)),
            pl.BlockSpec((bm, d), lambda i: (i, 0)),
            pl.BlockSpec((1, d), lambda i: (0, 0)),
        ],
        out_specs=[pl.BlockSpec((bm, d), lambda i: (i, 0)), pl.BlockSpec((bm, d), lambda i: (i, 0))],
        out_shape=[jax.ShapeDtypeStruct(x_all.shape, F32), jax.ShapeDtypeStruct(x_all.shape, BF16)],
        scratch_shapes=[pltpu.VMEM((d, xw), BF16), pltpu.VMEM((xw, d), BF16), pltpu.VMEM((nm, 2 * xw), BF16)],
        input_output_aliases={4: 0, 0: 1},
        compiler_params=_params(("arbitrary",)),
        name="xattn_prompt",
    )(hq_all, w_q, mkv, w_o, x_all, g_next.reshape(1, d))


def _xq_sample_body(h_ref, wq_ref, o_ref):
    o_ref[...] = jnp.dot(h_ref[...], wq_ref[...].astype(BF16), preferred_element_type=F32) / math.sqrt(X_HEAD_DIM)


def xq_sample(hq_all, w_q, layer, t, b):
    d = hq_all.shape[1]
    xw = w_q.shape[2]
    assert t % b == 0
    return pl.pallas_call(
        _xq_sample_body,
        grid=(1,),
        in_specs=[
            pl.BlockSpec((b, d), lambda i: (t // b, 0)),
            pl.BlockSpec((None, d, xw), lambda i: (layer, 0, 0)),
        ],
        out_specs=pl.BlockSpec((b, xw), lambda i: (0, 0)),
        out_shape=jax.ShapeDtypeStruct((b, xw), F32),
        compiler_params=_params(("arbitrary",)),
        name="xq_sample",
    )(hq_all, w_q)


def _xattn_core_body(q_ref, k_ref, v_ref, ones_ref, o_ref):
    bt, sub, hd = q_ref.shape
    nv = k_ref.shape[0] // (bt * sub)
    k = k_ref[...].reshape(bt, nv, sub, hd)
    prod = (k * q_ref[...][:, None]).reshape(bt * nv * sub, hd).astype(BF16)
    s = jnp.dot(prod, ones_ref[...], preferred_element_type=F32).reshape(bt, nv, sub, hd)

    def fold(x, op):
        step = X_HEADS
        while step < sub:
            x = op(x, pltpu.roll(x, step, 2))
            step *= 2
        return x

    mx = fold(jnp.max(s, axis=1, keepdims=True), jnp.maximum)
    p = jnp.exp(s - mx)
    l = fold(jnp.sum(p, axis=1, keepdims=True), jnp.add)
    acc = fold(jnp.sum(p * v_ref[...].reshape(bt, nv, sub, hd), axis=1, keepdims=True), jnp.add)
    o_ref[...] = (acc / l).reshape(bt, sub, hd)


def xattn_core_sample(q_s, mk_flat, mv_flat, layer, nm):
    b = q_s.shape[0]
    hd = X_HEAD_DIM
    sub = 8
    assert sub % X_HEADS == 0 and nm % (sub // X_HEADS) == 0
    bt = _pick(b, (8, 4, 2, 1))
    nblk = b // bt
    rows = bt * nm * X_HEADS
    q8 = jnp.tile(q_s.reshape(b, X_HEADS, hd), (1, sub // X_HEADS, 1))
    o8 = pl.pallas_call(
        _xattn_core_body,
        grid=(nblk,),
        in_specs=[
            pl.BlockSpec((bt, sub, hd), lambda i: (i, 0, 0)),
            pl.BlockSpec((rows, hd), lambda i: (layer * nblk + i, 0)),
            pl.BlockSpec((rows, hd), lambda i: (layer * nblk + i, 0)),
            pl.BlockSpec((hd, hd), lambda i: (0, 0)),
        ],
        out_specs=pl.BlockSpec((bt, sub, hd), lambda i: (i, 0, 0)),
        out_shape=jax.ShapeDtypeStruct((b, sub, hd), F32),
        compiler_params=_params(("arbitrary",)),
        name="xattn_core_sample",
    )(q8, mk_flat, mv_flat, jnp.ones((hd, hd), BF16))
    return o8[:, :X_HEADS].reshape(b, X_HEADS * hd)


def _xout_sample_body(a_ref, wo_ref, x_ref, gn_ref, hn_any, o_ref, hn_ref):
    del hn_any
    xn = x_ref[...] + jnp.dot(a_ref[...].astype(BF16), wo_ref[...].astype(BF16), preferred_element_type=F32)
    o_ref[...] = xn
    hn_ref[...] = _rms(xn, gn_ref[...]).astype(hn_ref.dtype)


def xout_sample(o_s, w_o, layer, x_all, hn_all, g_next, t):
    d = x_all.shape[1]
    b, xw = o_s.shape
    assert t % b == 0
    off = t // b
    return pl.pallas_call(
        _xout_sample_body,
        grid=(1,),
        in_specs=[
            pl.BlockSpec((b, xw), lambda i: (0, 0)),
            pl.BlockSpec((None, xw, d), lambda i: (layer, 0, 0)),
            pl.BlockSpec((b, d), lambda i: (off, 0)),
            pl.BlockSpec((1, d), lambda i: (0, 0)),
            pl.BlockSpec(memory_space=pl.ANY),
        ],
        out_specs=[pl.BlockSpec((b, d), lambda i: (off, 0)), pl.BlockSpec((b, d), lambda i: (off, 0))],
        out_shape=[jax.ShapeDtypeStruct(x_all.shape, F32), jax.ShapeDtypeStruct(hn_all.shape, hn_all.dtype)],
        input_output_aliases={2: 0, 4: 1},
        compiler_params=_params(("arbitrary",)),
        name="xout_sample",
    )(o_s, w_o, x_all, g_next.reshape(1, d), hn_all)


def kernel(x_prompt, x_sample, state_pool, cache_swa_k, cache_swa_v, cache_mem_k, cache_mem_v, mem_prompt,
           g_ffn1, w_ffn1_gu, w_ffn1_dn, g_mix, w_pool, pool_scale, w_qkv, w_o, sinks,
           g_xq, g_mem, w_xq, w_xkv, w_xo, g_ffn2, w_ffn2_gu, w_ffn2_dn, g_final):
    bp, t, d = x_prompt.shape
    b, s_len, _ = x_sample.shape
    assert bp == 1 and s_len == 1
    depth = g_ffn1.shape[0]
    xw = w_xq.shape[2]
    nm = mem_prompt.shape[1]
    n_kv = d // HEAD_DIM // GQA_GROUP
    kvw = n_kv * HEAD_DIM

    h = stacked_norm(x_prompt[0], x_sample[:, 0], g_ffn1[0])
    x = None
    cos_t, sin_t = rope_tables(t, b)
    mk_flat = cache_mem_k.reshape(-1, X_HEAD_DIM)
    mv_flat = cache_mem_v.reshape(-1, X_HEAD_DIM)
    state_sm = jnp.swapaxes(state_pool, 1, 2)

    pool_p, pool_s = [], []
    swa_kp, swa_vp, swa_ks, swa_vs = [], [], [], []
    mem_kp, mem_vp = [], []
    y_p = y_s = None
    for layer in range(depth):
        i = layer // 2
        act, wdb = gate_up(h, w_ffn1_gu, w_ffn1_dn, layer)
        if x is None:
            x, h = down_norm_stacking(act, wdb, x_prompt[0], x_sample[:, 0], g_mix[layer], BF16)
        else:
            x, h = down_norm(act, wdb, x, g_mix[layer], BF16)
        if layer % 2 == 0:
            x, hq, h_last = pool_prompt(x, h, g_mix[layer], w_pool, i, pool_scale[i], g_xq[layer], t)
            x, hq, new_state = pool_sample(x, hq, state_sm, i, g_mix[layer], w_pool, pool_scale[i], g_xq[layer], t)
            pool_p.append(h_last[None, POOL_MAXW - POOL_BUF:])
            pool_s.append(jnp.swapaxes(new_state, 0, 1))
        else:
            qkv = qkv_rope(h, w_qkv, i, cos_t, sin_t, d + kvw, kvw)
            buf = cache_swa_k.shape[2]
            keep = min(WINDOW, t)
            swa_kp.append(qkv[t - keep:t, d:d + kvw].reshape(1, keep, n_kv, HEAD_DIM))
            swa_vp.append(qkv[t - keep:t, d + kvw:].reshape(1, keep, n_kv, HEAD_DIM))
            assert buf == WINDOW
            to_fm = lambda c: jnp.transpose(c, (0, 2, 3, 1)).reshape(b * kvw, buf)
            from_fm = lambda c: jnp.transpose(c.reshape(b, n_kv, HEAD_DIM, buf), (0, 3, 1, 2))
            o_fm, wo_bf16 = swa_prompt(qkv, sinks[i], w_o, i, t, d)
            ks_fm, vs_fm, o_fm = swa_sample(qkv[t:].T, to_fm(cache_swa_k[i]), to_fm(cache_swa_v[i]), sinks[i],
                                            o_fm, n_kv)
            swa_ks.append(from_fm(ks_fm))
            swa_vs.append(from_fm(vs_fm))
            x, hq = proj_residual_norm(o_fm, wo_bf16, x, g_xq[layer])
        mkv = mem_kv(mem_prompt[0], g_mem[layer], w_xkv, layer)
        mem_kp.append(mkv[:, :xw].reshape(1, nm, X_HEADS, X_HEAD_DIM))
        mem_vp.append(mkv[:, xw:].reshape(1, nm, X_HEADS, X_HEAD_DIM))
        q_s = xq_sample(hq, w_xq, layer, t, b)
        x, h = xattn_prompt(hq, w_xq, mkv, w_xo, layer, x, g_ffn2[layer], t)
        o_s = xattn_core_sample(q_s, mk_flat, mv_flat, layer, nm)
        x, h = xout_sample(o_s, w_xo, layer, x, h, g_ffn2[layer], t)
        act, wdb = gate_up(h, w_ffn2_gu, w_ffn2_dn, layer)
        if layer + 1 < depth:
            x, h = down_norm(act, wdb, x, g_ffn1[layer + 1], BF16)
        else:
            (y_p,) = down_norm(act, wdb, x, g_final, F32, emit_x=False, row0=0, nrows=t)
            (y_s,) = down_norm(act, wdb, x, g_final, F32, emit_x=False, row0=t, nrows=b)
    return (y_p[None], y_s[:, None], jnp.stack(pool_p), jnp.stack(pool_s), jnp.stack(swa_kp), jnp.stack(swa_vp),
            jnp.stack(swa_ks), jnp.stack(swa_vs), jnp.stack(mem_kp), jnp.stack(mem_vp))
```

```python
import functools
import math

import jax
import jax.numpy as jnp
from jax import lax
from jax.experimental import pallas as pl
from jax.experimental.pallas import tpu as pltpu

F32 = jnp.float32
BF16 = jnp.bfloat16

RMS_EPS = 1e-6
PAST_LEN = 8192
POOL_WINDOWS = (2, 4, 8, 16)
POOL_MAXW = max(POOL_WINDOWS)
POOL_BUF = POOL_MAXW - 1
HEAD_DIM = 64
GQA_GROUP = 4
WINDOW = 128
ROPE_THETA = 10000.0
X_HEADS = 4
X_HEAD_DIM = 128
LANES = 128
LOG2E = math.log2(math.e)
Q_SCALE = LOG2E / math.sqrt(HEAD_DIM)
V7X_VMEM_LIMIT = 60 * 1024 * 1024


def _params(sem):
    return pltpu.CompilerParams(dimension_semantics=sem, vmem_limit_bytes=V7X_VMEM_LIMIT)


def _pick(n, candidates):
    for c in candidates:
        if n % c == 0:
            return c
    raise ValueError(f"no block size in {candidates} divides {n}")


def _rms(x, g):
    return x * lax.rsqrt(jnp.mean(x * x, axis=-1, keepdims=True) + RMS_EPS) * g


def _stack_norm_body(xp_ref, xs_ref, g_ref, h_ref):
    nt = pl.num_programs(0) - 1
    b = xs_ref.shape[0]

    @pl.when(pl.program_id(0) < nt)
    def _():
        h_ref[...] = _rms(xp_ref[...], g_ref[...]).astype(h_ref.dtype)

    @pl.when(pl.program_id(0) == nt)
    def _():
        h_ref[:b, :] = _rms(xs_ref[...], g_ref[...]).astype(h_ref.dtype)


def stacked_norm(x_p, x_s, g):
    t, d = x_p.shape
    b = x_s.shape[0]
    bm = _pick(t, (512, 256, 128, 64, 32, 16))
    nt = t // bm
    assert b <= bm and b % 16 == 0
    return pl.pallas_call(
        _stack_norm_body,
        grid=(nt + 1,),
        in_specs=[pl.BlockSpec((bm, d), lambda i: (jnp.minimum(i, nt - 1), 0)),
                  pl.BlockSpec((b, d), lambda i: (0, 0)),
                  pl.BlockSpec((1, d), lambda i: (0, 0))],
        out_specs=pl.BlockSpec((bm, d), lambda i: (i, 0)),
        out_shape=jax.ShapeDtypeStruct((t + b, d), BF16),
        compiler_params=_params(("arbitrary",)),
        name="stack_norm",
    )(x_p, x_s, g.reshape(1, d))


def _gate_up_body(h_ref, wg_ref, wu_ref, wd_ref, o_ref, wdb_ref, wb_ref, *, bf):
    @pl.when(pl.program_id(1) == 0)
    def _():
        wb_ref[:, :bf] = wg_ref[...].astype(BF16)
        wb_ref[:, bf:] = wu_ref[...].astype(BF16)
        wdb_ref[...] = wd_ref[...].astype(BF16)

    bm = h_ref.shape[0]
    nchunk = 8 if bm % 128 == 0 else (4 if bm % 64 == 0 else 1)
    cr = bm // nchunk
    rs = [jnp.dot(h_ref[c * cr:(c + 1) * cr, :], wb_ref[...], preferred_element_type=F32) for c in range(nchunk)]
    for c in range(nchunk):
        a = rs[c][:, :bf]
        b = rs[c][:, bf:]
        o_ref[c * cr:(c + 1) * cr, :] = (a / (1.0 + jnp.exp(-a)) * b * 0.5).astype(o_ref.dtype)


def gate_up(h, w_gu, w_dn, layer):
    m, d = h.shape
    f = w_gu.shape[2] // 2
    bf = _pick(f, (512, 256, 128))
    bm = _pick(m, (1664, 1024, 512, 256, 128, 64, 32, 16))
    nf = f // bf
    return pl.pallas_call(
        functools.partial(_gate_up_body, bf=bf),
        grid=(nf, m // bm),
        in_specs=[
            pl.BlockSpec((bm, d), lambda j, i: (i, 0)),
            pl.BlockSpec((None, d, bf), lambda j, i: (layer, 0, j)),
            pl.BlockSpec((None, d, bf), lambda j, i: (layer, 0, j + nf)),
            pl.BlockSpec((None, bf, d), lambda j, i: (layer, j, 0)),
        ],
        out_specs=[
            pl.BlockSpec((bm, bf), lambda j, i: (i, j)),
            pl.BlockSpec((bf, d), lambda j, i: (j, 0)),
        ],
        out_shape=[jax.ShapeDtypeStruct((m, f), BF16), jax.ShapeDtypeStruct((f, d), BF16)],
        scratch_shapes=[pltpu.VMEM((d, 2 * bf), BF16)],
        compiler_params=_params(("arbitrary", "arbitrary")),
        name="gate_up",
    )(h, w_gu, w_gu, w_dn)


def _down_body(a_ref, w_ref, x_ref, g_ref, *out_refs, emit_x):
    xn = x_ref[...] + jnp.dot(a_ref[...], w_ref[...], preferred_element_type=F32)
    if emit_x:
        out_refs[0][...] = xn
    h_ref = out_refs[-1]
    h_ref[...] = _rms(xn, g_ref[...]).astype(h_ref.dtype)


def down_norm(a, w_bf16, x, g_next, h_dtype, emit_x=True, row0=0, nrows=None):
    m, k = a.shape
    d = w_bf16.shape[1]
    nrows = m if nrows is None else nrows
    bm = _pick(math.gcd(nrows, row0) if row0 else nrows, (416, 320, 256, 128, 64, 32, 16))
    off = row0 // bm
    row = lambda i: (i, 0)
    src = lambda i: (i + off, 0)
    out_specs = [pl.BlockSpec((bm, d), row)]
    out_shape = [jax.ShapeDtypeStruct((nrows, d), h_dtype)]
    if emit_x:
        out_specs.insert(0, pl.BlockSpec((bm, d), src))
        out_shape.insert(0, jax.ShapeDtypeStruct((m, d), F32))
    return pl.pallas_call(
        functools.partial(_down_body, emit_x=emit_x),
        grid=(nrows // bm,),
        in_specs=[
            pl.BlockSpec((bm, k), src),
            pl.BlockSpec((k, d), lambda i: (0, 0), pipeline_mode=pl.Buffered(1)),
            pl.BlockSpec((bm, d), src),
            pl.BlockSpec((1, d), lambda i: (0, 0)),
        ],
        out_specs=out_specs,
        out_shape=out_shape,
        input_output_aliases={2: 0} if emit_x else {},
        compiler_params=_params(("arbitrary",)),
        name="down_norm",
    )(a, w_bf16, x, g_next.reshape(1, d))


def _down_first_body(a_ref, w_ref, xp_ref, xs_ref, g_ref, xo_ref, h_ref, *, split):
    bm = a_ref.shape[0]
    last = pl.num_programs(0) - 1

    def finish(rows, x):
        xn = x + jnp.dot(a_ref[rows, :], w_ref[...], preferred_element_type=F32)
        xo_ref[rows, :] = xn
        h_ref[rows, :] = _rms(xn, g_ref[...]).astype(h_ref.dtype)

    @pl.when(pl.program_id(0) < last)
    def _():
        finish(slice(0, bm), xp_ref[...])

    @pl.when(pl.program_id(0) == last)
    def _():
        finish(slice(0, split), xp_ref[:split, :])
        finish(slice(split, bm), xs_ref[...])


def down_norm_stacking(a, w_bf16, x_p, x_s, g_next, h_dtype):
    m, k = a.shape
    d = w_bf16.shape[1]
    t, b = x_p.shape[0], x_s.shape[0]
    bm = _pick(m, (416, 320, 256, 128, 64, 32, 16))
    nt = m // bm
    split = t - (nt - 1) * bm
    assert m == t + b and bm - split == b and split > 0 and split % 16 == 0
    row = lambda i: (i, 0)
    return pl.pallas_call(
        functools.partial(_down_first_body, split=split),
        grid=(nt,),
        in_specs=[
            pl.BlockSpec((bm, k), row),
            pl.BlockSpec((k, d), lambda i: (0, 0), pipeline_mode=pl.Buffered(1)),
            pl.BlockSpec((bm, d), row),
            pl.BlockSpec((b, d), lambda i: (0, 0)),
            pl.BlockSpec((1, d), lambda i: (0, 0)),
        ],
        out_specs=[pl.BlockSpec((bm, d), row), pl.BlockSpec((bm, d), row)],
        out_shape=[jax.ShapeDtypeStruct((m, d), F32), jax.ShapeDtypeStruct((m, d), h_dtype)],
        compiler_params=_params(("arbitrary",)),
        name="down_norm_stacking",
    )(a, w_bf16, x_p, x_s, g_next.reshape(1, d))


def _proj_res_body(a_ref, w_ref, x_ref, g_ref, xo_ref, h_ref):
    xn = x_ref[...] + lax.dot_general(a_ref[...], w_ref[...], (((0,), (0,)), ((), ())),
                                      preferred_element_type=F32)
    xo_ref[...] = xn
    h_ref[...] = _rms(xn, g_ref[...]).astype(h_ref.dtype)


def proj_residual_norm(a_fm, w_bf16, x, g_next):
    k, m = a_fm.shape
    d = w_bf16.shape[1]
    bm = _pick(m, (640, 512, 256, 128))
    row = lambda i: (i, 0)
    return pl.pallas_call(
        _proj_res_body,
        grid=(m // bm,),
        in_specs=[
            pl.BlockSpec((k, bm), lambda i: (0, i)),
            pl.BlockSpec((k, d), lambda i: (0, 0), pipeline_mode=pl.Buffered(1)),
            pl.BlockSpec((bm, d), row),
            pl.BlockSpec((1, d), lambda i: (0, 0)),
        ],
        out_specs=[pl.BlockSpec((bm, d), row), pl.BlockSpec((bm, d), row)],
        out_shape=[jax.ShapeDtypeStruct((m, d), F32), jax.ShapeDtypeStruct((m, d), BF16)],
        input_output_aliases={2: 0},
        compiler_params=_params(("arbitrary",)),
        name="proj_residual_norm",
    )(a_fm, w_bf16, x, g_next.reshape(1, d))


def _qkv_body(h_ref, w_ref, c_ref, s_ref, o_ref, wb_ref, *, n_q_tiles, n_rope_tiles):
    @pl.when(pl.program_id(1) == 0)
    def _():
        wb_ref[...] = w_ref[...].astype(BF16)

    bm, bn = o_ref.shape
    is_rope = pl.program_id(0) < n_rope_tiles
    scale = jnp.where(pl.program_id(0) < n_q_tiles, Q_SCALE, 1.0).astype(F32)
    nchunk = 8 if bm % 128 == 0 else (4 if bm % 64 == 0 else 1)
    cr = bm // nchunk
    rs = [jnp.dot(h_ref[c * cr:(c + 1) * cr, :], wb_ref[...], preferred_element_type=F32) for c in range(nchunk)]
    lane = lax.broadcasted_iota(jnp.int32, (cr, LANES), 1)
    first_half = (lane % HEAD_DIM) < (HEAD_DIM // 2)
    for c in range(nchunk):
        rows = slice(c * cr, (c + 1) * cr)
        cos = c_ref[rows, :]
        sin = s_ref[rows, :]
        for ci in range(bn // LANES):
            blk = rs[c][:, ci * LANES:(ci + 1) * LANES] * scale
            partner = jnp.where(first_half,
                                pltpu.roll(blk, LANES - HEAD_DIM // 2, 1),
                                pltpu.roll(blk, HEAD_DIM // 2, 1))
            o_ref[rows, ci * LANES:(ci + 1) * LANES] = jnp.where(is_rope, blk * cos + partner * sin, blk)


def qkv_rope(h, w_qkv, layer, cos_t, sin_t, n_rope_cols, bn):
    m, d = h.shape
    n = w_qkv.shape[2]
    assert n % bn == 0 and n_rope_cols % bn == 0 and bn % LANES == 0
    bm = _pick(m, (1664, 1024, 512, 256, 128, 64, 32, 16))
    return pl.pallas_call(
        functools.partial(_qkv_body, n_q_tiles=d // bn, n_rope_tiles=n_rope_cols // bn),
        grid=(n // bn, m // bm),
        in_specs=[
            pl.BlockSpec((bm, d), lambda j, i: (i, 0)),
            pl.BlockSpec((None, d, bn), lambda j, i: (layer, 0, j)),
            pl.BlockSpec((bm, LANES), lambda j, i: (i, 0)),
            pl.BlockSpec((bm, LANES), lambda j, i: (i, 0)),
        ],
        out_specs=pl.BlockSpec((bm, bn), lambda j, i: (i, j)),
        out_shape=jax.ShapeDtypeStruct((m, n), F32),
        scratch_shapes=[pltpu.VMEM((d, bn), BF16)],
        compiler_params=_params(("arbitrary", "arbitrary")),
        name="qkv_rope",
    )(h, w_qkv, cos_t, sin_t)


def rope_tables(t_prompt, n_sample):
    half = HEAD_DIM // 2
    inv = ROPE_THETA ** (-jnp.arange(half, dtype=F32) / half)
    pos = jnp.concatenate([jnp.arange(t_prompt), jnp.full((n_sample,), PAST_LEN)]).astype(F32)
    ang = pos[:, None] * inv[None, :]
    cos, sin = jnp.cos(ang), jnp.sin(ang)
    reps = LANES // HEAD_DIM
    return (jnp.tile(jnp.concatenate([cos, cos], axis=1), (1, reps)),
            jnp.tile(jnp.concatenate([-sin, sin], axis=1), (1, reps)))


def _swa_prompt_body(sink_ref, q_ref, kp_ref, kc_ref, vp_ref, vc_ref, wo_ref, o_ref, wob_ref, *, n_kv, nsub):
    n = pl.program_id(0)
    blk = WINDOW
    hd = HEAD_DIM
    assert LANES == 2 * hd
    wob_ref[...] = wo_ref[...].astype(BF16)
    nq = GQA_GROUP * blk
    keys = lax.broadcasted_iota(jnp.int32, (2 * blk, nq), 0)
    qrow = lax.broadcasted_iota(jnp.int32, (2 * blk, nq), 1) % blk
    diff = qrow + blk - keys
    in_window = (diff >= 0) & (diff < WINDOW)
    masks = [in_window & ((keys >= blk) | (n > 0))] + [in_window] * (nsub - 1)
    head_of_col = lax.broadcasted_iota(jnp.int32, (1, nq), 1) // blk
    lane = lax.broadcasted_iota(jnp.int32, (1, LANES), 1)
    log2e = LOG2E
    def mine(kv):
        return (lane >= (kv % 2) * hd) & (lane < (kv % 2 + 1) * hd)

    def window_rows(prev_ref, cur_ref, sub, ps):
        prev = prev_ref[:, ps] if sub == 0 else cur_ref[(sub - 1) * blk:sub * blk, ps]
        return jnp.concatenate([prev, cur_ref[sub * blk:(sub + 1) * blk, ps]], axis=0)

    def scores(item):
        kv, sub = item
        ps = slice((kv // 2) * LANES, (kv // 2 + 1) * LANES)
        kpair = window_rows(kp_ref, kc_ref, sub, ps).astype(BF16)
        parts = []
        for g in range(GQA_GROUP):
            c0 = (kv * GQA_GROUP + g - (g % 2)) * hd
            src = q_ref[sub * blk:(sub + 1) * blk, c0:c0 + LANES]
            if g % 2 != kv % 2:
                src = pltpu.roll(src, hd, 1)
            parts.append(jnp.where(mine(kv), src, 0.0))
        qs = jnp.concatenate(parts, axis=0).astype(BF16)
        return lax.dot_general(kpair, qs, (((1,), (1,)), ((), ())), preferred_element_type=F32)

    def softmax(item, s):
        kv, sub = item
        s = jnp.where(masks[sub], s, -jnp.inf)
        sink = jnp.zeros((1, nq), F32)
        for g in range(GQA_GROUP):
            sink = jnp.where(head_of_col == g, sink_ref[kv * GQA_GROUP + g] * log2e, sink)
        mx = jnp.maximum(jnp.max(s, axis=0, keepdims=True), sink)
        return jnp.exp2(s - mx).astype(BF16), jnp.exp2(sink - mx)

    def weighted_values(item, pb, sink_term):
        kv, sub = item
        ps = slice((kv // 2) * LANES, (kv // 2 + 1) * LANES)
        vpair = window_rows(vp_ref, vc_ref, sub, ps)
        vaug = jnp.where(mine(kv), vpair, 1.0).astype(BF16)
        ot = lax.dot_general(vaug, pb, (((0,), (0,)), ((), ())), preferred_element_type=F32)
        half = kv % 2
        other = (1 - half) * hd
        denom = ot[other:other + 1, :] + sink_term
        on = (ot[half * hd:(half + 1) * hd, :] * (1.0 / denom)).astype(o_ref.dtype)
        for g in range(GQA_GROUP):
            r0 = (kv * GQA_GROUP + g) * hd
            o_ref[r0:r0 + hd, sub * blk:(sub + 1) * blk] = on[:, g * blk:(g + 1) * blk]

    @pl.when(n < pl.num_programs(0) - 1)
    def _():
        items = [(kv, sub) for sub in range(nsub) for kv in range(n_kv)]
        ahead = 2
        pending = {i: scores(items[i]) for i in range(min(ahead, len(items)))}
        for i, item in enumerate(items):
            pb, sink_term = softmax(item, pending.pop(i))
            if i + ahead < len(items):
                pending[i + ahead] = scores(items[i + ahead])
            weighted_values(item, pb, sink_term)

    @pl.when(n == pl.num_programs(0) - 1)
    def _():
        o_ref[...] = jnp.zeros_like(o_ref)


def swa_prompt(qkv, sinks, w_o, layer, t, d):
    n_kv = d // HEAD_DIM // GQA_GROUP
    kvw = n_kv * HEAD_DIM
    assert t % WINDOW == 0
    nsub = 2 if (t // WINDOW) % 2 == 0 else 1
    nblk = t // (nsub * WINDOW)
    assert d % kvw == 0 and n_kv % 2 == 0 and d % nblk == 0 and (d // nblk) % 16 == 0
    wrows = d // nblk
    kblk = d // kvw
    vblk = kblk + 1
    assert 0 < qkv.shape[0] - t <= nsub * WINDOW
    cur = lambda n: jnp.minimum(n, nblk - 1)
    prev = lambda n: jnp.maximum(cur(n) * nsub - 1, 0)
    return pl.pallas_call(
        functools.partial(_swa_prompt_body, n_kv=n_kv, nsub=nsub),
        grid=(nblk + 1,),
        in_specs=[
            pl.BlockSpec(memory_space=pltpu.SMEM),
            pl.BlockSpec((nsub * WINDOW, d), lambda n: (cur(n), 0)),
            pl.BlockSpec((WINDOW, kvw), lambda n: (prev(n), kblk)),
            pl.BlockSpec((nsub * WINDOW, kvw), lambda n: (cur(n), kblk)),
            pl.BlockSpec((WINDOW, kvw), lambda n: (prev(n), vblk)),
            pl.BlockSpec((nsub * WINDOW, kvw), lambda n: (cur(n), vblk)),
            pl.BlockSpec((None, wrows, d), lambda n: (layer, cur(n), 0)),
        ],
        out_specs=[pl.BlockSpec((d, nsub * WINDOW), lambda n: (0, n)),
                   pl.BlockSpec((wrows, d), lambda n: (cur(n), 0))],
        out_shape=[jax.ShapeDtypeStruct((d, qkv.shape[0]), BF16), jax.ShapeDtypeStruct((d, d), BF16)],
        compiler_params=_params(("arbitrary",)),
        name="swa_prompt",
    )(sinks, qkv, qkv, qkv, qkv, qkv, w_o)


def _swa_sample_body(sink_ref, qkvt_ref, k_ref, v_ref, o_any, ko_ref, vo_ref, o_ref, acc_ref, *, n_kv, bt):
    del o_any
    i = pl.program_id(0)
    nk = k_ref.shape[1]
    nb = qkvt_ref.shape[1]
    kvw = n_kv * HEAD_DIM
    n_heads = n_kv * GQA_GROUP
    nq = n_heads * HEAD_DIM
    sub = 8

    @pl.when(i == 0)
    def _():
        acc_ref[...] = jnp.zeros_like(acc_ref)

    key_lane = lax.broadcasted_iota(jnp.int32, (1, nk), 1)
    sublane = lax.broadcasted_iota(jnp.int32, (sub, nk), 0)
    head_row = lax.broadcasted_iota(jnp.int32, (n_heads, 1), 0)
    sink = jnp.zeros((n_heads, 1), F32)
    for hd in range(n_heads):
        sink = jnp.where(head_row == hd, sink_ref[hd] * LOG2E, sink)
    qt = qkvt_ref[:nq, :].astype(BF16)
    knt = qkvt_ref[nq:nq + kvw, :]
    vnt = qkvt_ref[nq + kvw:, :]
    gs = bt
    wide_row = lax.broadcasted_iota(jnp.int32, (nb, gs * nk), 0)
    wide_lane_group = lax.broadcasted_iota(jnp.int32, (nb, gs * nk), 1) // nk
    tall_lane_group = lax.broadcasted_iota(jnp.int32, (gs * nk, nb), 0) // nk
    tall_lane = lax.broadcasted_iota(jnp.int32, (gs * nk, nb), 1)
    qcols_of = [jnp.dot(qt, (wide_row == wide_lane_group + (i * bt + g0)).astype(BF16),
                        preferred_element_type=F32) for g0 in range(0, bt, gs)]
    kfulls, vfulls = [], []
    assert nb == nk
    for j in range(bt):
        to_last = nk - 1 - (i * bt + j)
        kn = pltpu.roll(knt, to_last, 1)
        vn = pltpu.roll(vnt, to_last, 1)
        rows = slice(j * kvw, (j + 1) * kvw)
        kfulls.append(jnp.where(key_lane == nk - 1, kn, pltpu.roll(k_ref[rows, :], nk - 1, 1)))
        vfulls.append(jnp.where(key_lane == nk - 1, vn, pltpu.roll(v_ref[rows, :], nk - 1, 1)))
        ko_ref[rows, :] = kfulls[j]
        vo_ref[rows, :] = vfulls[j]
    all_scores = []
    for j in range(bt):
        qcols = qcols_of[j // gs][:, (j % gs) * nk:(j % gs + 1) * nk]
        kfull = kfulls[j]
        tiles = []
        for tile in range(n_heads // sub):
            st = jnp.zeros((sub, nk), F32)
            for r in range(sub):
                hidx = tile * sub + r
                kv = hidx // GQA_GROUP
                prod = kfull[kv * HEAD_DIM:(kv + 1) * HEAD_DIM] * qcols[hidx * HEAD_DIM:(hidx + 1) * HEAD_DIM]
                red = jnp.sum(prod.reshape(HEAD_DIM // sub, sub, nk), axis=0)
                step = sub // 2
                while step >= 1:
                    red = red + pltpu.roll(red, step, 0)
                    step //= 2
                st = jnp.where(sublane == r, red, st)
            tiles.append(st)
        all_scores.append(jnp.concatenate(tiles, axis=0))
    probs = []
    for s in all_scores:
        mx = jnp.maximum(jnp.max(s, axis=1, keepdims=True), sink)
        p = jnp.exp2(s - mx)
        probs.append(p / (jnp.sum(p, axis=1, keepdims=True) + jnp.exp2(sink - mx)))
    pvs = []
    for j in range(bt):
        pn, vfull = probs[j], vfulls[j]
        pvs.append(jnp.concatenate(
            [vfull[(hidx // GQA_GROUP) * HEAD_DIM:(hidx // GQA_GROUP + 1) * HEAD_DIM] * pn[hidx:hidx + 1, :]
             for hidx in range(n_heads)], axis=0).astype(BF16))
        if len(pvs) == gs:
            first = i * bt + j + 1 - gs
            acc_ref[...] += jnp.dot(jnp.concatenate(pvs, axis=1),
                                    (tall_lane_group + first == tall_lane).astype(BF16),
                                    preferred_element_type=F32)
            pvs = []

    @pl.when(i == pl.num_programs(0) - 1)
    def _():
        o_ref[...] = acc_ref[...].astype(o_ref.dtype)


def swa_sample(qkvt, kt, vt, sinks, o_fm, n_kv):
    nrow, b = qkvt.shape
    kvw = n_kv * HEAD_DIM
    d = nrow - 2 * kvw
    nk = kt.shape[1]
    t = o_fm.shape[1] - b
    assert d == n_kv * GQA_GROUP * HEAD_DIM and (n_kv * GQA_GROUP) % 8 == 0 and t % b == 0
    bt = _pick(b, (8, 4, 2, 1))
    cache_spec = pl.BlockSpec((bt * kvw, nk), lambda i: (i, 0))
    return pl.pallas_call(
        functools.partial(_swa_sample_body, n_kv=n_kv, bt=bt),
        grid=(b // bt,),
        in_specs=[
            pl.BlockSpec(memory_space=pltpu.SMEM),
            pl.BlockSpec((nrow, b), lambda i: (0, 0)),
            cache_spec,
            cache_spec,
            pl.BlockSpec(memory_space=pl.ANY),
        ],
        out_specs=[cache_spec, cache_spec, pl.BlockSpec((d, b), lambda i: (0, t // b))],
        out_shape=[jax.ShapeDtypeStruct(kt.shape, F32), jax.ShapeDtypeStruct(vt.shape, F32),
                   jax.ShapeDtypeStruct(o_fm.shape, o_fm.dtype)],
        scratch_shapes=[pltpu.VMEM((d, b), F32)],
        input_output_aliases={4: 2},
        compiler_params=_params(("arbitrary",)),
        name="swa_sample",
    )(sinks, qkvt, kt, vt, o_fm)


def _pool_rows(x_ref, g_ref, sc_ref, gn_ref, wb_ref, hl_ref, o_ref, hn_ref, gc):
    i = pl.program_id(0)
    bm = x_ref.shape[0]
    x = x_ref[...]
    h = _rms(x, g_ref[...])
    hprev = hl_ref[...]
    hl_ref[...] = h[bm - POOL_MAXW:, :]
    pos1 = (lax.broadcasted_iota(jnp.int32, (bm, 1), 0) + i * bm + 1).astype(F32)
    for gi, w in enumerate(POOL_WINDOWS):
        cs = slice(gi * gc, (gi + 1) * gc)
        hg = h[:, cs]
        acc = jnp.concatenate([hprev[:, cs], hg], axis=0)
        step = 1
        while step < w:
            acc = acc + pltpu.roll(acc, step, 0)
            step *= 2
        inv_cnt = 1.0 / jnp.minimum(jnp.float32(w), pos1)
        pooled = (acc[POOL_MAXW:, :] * inv_cnt - hg).astype(BF16)
        y = jnp.dot(pooled, wb_ref[gi], preferred_element_type=F32)
        o_ref[:, cs] = x[:, cs] + y * sc_ref[:, cs]
    hn_ref[...] = _rms(o_ref[...], gn_ref[...]).astype(hn_ref.dtype)


def _pool_sample_body(x_ref, st_ref, g_ref, w_ref, sc_ref, gn_ref, hn_any, o_ref, hn_ref, ns_ref, *, gc):
    del hn_any
    x = x_ref[...]
    h = _rms(x, g_ref[...])
    ns_ref[:POOL_BUF - 1] = st_ref[1:]
    ns_ref[POOL_BUF - 1] = h
    for gi, w in enumerate(POOL_WINDOWS):
        cs = slice(gi * gc, (gi + 1) * gc)
        hg = h[:, cs]
        tot = hg + jnp.sum(st_ref[POOL_BUF - (w - 1):, :, cs], axis=0)
        cnt = float(min(w, PAST_LEN + 1))
        pooled = (tot / cnt - hg).astype(BF16)
        y = jnp.dot(pooled, w_ref[gi].astype(BF16), preferred_element_type=F32)
        o_ref[:, cs] = x[:, cs] + y * sc_ref[:, cs]
    hn_ref[...] = _rms(o_ref[...], gn_ref[...]).astype(hn_ref.dtype)


def pool_sample(x_all, hn_all, state, layer, g, w_pool, scale, g_next, t):
    d = x_all.shape[1]
    b = state.shape[2]
    _, ng, gc, _ = w_pool.shape
    bt = _pick(b, (32, 16, 8))
    assert t % bt == 0
    off = t // bt
    return pl.pallas_call(
        functools.partial(_pool_sample_body, gc=gc),
        grid=(b // bt,),
        in_specs=[
            pl.BlockSpec((bt, d), lambda i: (off + i, 0)),
            pl.BlockSpec((None, POOL_BUF, bt, d), lambda i: (layer, 0, i, 0)),
            pl.BlockSpec((1, d), lambda i: (0, 0)),
            pl.BlockSpec((None, ng, gc, gc), lambda i: (layer, 0, 0, 0)),
            pl.BlockSpec((1, d), lambda i: (0, 0)),
            pl.BlockSpec((1, d), lambda i: (0, 0)),
            pl.BlockSpec(memory_space=pl.ANY),
        ],
        out_specs=[
            pl.BlockSpec((bt, d), lambda i: (off + i, 0)),
            pl.BlockSpec((bt, d), lambda i: (off + i, 0)),
            pl.BlockSpec((POOL_BUF, bt, d), lambda i: (0, i, 0)),
        ],
        out_shape=[jax.ShapeDtypeStruct(x_all.shape, F32), jax.ShapeDtypeStruct(hn_all.shape, hn_all.dtype),
                   jax.ShapeDtypeStruct((POOL_BUF, b, d), F32)],
        input_output_aliases={0: 0, 6: 1},
        compiler_params=_params(("arbitrary",)),
        name="pool_sample",
    )(x_all, state, g.reshape(1, d), w_pool, scale.reshape(1, d), g_next.reshape(1, d), hn_all)


def _mem_kv_body(m_ref, g_ref, w_ref, o_ref):
    h = _rms(m_ref[...], g_ref[...]).astype(BF16)
    o_ref[...] = jnp.dot(h, w_ref[...].astype(BF16), preferred_element_type=F32)


def mem_kv(mem, g, w_kv, layer):
    m, d = mem.shape
    n = w_kv.shape[2]
    bn = _pick(n, (512, 256, 128))
    return pl.pallas_call(
        _mem_kv_body,
        grid=(n // bn,),
        in_specs=[
            pl.BlockSpec((m, d), lambda j: (0, 0)),
            pl.BlockSpec((1, d), lambda j: (0, 0)),
            pl.BlockSpec((None, d, bn), lambda j: (layer, 0, j)),
        ],
        out_specs=pl.BlockSpec((m, bn), lambda j: (0, j)),
        out_shape=jax.ShapeDtypeStruct((m, n), F32),
        compiler_params=_params(("arbitrary",)),
        name="mem_kv",
    )(mem, g.reshape(1, d), w_kv)


def _xattn_prompt_body(h_ref, wq_ref, kv_ref, wo_ref, x_ref, gn_ref, o_ref, hn_ref, wqb_ref, wob_ref, kvb_ref):
    @pl.when(pl.program_id(0) == 0)
    def _():
        wqb_ref[...] = wq_ref[...].astype(BF16)
        wob_ref[...] = wo_ref[...].astype(BF16)
        kvb_ref[...] = kv_ref[...].astype(BF16)

    _xattn_rows(h_ref, x_ref, gn_ref, wqb_ref, wob_ref, kvb_ref, o_ref, hn_ref)


def _xattn_rows(h_ref, x_ref, gn_ref, wqb_ref, wob_ref, kvb_ref, o_ref, hn_ref):
    xw = X_HEADS * X_HEAD_DIM
    bm = h_ref.shape[0]
    nchunk = 2 if bm % 32 == 0 else 1
    rows = [slice(c * (bm // nchunk), (c + 1) * (bm // nchunk)) for c in range(nchunk)]
    qs = [jnp.dot(h_ref[r, :], wqb_ref[...], preferred_element_type=F32) / math.sqrt(X_HEAD_DIM) for r in rows]
    scores = [[lax.dot_general(kvb_ref[:, hd * X_HEAD_DIM:(hd + 1) * X_HEAD_DIM],
                               q[:, hd * X_HEAD_DIM:(hd + 1) * X_HEAD_DIM].astype(BF16), (((1,), (1,)), ((), ())),
                               preferred_element_type=F32) for hd in range(X_HEADS)] for q in qs]
    attn = []
    for c in range(nchunk):
        outs = []
        for hd in range(X_HEADS):
            s = scores[c][hd]
            p = jnp.exp(s - jnp.max(s, axis=0, keepdims=True))
            l = jnp.sum(p, axis=0, keepdims=True)
            vs = slice(xw + hd * X_HEAD_DIM, xw + (hd + 1) * X_HEAD_DIM)
            outs.append(lax.dot_general(kvb_ref[:, vs], p.astype(BF16), (((0,), (0,)), ((), ())),
                                        preferred_element_type=F32) / l)
        attn.append(jnp.concatenate(outs, axis=0).astype(BF16))
    for c, r in enumerate(rows):
        xn = x_ref[r, :] + lax.dot_general(attn[c], wob_ref[...], (((0,), (0,)), ((), ())),
                                           preferred_element_type=F32)
        o_ref[r, :] = xn
        hn_ref[r, :] = _rms(xn, gn_ref[...]).astype(hn_ref.dtype)


def xattn_prompt(hq_all, w_q, mkv, w_o, layer, x_all, g_next, t):
    d = x_all.shape[1]
    xw = w_q.shape[2]
    nm = mkv.shape[0]
    bm = _pick(t, (512, 256, 128, 64, 32, 16))
    return pl.pallas_call(
        _xattn_prompt_body,
        grid=(t // bm,),
        in_specs=[
            pl.BlockSpec((bm, d), lambda i: (i, 0)),
            pl.BlockSpec((None, d, xw), lambda i: (layer, 0, 0)),
            pl.BlockSpec((nm, 2 * xw), lambda i: (0, 0)),
            pl.BlockSpec((None, xw, d), lambda i: (layer, 0, 0)),
            pl.BlockSpec((bm, d), lambda i: (i, 0)),
            pl.BlockSpec((1, d), lambda i: (0, 0)),
        ],
        out_specs=[pl.BlockSpec((bm, d), lambda i: (i, 0)), pl.BlockSpec((bm, d), lambda i: (i, 0))],
        out_shape=[jax.ShapeDtypeStruct(x_all.shape, F32), jax.ShapeDtypeStruct(x_all.shape, BF16)],
        scratch_shapes=[pltpu.VMEM((d, xw), BF16), pltpu.VMEM((xw, d), BF16), pltpu.VMEM((nm, 2 * xw), BF16)],
        input_output_aliases={4: 0, 0: 1},
        compiler_params=_params(("arbitrary",)),
        name="xattn_prompt",
    )(hq_all, w_q, mkv, w_o, x_all, g_next.reshape(1, d))


def _pool_xattn_body(x_ref, g_ref, wp_ref, sc_ref, gq_ref, wq_ref, kv_ref, wo_ref, gn_ref, hn_any,
                     o_ref, hn_ref, hl_ref, wpb_ref, wqb_ref, wob_ref, kvb_ref, x1_ref, hq_ref, *, gc):
    del hn_any

    @pl.when(pl.program_id(0) == 0)
    def _():
        wpb_ref[...] = wp_ref[...].astype(BF16)
        wqb_ref[...] = wq_ref[...].astype(BF16)
        wob_ref[...] = wo_ref[...].astype(BF16)
        kvb_ref[...] = kv_ref[...].astype(BF16)
        hl_ref[...] = jnp.zeros_like(hl_ref)

    _pool_rows(x_ref, g_ref, sc_ref, gq_ref, wpb_ref, hl_ref, x1_ref, hq_ref, gc)
    _xattn_rows(hq_ref, x1_ref, gn_ref, wqb_ref, wob_ref, kvb_ref, o_ref, hn_ref)


def pool_xattn_prompt(x_all, hn_all, g, w_pool, pool_layer, scale, g_xq, w_q, mkv, w_o, layer, g_next, t):
    d = x_all.shape[1]
    _, ng, gc, _ = w_pool.shape
    xw = w_q.shape[2]
    nm = mkv.shape[0]
    bm = _pick(t, (512, 256, 128, 64, 32, 16))
    row = lambda i: (i, 0)
    vec = pl.BlockSpec((1, d), lambda i: (0, 0))
    once = pl.Buffered(1)
    return pl.pallas_call(
        functools.partial(_pool_xattn_body, gc=gc),
        grid=(t // bm,),
        in_specs=[
            pl.BlockSpec((bm, d), row),
            vec,
            pl.BlockSpec((None, ng, gc, gc), lambda i: (pool_layer, 0, 0, 0), pipeline_mode=once),
            vec,
            vec,
            pl.BlockSpec((None, d, xw), lambda i: (layer, 0, 0), pipeline_mode=once),
            pl.BlockSpec((nm, 2 * xw), lambda i: (0, 0), pipeline_mode=once),
            pl.BlockSpec((None, xw, d), lambda i: (layer, 0, 0), pipeline_mode=once),
            vec,
            pl.BlockSpec(memory_space=pl.ANY),
        ],
        out_specs=[pl.BlockSpec((bm, d), row), pl.BlockSpec((bm, d), row),
                   pl.BlockSpec((POOL_MAXW, d), lambda i: (0, 0))],
        out_shape=[jax.ShapeDtypeStruct(x_all.shape, F32), jax.ShapeDtypeStruct(hn_all.shape, hn_all.dtype),
                   jax.ShapeDtypeStruct((POOL_MAXW, d), F32)],
        scratch_shapes=[pltpu.VMEM((ng, gc, gc), BF16), pltpu.VMEM((d, xw), BF16), pltpu.VMEM((xw, d), BF16),
                        pltpu.VMEM((nm, 2 * xw), BF16), pltpu.VMEM((bm, d), F32), pltpu.VMEM((bm, d), BF16)],
        input_output_aliases={0: 0, 9: 1},
        compiler_params=_params(("arbitrary",)),
        name="pool_xattn_prompt",
    )(x_all, g.reshape(1, d), w_pool, scale.reshape(1, d), g_xq.reshape(1, d), w_q, mkv, w_o,
      g_next.reshape(1, d), hn_all)


def _xq_sample_body(h_ref, wq_ref, o_ref):
    o_ref[...] = jnp.dot(h_ref[...], wq_ref[...].astype(BF16), preferred_element_type=F32) / math.sqrt(X_HEAD_DIM)


def xq_sample(hq_all, w_q, layer, t, b):
    d = hq_all.shape[1]
    xw = w_q.shape[2]
    assert t % b == 0
    return pl.pallas_call(
        _xq_sample_body,
        grid=(1,),
        in_specs=[
            pl.BlockSpec((b, d), lambda i: (t // b, 0)),
            pl.BlockSpec((None, d, xw), lambda i: (layer, 0, 0)),
        ],
        out_specs=pl.BlockSpec((b, xw), lambda i: (0, 0)),
        out_shape=jax.ShapeDtypeStruct((b, xw), F32),
        compiler_params=_params(("arbitrary",)),
        name="xq_sample",
    )(hq_all, w_q)


def _xattn_core_body(q_ref, k_ref, v_ref, ones_ref, o_ref):
    bt, sub, hd = q_ref.shape
    nv = k_ref.shape[0] // (bt * sub)
    k = k_ref[...].reshape(bt, nv, sub, hd)
    prod = (k * q_ref[...][:, None]).reshape(bt * nv * sub, hd).astype(BF16)
    s = jnp.dot(prod, ones_ref[...], preferred_element_type=F32).reshape(bt, nv, sub, hd)

    def fold(x, op):
        step = X_HEADS
        while step < sub:
            x = op(x, pltpu.roll(x, step, 2))
            step *= 2
        return x

    mx = fold(jnp.max(s, axis=1, keepdims=True), jnp.maximum)
    p = jnp.exp(s - mx)
    l = fold(jnp.sum(p, axis=1, keepdims=True), jnp.add)
    acc = fold(jnp.sum(p * v_ref[...].reshape(bt, nv, sub, hd), axis=1, keepdims=True), jnp.add)
    o_ref[...] = (acc / l).reshape(bt, sub, hd)


def xattn_core_sample(q_s, mk_flat, mv_flat, layer, nm):
    b = q_s.shape[0]
    hd = X_HEAD_DIM
    sub = 8
    assert sub % X_HEADS == 0 and nm % (sub // X_HEADS) == 0
    bt = _pick(b, (8, 4, 2, 1))
    nblk = b // bt
    rows = bt * nm * X_HEADS
    q8 = jnp.tile(q_s.reshape(b, X_HEADS, hd), (1, sub // X_HEADS, 1))
    o8 = pl.pallas_call(
        _xattn_core_body,
        grid=(nblk,),
        in_specs=[
            pl.BlockSpec((bt, sub, hd), lambda i: (i, 0, 0)),
            pl.BlockSpec((rows, hd), lambda i: (layer * nblk + i, 0)),
            pl.BlockSpec((rows, hd), lambda i: (layer * nblk + i, 0)),
            pl.BlockSpec((hd, hd), lambda i: (0, 0)),
        ],
        out_specs=pl.BlockSpec((bt, sub, hd), lambda i: (i, 0, 0)),
        out_shape=jax.ShapeDtypeStruct((b, sub, hd), F32),
        compiler_params=_params(("arbitrary",)),
        name="xattn_core_sample",
    )(q8, mk_flat, mv_flat, jnp.ones((hd, hd), BF16))
    return o8[:, :X_HEADS].reshape(b, X_HEADS * hd)


def _xout_sample_body(a_ref, wo_ref, x_ref, gn_ref, hn_any, o_ref, hn_ref):
    del hn_any
    xn = x_ref[...] + jnp.dot(a_ref[...].astype(BF16), wo_ref[...].astype(BF16), preferred_element_type=F32)
    o_ref[...] = xn
    hn_ref[...] = _rms(xn, gn_ref[...]).astype(hn_ref.dtype)


def xout_sample(o_s, w_o, layer, x_all, hn_all, g_next, t):
    d = x_all.shape[1]
    b, xw = o_s.shape
    assert t % b == 0
    off = t // b
    return pl.pallas_call(
        _xout_sample_body,
        grid=(1,),
        in_specs=[
            pl.BlockSpec((b, xw), lambda i: (0, 0)),
            pl.BlockSpec((None, xw, d), lambda i: (layer, 0, 0)),
            pl.BlockSpec((b, d), lambda i: (off, 0)),
            pl.BlockSpec((1, d), lambda i: (0, 0)),
            pl.BlockSpec(memory_space=pl.ANY),
        ],
        out_specs=[pl.BlockSpec((b, d), lambda i: (off, 0)), pl.BlockSpec((b, d), lambda i: (off, 0))],
        out_shape=[jax.ShapeDtypeStruct(x_all.shape, F32), jax.ShapeDtypeStruct(hn_all.shape, hn_all.dtype)],
        input_output_aliases={2: 0, 4: 1},
        compiler_params=_params(("arbitrary",)),
        name="xout_sample",
    )(o_s, w_o, x_all, g_next.reshape(1, d), hn_all)


def kernel(x_prompt, x_sample, state_pool, cache_swa_k, cache_swa_v, cache_mem_k, cache_mem_v, mem_prompt,
           g_ffn1, w_ffn1_gu, w_ffn1_dn, g_mix, w_pool, pool_scale, w_qkv, w_o, sinks,
           g_xq, g_mem, w_xq, w_xkv, w_xo, g_ffn2, w_ffn2_gu, w_ffn2_dn, g_final):
    bp, t, d = x_prompt.shape
    b, s_len, _ = x_sample.shape
    assert bp == 1 and s_len == 1
    depth = g_ffn1.shape[0]
    xw = w_xq.shape[2]
    nm = mem_prompt.shape[1]
    n_kv = d // HEAD_DIM // GQA_GROUP
    kvw = n_kv * HEAD_DIM

    h = stacked_norm(x_prompt[0], x_sample[:, 0], g_ffn1[0])
    x = None
    cos_t, sin_t = rope_tables(t, b)
    mk_flat = cache_mem_k.reshape(-1, X_HEAD_DIM)
    mv_flat = cache_mem_v.reshape(-1, X_HEAD_DIM)
    state_sm = jnp.swapaxes(state_pool, 1, 2)

    pool_p, pool_s = [], []
    swa_kp, swa_vp, swa_ks, swa_vs = [], [], [], []
    mem_kp, mem_vp = [], []
    y_p = y_s = None
    for layer in range(depth):
        i = layer // 2
        act, wdb = gate_up(h, w_ffn1_gu, w_ffn1_dn, layer)
        if x is None:
            x, h = down_norm_stacking(act, wdb, x_prompt[0], x_sample[:, 0], g_mix[layer], BF16)
        else:
            x, h = down_norm(act, wdb, x, g_mix[layer], BF16)
        mkv = mem_kv(mem_prompt[0], g_mem[layer], w_xkv, layer)
        mem_kp.append(mkv[:, :xw].reshape(1, nm, X_HEADS, X_HEAD_DIM))
        mem_vp.append(mkv[:, xw:].reshape(1, nm, X_HEADS, X_HEAD_DIM))
        if layer % 2 == 0:
            x, hq, new_state = pool_sample(x, h, state_sm, i, g_mix[layer], w_pool, pool_scale[i], g_xq[layer], t)
            q_s = xq_sample(hq, w_xq, layer, t, b)
            x, h, h_last = pool_xattn_prompt(x, hq, g_mix[layer], w_pool, i, pool_scale[i], g_xq[layer],
                                             w_xq, mkv, w_xo, layer, g_ffn2[layer], t)
            pool_p.append(h_last[None, POOL_MAXW - POOL_BUF:])
            pool_s.append(jnp.swapaxes(new_state, 0, 1))
        else:
            qkv = qkv_rope(h, w_qkv, i, cos_t, sin_t, d + kvw, kvw)
            buf = cache_swa_k.shape[2]
            keep = min(WINDOW, t)
            swa_kp.append(qkv[t - keep:t, d:d + kvw].reshape(1, keep, n_kv, HEAD_DIM))
            swa_vp.append(qkv[t - keep:t, d + kvw:].reshape(1, keep, n_kv, HEAD_DIM))
            assert buf == WINDOW
            to_fm = lambda c: jnp.transpose(c, (0, 2, 3, 1)).reshape(b * kvw, buf)
            from_fm = lambda c: jnp.transpose(c.reshape(b, n_kv, HEAD_DIM, buf), (0, 3, 1, 2))
            o_fm, wo_bf16 = swa_prompt(qkv, sinks[i], w_o, i, t, d)
            ks_fm, vs_fm, o_fm = swa_sample(qkv[t:].T, to_fm(cache_swa_k[i]), to_fm(cache_swa_v[i]), sinks[i],
                                            o_fm, n_kv)
            swa_ks.append(from_fm(ks_fm))
            swa_vs.append(from_fm(vs_fm))
            x, hq = proj_residual_norm(o_fm, wo_bf16, x, g_xq[layer])
            q_s = xq_sample(hq, w_xq, layer, t, b)
            x, h = xattn_prompt(hq, w_xq, mkv, w_xo, layer, x, g_ffn2[layer], t)
        o_s = xattn_core_sample(q_s, mk_flat, mv_flat, layer, nm)
        x, h = xout_sample(o_s, w_xo, layer, x, h, g_ffn2[layer], t)
        act, wdb = gate_up(h, w_ffn2_gu, w_ffn2_dn, layer)
        if layer + 1 < depth:
            x, h = down_norm(act, wdb, x, g_ffn1[layer + 1], BF16)
        else:
            (y_p,) = down_norm(act, wdb, x, g_final, F32, emit_x=False, row0=0, nrows=t)
            (y_s,) = down_norm(act, wdb, x, g_final, F32, emit_x=False, row0=t, nrows=b)
    return (y_p[None], y_s[:, None], jnp.stack(pool_p), jnp.stack(pool_s), jnp.stack(swa_kp), jnp.stack(swa_vp),
            jnp.stack(swa_ks), jnp.stack(swa_vs), jnp.stack(mem_kp), jnp.stack(mem_vp))
```

```python
import functools
import math

import jax
import jax.numpy as jnp
from jax import lax
from jax.experimental import pallas as pl
from jax.experimental.pallas import tpu as pltpu

F32 = jnp.float32
BF16 = jnp.bfloat16

RMS_EPS = 1e-6
PAST_LEN = 8192
POOL_WINDOWS = (2, 4, 8, 16)
POOL_MAXW = max(POOL_WINDOWS)
POOL_BUF = POOL_MAXW - 1
HEAD_DIM = 64
GQA_GROUP = 4
WINDOW = 128
ROPE_THETA = 10000.0
X_HEADS = 4
X_HEAD_DIM = 128
LANES = 128
LOG2E = math.log2(math.e)
Q_SCALE = LOG2E / math.sqrt(HEAD_DIM)
V7X_VMEM_LIMIT = 60 * 1024 * 1024


def _params(sem):
    return pltpu.CompilerParams(dimension_semantics=sem, vmem_limit_bytes=V7X_VMEM_LIMIT)


def _pick(n, candidates):
    for c in candidates:
        if n % c == 0:
            return c
    raise ValueError(f"no block size in {candidates} divides {n}")


def _rms(x, g):
    return x * lax.rsqrt(jnp.mean(x * x, axis=-1, keepdims=True) + RMS_EPS) * g


def _stack_norm_body(xp_ref, xs_ref, g_ref, h_ref):
    nt = pl.num_programs(0) - 1
    b = xs_ref.shape[0]

    @pl.when(pl.program_id(0) < nt)
    def _():
        h_ref[...] = _rms(xp_ref[...], g_ref[...]).astype(h_ref.dtype)

    @pl.when(pl.program_id(0) == nt)
    def _():
        h_ref[:b, :] = _rms(xs_ref[...], g_ref[...]).astype(h_ref.dtype)


def stacked_norm(x_p, x_s, g):
    t, d = x_p.shape
    b = x_s.shape[0]
    bm = _pick(t, (512, 256, 128, 64, 32, 16))
    nt = t // bm
    assert b <= bm and b % 16 == 0
    return pl.pallas_call(
        _stack_norm_body,
        grid=(nt + 1,),
        in_specs=[pl.BlockSpec((bm, d), lambda i: (jnp.minimum(i, nt - 1), 0)),
                  pl.BlockSpec((b, d), lambda i: (0, 0)),
                  pl.BlockSpec((1, d), lambda i: (0, 0))],
        out_specs=pl.BlockSpec((bm, d), lambda i: (i, 0)),
        out_shape=jax.ShapeDtypeStruct((t + b, d), BF16),
        compiler_params=_params(("arbitrary",)),
        name="stack_norm",
    )(x_p, x_s, g.reshape(1, d))


def _gate_up_body(h_ref, wg_ref, wu_ref, wd_ref, o_ref, wdb_ref, wb_ref, *, bf):
    @pl.when(pl.program_id(1) == 0)
    def _():
        wb_ref[:, :bf] = wg_ref[...].astype(BF16)
        wb_ref[:, bf:] = wu_ref[...].astype(BF16)
        wdb_ref[...] = wd_ref[...].astype(BF16)

    bm = h_ref.shape[0]
    nchunk = 8 if bm % 128 == 0 else (4 if bm % 64 == 0 else 1)
    cr = bm // nchunk
    rs = [jnp.dot(h_ref[c * cr:(c + 1) * cr, :], wb_ref[...], preferred_element_type=F32) for c in range(nchunk)]
    for c in range(nchunk):
        a = rs[c][:, :bf]
        b = rs[c][:, bf:]
        o_ref[c * cr:(c + 1) * cr, :] = (a / (1.0 + jnp.exp(-a)) * b * 0.5).astype(o_ref.dtype)


def gate_up(h, w_gu, w_dn, layer):
    m, d = h.shape
    f = w_gu.shape[2] // 2
    bf = _pick(f, (512, 256, 128))
    bm = _pick(m, (1664, 1024, 512, 256, 128, 64, 32, 16))
    nf = f // bf
    return pl.pallas_call(
        functools.partial(_gate_up_body, bf=bf),
        grid=(nf, m // bm),
        in_specs=[
            pl.BlockSpec((bm, d), lambda j, i: (i, 0)),
            pl.BlockSpec((None, d, bf), lambda j, i: (layer, 0, j)),
            pl.BlockSpec((None, d, bf), lambda j, i: (layer, 0, j + nf)),
            pl.BlockSpec((None, bf, d), lambda j, i: (layer, j, 0)),
        ],
        out_specs=[
            pl.BlockSpec((bm, bf), lambda j, i: (i, j)),
            pl.BlockSpec((bf, d), lambda j, i: (j, 0)),
        ],
        out_shape=[jax.ShapeDtypeStruct((m, f), BF16), jax.ShapeDtypeStruct((f, d), BF16)],
        scratch_shapes=[pltpu.VMEM((d, 2 * bf), BF16)],
        compiler_params=_params(("arbitrary", "arbitrary")),
        name="gate_up",
    )(h, w_gu, w_gu, w_dn)


def _down_body(a_ref, w_ref, x_ref, g_ref, *out_refs, emit_x):
    xn = x_ref[...] + jnp.dot(a_ref[...], w_ref[...], preferred_element_type=F32)
    if emit_x:
        out_refs[0][...] = xn
    h_ref = out_refs[-1]
    h_ref[...] = _rms(xn, g_ref[...]).astype(h_ref.dtype)


def down_norm(a, w_bf16, x, g_next, h_dtype, emit_x=True, row0=0, nrows=None):
    m, k = a.shape
    d = w_bf16.shape[1]
    nrows = m if nrows is None else nrows
    bm = _pick(math.gcd(nrows, row0) if row0 else nrows, (416, 320, 256, 128, 64, 32, 16))
    off = row0 // bm
    row = lambda i: (i, 0)
    src = lambda i: (i + off, 0)
    out_specs = [pl.BlockSpec((bm, d), row)]
    out_shape = [jax.ShapeDtypeStruct((nrows, d), h_dtype)]
    if emit_x:
        out_specs.insert(0, pl.BlockSpec((bm, d), src))
        out_shape.insert(0, jax.ShapeDtypeStruct((m, d), F32))
    return pl.pallas_call(
        functools.partial(_down_body, emit_x=emit_x),
        grid=(nrows // bm,),
        in_specs=[
            pl.BlockSpec((bm, k), src),
            pl.BlockSpec((k, d), lambda i: (0, 0), pipeline_mode=pl.Buffered(1)),
            pl.BlockSpec((bm, d), src),
            pl.BlockSpec((1, d), lambda i: (0, 0)),
        ],
        out_specs=out_specs,
        out_shape=out_shape,
        input_output_aliases={2: 0} if emit_x else {},
        compiler_params=_params(("arbitrary",)),
        name="down_norm",
    )(a, w_bf16, x, g_next.reshape(1, d))


def _down_first_body(a_ref, w_ref, xp_ref, xs_ref, g_ref, xo_ref, h_ref, *, split):
    bm = a_ref.shape[0]
    last = pl.num_programs(0) - 1

    def finish(rows, x):
        xn = x + jnp.dot(a_ref[rows, :], w_ref[...], preferred_element_type=F32)
        xo_ref[rows, :] = xn
        h_ref[rows, :] = _rms(xn, g_ref[...]).astype(h_ref.dtype)

    @pl.when(pl.program_id(0) < last)
    def _():
        finish(slice(0, bm), xp_ref[...])

    @pl.when(pl.program_id(0) == last)
    def _():
        finish(slice(0, split), xp_ref[:split, :])
        finish(slice(split, bm), xs_ref[...])


def down_norm_stacking(a, w_bf16, x_p, x_s, g_next, h_dtype):
    m, k = a.shape
    d = w_bf16.shape[1]
    t, b = x_p.shape[0], x_s.shape[0]
    bm = _pick(m, (416, 320, 256, 128, 64, 32, 16))
    nt = m // bm
    split = t - (nt - 1) * bm
    assert m == t + b and bm - split == b and split > 0 and split % 16 == 0
    row = lambda i: (i, 0)
    return pl.pallas_call(
        functools.partial(_down_first_body, split=split),
        grid=(nt,),
        in_specs=[
            pl.BlockSpec((bm, k), row),
            pl.BlockSpec((k, d), lambda i: (0, 0), pipeline_mode=pl.Buffered(1)),
            pl.BlockSpec((bm, d), row),
            pl.BlockSpec((b, d), lambda i: (0, 0)),
            pl.BlockSpec((1, d), lambda i: (0, 0)),
        ],
        out_specs=[pl.BlockSpec((bm, d), row), pl.BlockSpec((bm, d), row)],
        out_shape=[jax.ShapeDtypeStruct((m, d), F32), jax.ShapeDtypeStruct((m, d), h_dtype)],
        compiler_params=_params(("arbitrary",)),
        name="down_norm_stacking",
    )(a, w_bf16, x_p, x_s, g_next.reshape(1, d))


def _proj_res_body(a_ref, w_ref, x_ref, g_ref, xo_ref, h_ref):
    xn = x_ref[...] + lax.dot_general(a_ref[...], w_ref[...], (((0,), (0,)), ((), ())),
                                      preferred_element_type=F32)
    xo_ref[...] = xn
    h_ref[...] = _rms(xn, g_ref[...]).astype(h_ref.dtype)


def proj_residual_norm(a_fm, w_bf16, x, g_next):
    k, m = a_fm.shape
    d = w_bf16.shape[1]
    bm = _pick(m, (640, 512, 256, 128))
    row = lambda i: (i, 0)
    return pl.pallas_call(
        _proj_res_body,
        grid=(m // bm,),
        in_specs=[
            pl.BlockSpec((k, bm), lambda i: (0, i)),
            pl.BlockSpec((k, d), lambda i: (0, 0), pipeline_mode=pl.Buffered(1)),
            pl.BlockSpec((bm, d), row),
            pl.BlockSpec((1, d), lambda i: (0, 0)),
        ],
        out_specs=[pl.BlockSpec((bm, d), row), pl.BlockSpec((bm, d), row)],
        out_shape=[jax.ShapeDtypeStruct((m, d), F32), jax.ShapeDtypeStruct((m, d), BF16)],
        input_output_aliases={2: 0},
        compiler_params=_params(("arbitrary",)),
        name="proj_residual_norm",
    )(a_fm, w_bf16, x, g_next.reshape(1, d))


def _qkv_body(h_ref, w_ref, c_ref, s_ref, o_ref, wb_ref, *, n_q_tiles, n_rope_tiles):
    @pl.when(pl.program_id(1) == 0)
    def _():
        wb_ref[...] = w_ref[...].astype(BF16)

    bm, bn = o_ref.shape
    is_rope = pl.program_id(0) < n_rope_tiles
    scale = jnp.where(pl.program_id(0) < n_q_tiles, Q_SCALE, 1.0).astype(F32)
    nchunk = 8 if bm % 128 == 0 else (4 if bm % 64 == 0 else 1)
    cr = bm // nchunk
    rs = [jnp.dot(h_ref[c * cr:(c + 1) * cr, :], wb_ref[...], preferred_element_type=F32) for c in range(nchunk)]
    lane = lax.broadcasted_iota(jnp.int32, (cr, LANES), 1)
    first_half = (lane % HEAD_DIM) < (HEAD_DIM // 2)
    for c in range(nchunk):
        rows = slice(c * cr, (c + 1) * cr)
        cos = c_ref[rows, :]
        sin = s_ref[rows, :]
        for ci in range(bn // LANES):
            blk = rs[c][:, ci * LANES:(ci + 1) * LANES] * scale
            partner = jnp.where(first_half,
                                pltpu.roll(blk, LANES - HEAD_DIM // 2, 1),
                                pltpu.roll(blk, HEAD_DIM // 2, 1))
            o_ref[rows, ci * LANES:(ci + 1) * LANES] = jnp.where(is_rope, blk * cos + partner * sin, blk)


def qkv_rope(h, w_qkv, layer, cos_t, sin_t, n_rope_cols, bn):
    m, d = h.shape
    n = w_qkv.shape[2]
    assert n % bn == 0 and n_rope_cols % bn == 0 and bn % LANES == 0
    bm = _pick(m, (1664, 1024, 512, 256, 128, 64, 32, 16))
    return pl.pallas_call(
        functools.partial(_qkv_body, n_q_tiles=d // bn, n_rope_tiles=n_rope_cols // bn),
        grid=(n // bn, m // bm),
        in_specs=[
            pl.BlockSpec((bm, d), lambda j, i: (i, 0)),
            pl.BlockSpec((None, d, bn), lambda j, i: (layer, 0, j)),
            pl.BlockSpec((bm, LANES), lambda j, i: (i, 0)),
            pl.BlockSpec((bm, LANES), lambda j, i: (i, 0)),
        ],
        out_specs=pl.BlockSpec((bm, bn), lambda j, i: (i, j)),
        out_shape=jax.ShapeDtypeStruct((m, n), F32),
        scratch_shapes=[pltpu.VMEM((d, bn), BF16)],
        compiler_params=_params(("arbitrary", "arbitrary")),
        name="qkv_rope",
    )(h, w_qkv, cos_t, sin_t)


def rope_tables(t_prompt, n_sample):
    half = HEAD_DIM // 2
    inv = ROPE_THETA ** (-jnp.arange(half, dtype=F32) / half)
    pos = jnp.concatenate([jnp.arange(t_prompt), jnp.full((n_sample,), PAST_LEN)]).astype(F32)
    ang = pos[:, None] * inv[None, :]
    cos, sin = jnp.cos(ang), jnp.sin(ang)
    reps = LANES // HEAD_DIM
    return (jnp.tile(jnp.concatenate([cos, cos], axis=1), (1, reps)),
            jnp.tile(jnp.concatenate([-sin, sin], axis=1), (1, reps)))


def _swa_prompt_body(sink_ref, q_ref, kp_ref, kc_ref, vp_ref, vc_ref, wo_ref, o_ref, wob_ref, *, n_kv, nsub):
    n = pl.program_id(0)
    blk = WINDOW
    hd = HEAD_DIM
    assert LANES == 2 * hd
    wob_ref[...] = wo_ref[...].astype(BF16)
    nq = GQA_GROUP * blk
    keys = lax.broadcasted_iota(jnp.int32, (2 * blk, nq), 0)
    qrow = lax.broadcasted_iota(jnp.int32, (2 * blk, nq), 1) % blk
    diff = qrow + blk - keys
    in_window = (diff >= 0) & (diff < WINDOW)
    masks = [in_window & ((keys >= blk) | (n > 0))] + [in_window] * (nsub - 1)
    head_of_col = lax.broadcasted_iota(jnp.int32, (1, nq), 1) // blk
    lane = lax.broadcasted_iota(jnp.int32, (1, LANES), 1)
    log2e = LOG2E
    def mine(kv):
        return (lane >= (kv % 2) * hd) & (lane < (kv % 2 + 1) * hd)

    def window_rows(prev_ref, cur_ref, sub, ps):
        prev = prev_ref[:, ps] if sub == 0 else cur_ref[(sub - 1) * blk:sub * blk, ps]
        return jnp.concatenate([prev, cur_ref[sub * blk:(sub + 1) * blk, ps]], axis=0)

    def scores(item):
        kv, sub = item
        ps = slice((kv // 2) * LANES, (kv // 2 + 1) * LANES)
        kpair = window_rows(kp_ref, kc_ref, sub, ps).astype(BF16)
        parts = []
        for g in range(GQA_GROUP):
            c0 = (kv * GQA_GROUP + g - (g % 2)) * hd
            src = q_ref[sub * blk:(sub + 1) * blk, c0:c0 + LANES]
            if g % 2 != kv % 2:
                src = pltpu.roll(src, hd, 1)
            parts.append(jnp.where(mine(kv), src, 0.0))
        qs = jnp.concatenate(parts, axis=0).astype(BF16)
        return lax.dot_general(kpair, qs, (((1,), (1,)), ((), ())), preferred_element_type=F32)

    def softmax(item, s):
        kv, sub = item
        s = jnp.where(masks[sub], s, -jnp.inf)
        sink = jnp.zeros((1, nq), F32)
        for g in range(GQA_GROUP):
            sink = jnp.where(head_of_col == g, sink_ref[kv * GQA_GROUP + g] * log2e, sink)
        mx = jnp.maximum(jnp.max(s, axis=0, keepdims=True), sink)
        return jnp.exp2(s - mx).astype(BF16), jnp.exp2(sink - mx)

    def weighted_values(item, pb, sink_term):
        kv, sub = item
        ps = slice((kv // 2) * LANES, (kv // 2 + 1) * LANES)
        vpair = window_rows(vp_ref, vc_ref, sub, ps)
        vaug = jnp.where(mine(kv), vpair, 1.0).astype(BF16)
        ot = lax.dot_general(vaug, pb, (((0,), (0,)), ((), ())), preferred_element_type=F32)
        half = kv % 2
        other = (1 - half) * hd
        denom = ot[other:other + 1, :] + sink_term
        on = (ot[half * hd:(half + 1) * hd, :] * (1.0 / denom)).astype(o_ref.dtype)
        for g in range(GQA_GROUP):
            r0 = (kv * GQA_GROUP + g) * hd
            o_ref[r0:r0 + hd, sub * blk:(sub + 1) * blk] = on[:, g * blk:(g + 1) * blk]

    @pl.when(n < pl.num_programs(0) - 1)
    def _():
        items = [(kv, sub) for sub in range(nsub) for kv in range(n_kv)]
        ahead = 2
        pending = {i: scores(items[i]) for i in range(min(ahead, len(items)))}
        for i, item in enumerate(items):
            pb, sink_term = softmax(item, pending.pop(i))
            if i + ahead < len(items):
                pending[i + ahead] = scores(items[i + ahead])
            weighted_values(item, pb, sink_term)

    @pl.when(n == pl.num_programs(0) - 1)
    def _():
        o_ref[...] = jnp.zeros_like(o_ref)


def swa_prompt(qkv, sinks, w_o, layer, t, d):
    n_kv = d // HEAD_DIM // GQA_GROUP
    kvw = n_kv * HEAD_DIM
    assert t % WINDOW == 0
    nsub = 2 if (t // WINDOW) % 2 == 0 else 1
    nblk = t // (nsub * WINDOW)
    assert d % kvw == 0 and n_kv % 2 == 0 and d % nblk == 0 and (d // nblk) % 16 == 0
    wrows = d // nblk
    kblk = d // kvw
    vblk = kblk + 1
    assert 0 < qkv.shape[0] - t <= nsub * WINDOW
    cur = lambda n: jnp.minimum(n, nblk - 1)
    prev = lambda n: jnp.maximum(cur(n) * nsub - 1, 0)
    return pl.pallas_call(
        functools.partial(_swa_prompt_body, n_kv=n_kv, nsub=nsub),
        grid=(nblk + 1,),
        in_specs=[
            pl.BlockSpec(memory_space=pltpu.SMEM),
            pl.BlockSpec((nsub * WINDOW, d), lambda n: (cur(n), 0)),
            pl.BlockSpec((WINDOW, kvw), lambda n: (prev(n), kblk)),
            pl.BlockSpec((nsub * WINDOW, kvw), lambda n: (cur(n), kblk)),
            pl.BlockSpec((WINDOW, kvw), lambda n: (prev(n), vblk)),
            pl.BlockSpec((nsub * WINDOW, kvw), lambda n: (cur(n), vblk)),
            pl.BlockSpec((None, wrows, d), lambda n: (layer, cur(n), 0)),
        ],
        out_specs=[pl.BlockSpec((d, nsub * WINDOW), lambda n: (0, n)),
                   pl.BlockSpec((wrows, d), lambda n: (cur(n), 0))],
        out_shape=[jax.ShapeDtypeStruct((d, qkv.shape[0]), BF16), jax.ShapeDtypeStruct((d, d), BF16)],
        compiler_params=_params(("arbitrary",)),
        name="swa_prompt",
    )(sinks, qkv, qkv, qkv, qkv, qkv, w_o)


def _swa_sample_body(sink_ref, qkvt_ref, k_ref, v_ref, o_any, ko_ref, vo_ref, o_ref, acc_ref, *, n_kv, bt):
    del o_any
    i = pl.program_id(0)
    nk = k_ref.shape[1]
    nb = qkvt_ref.shape[1]
    kvw = n_kv * HEAD_DIM
    n_heads = n_kv * GQA_GROUP
    nq = n_heads * HEAD_DIM
    sub = 8

    @pl.when(i == 0)
    def _():
        acc_ref[...] = jnp.zeros_like(acc_ref)

    key_lane = lax.broadcasted_iota(jnp.int32, (1, nk), 1)
    sublane = lax.broadcasted_iota(jnp.int32, (sub, nk), 0)
    head_row = lax.broadcasted_iota(jnp.int32, (n_heads, 1), 0)
    sink = jnp.zeros((n_heads, 1), F32)
    for hd in range(n_heads):
        sink = jnp.where(head_row == hd, sink_ref[hd] * LOG2E, sink)
    qt = qkvt_ref[:nq, :].astype(BF16)
    knt = qkvt_ref[nq:nq + kvw, :]
    vnt = qkvt_ref[nq + kvw:, :]
    gs = bt
    wide_row = lax.broadcasted_iota(jnp.int32, (nb, gs * nk), 0)
    wide_lane_group = lax.broadcasted_iota(jnp.int32, (nb, gs * nk), 1) // nk
    tall_lane_group = lax.broadcasted_iota(jnp.int32, (gs * nk, nb), 0) // nk
    tall_lane = lax.broadcasted_iota(jnp.int32, (gs * nk, nb), 1)
    qcols_of = [jnp.dot(qt, (wide_row == wide_lane_group + (i * bt + g0)).astype(BF16),
                        preferred_element_type=F32) for g0 in range(0, bt, gs)]
    kfulls, vfulls = [], []
    assert nb == nk
    for j in range(bt):
        to_last = nk - 1 - (i * bt + j)
        kn = pltpu.roll(knt, to_last, 1)
        vn = pltpu.roll(vnt, to_last, 1)
        rows = slice(j * kvw, (j + 1) * kvw)
        kfulls.append(jnp.where(key_lane == nk - 1, kn, pltpu.roll(k_ref[rows, :], nk - 1, 1)))
        vfulls.append(jnp.where(key_lane == nk - 1, vn, pltpu.roll(v_ref[rows, :], nk - 1, 1)))
        ko_ref[rows, :] = kfulls[j]
        vo_ref[rows, :] = vfulls[j]
    all_scores = []
    for j in range(bt):
        qcols = qcols_of[j // gs][:, (j % gs) * nk:(j % gs + 1) * nk]
        kfull = kfulls[j]
        tiles = []
        for tile in range(n_heads // sub):
            st = jnp.zeros((sub, nk), F32)
            for r in range(sub):
                hidx = tile * sub + r
                kv = hidx // GQA_GROUP
                prod = kfull[kv * HEAD_DIM:(kv + 1) * HEAD_DIM] * qcols[hidx * HEAD_DIM:(hidx + 1) * HEAD_DIM]
                red = jnp.sum(prod.reshape(HEAD_DIM // sub, sub, nk), axis=0)
                step = sub // 2
                while step >= 1:
                    red = red + pltpu.roll(red, step, 0)
                    step //= 2
                st = jnp.where(sublane == r, red, st)
            tiles.append(st)
        all_scores.append(jnp.concatenate(tiles, axis=0))
    probs = []
    for s in all_scores:
        mx = jnp.maximum(jnp.max(s, axis=1, keepdims=True), sink)
        p = jnp.exp2(s - mx)
        probs.append(p / (jnp.sum(p, axis=1, keepdims=True) + jnp.exp2(sink - mx)))
    pvs = []
    for j in range(bt):
        pn, vfull = probs[j], vfulls[j]
        pvs.append(jnp.concatenate(
            [vfull[(hidx // GQA_GROUP) * HEAD_DIM:(hidx // GQA_GROUP + 1) * HEAD_DIM] * pn[hidx:hidx + 1, :]
             for hidx in range(n_heads)], axis=0).astype(BF16))
        if len(pvs) == gs:
            first = i * bt + j + 1 - gs
            acc_ref[...] += jnp.dot(jnp.concatenate(pvs, axis=1),
                                    (tall_lane_group + first == tall_lane).astype(BF16),
                                    preferred_element_type=F32)
            pvs = []

    @pl.when(i == pl.num_programs(0) - 1)
    def _():
        o_ref[...] = acc_ref[...].astype(o_ref.dtype)


def swa_sample(qkvt, kt, vt, sinks, o_fm, n_kv):
    nrow, b = qkvt.shape
    kvw = n_kv * HEAD_DIM
    d = nrow - 2 * kvw
    nk = kt.shape[1]
    t = o_fm.shape[1] - b
    assert d == n_kv * GQA_GROUP * HEAD_DIM and (n_kv * GQA_GROUP) % 8 == 0 and t % b == 0
    bt = _pick(b, (8, 4, 2, 1))
    cache_spec = pl.BlockSpec((bt * kvw, nk), lambda i: (i, 0))
    return pl.pallas_call(
        functools.partial(_swa_sample_body, n_kv=n_kv, bt=bt),
        grid=(b // bt,),
        in_specs=[
            pl.BlockSpec(memory_space=pltpu.SMEM),
            pl.BlockSpec((nrow, b), lambda i: (0, 0)),
            cache_spec,
            cache_spec,
            pl.BlockSpec(memory_space=pl.ANY),
        ],
        out_specs=[cache_spec, cache_spec, pl.BlockSpec((d, b), lambda i: (0, t // b))],
        out_shape=[jax.ShapeDtypeStruct(kt.shape, F32), jax.ShapeDtypeStruct(vt.shape, F32),
                   jax.ShapeDtypeStruct(o_fm.shape, o_fm.dtype)],
        scratch_shapes=[pltpu.VMEM((d, b), F32)],
        input_output_aliases={4: 2},
        compiler_params=_params(("arbitrary",)),
        name="swa_sample",
    )(sinks, qkvt, kt, vt, o_fm)


def _pool_rows(x_ref, g_ref, sc_ref, gn_ref, wb_ref, o_ref, hn_ref, gc, rows, pos0, hprev):
    x = x_ref[rows, :]
    n = x.shape[0]
    h = _rms(x, g_ref[...])
    pos1 = (lax.broadcasted_iota(jnp.int32, (n, 1), 0) + (pos0 + 1)).astype(F32)
    for gi, w in enumerate(POOL_WINDOWS):
        cs = slice(gi * gc, (gi + 1) * gc)
        hg = h[:, cs]
        acc = jnp.concatenate([hprev[:, cs], hg], axis=0)
        step = 1
        while step < w:
            acc = acc + pltpu.roll(acc, step, 0)
            step *= 2
        inv_cnt = 1.0 / jnp.minimum(jnp.float32(w), pos1)
        pooled = (acc[POOL_MAXW:, :] * inv_cnt - hg).astype(BF16)
        y = jnp.dot(pooled, wb_ref[gi], preferred_element_type=F32)
        o_ref[rows, cs] = x[:, cs] + y * sc_ref[:, cs]
    hn_ref[rows, :] = _rms(o_ref[rows, :], gn_ref[...]).astype(hn_ref.dtype)
    return h[n - POOL_MAXW:, :]


def _pool_sample_body(x_ref, st_ref, g_ref, w_ref, sc_ref, gn_ref, hn_any, o_ref, hn_ref, ns_ref, *, gc):
    del hn_any
    x = x_ref[...]
    h = _rms(x, g_ref[...])
    ns_ref[:POOL_BUF - 1] = st_ref[1:]
    ns_ref[POOL_BUF - 1] = h
    for gi, w in enumerate(POOL_WINDOWS):
        cs = slice(gi * gc, (gi + 1) * gc)
        hg = h[:, cs]
        tot = hg + jnp.sum(st_ref[POOL_BUF - (w - 1):, :, cs], axis=0)
        cnt = float(min(w, PAST_LEN + 1))
        pooled = (tot / cnt - hg).astype(BF16)
        y = jnp.dot(pooled, w_ref[gi].astype(BF16), preferred_element_type=F32)
        o_ref[:, cs] = x[:, cs] + y * sc_ref[:, cs]
    hn_ref[...] = _rms(o_ref[...], gn_ref[...]).astype(hn_ref.dtype)


def pool_sample(x_all, hn_all, state, layer, g, w_pool, scale, g_next, t):
    d = x_all.shape[1]
    b = state.shape[2]
    _, ng, gc, _ = w_pool.shape
    bt = _pick(b, (32, 16, 8))
    assert t % bt == 0
    off = t // bt
    return pl.pallas_call(
        functools.partial(_pool_sample_body, gc=gc),
        grid=(b // bt,),
        in_specs=[
            pl.BlockSpec((bt, d), lambda i: (off + i, 0)),
            pl.BlockSpec((None, POOL_BUF, bt, d), lambda i: (layer, 0, i, 0)),
            pl.BlockSpec((1, d), lambda i: (0, 0)),
            pl.BlockSpec((None, ng, gc, gc), lambda i: (layer, 0, 0, 0)),
            pl.BlockSpec((1, d), lambda i: (0, 0)),
            pl.BlockSpec((1, d), lambda i: (0, 0)),
            pl.BlockSpec(memory_space=pl.ANY),
        ],
        out_specs=[
            pl.BlockSpec((bt, d), lambda i: (off + i, 0)),
            pl.BlockSpec((bt, d), lambda i: (off + i, 0)),
            pl.BlockSpec((POOL_BUF, bt, d), lambda i: (0, i, 0)),
        ],
        out_shape=[jax.ShapeDtypeStruct(x_all.shape, F32), jax.ShapeDtypeStruct(hn_all.shape, hn_all.dtype),
                   jax.ShapeDtypeStruct((POOL_BUF, b, d), F32)],
        input_output_aliases={0: 0, 6: 1},
        compiler_params=_params(("arbitrary",)),
        name="pool_sample",
    )(x_all, state, g.reshape(1, d), w_pool, scale.reshape(1, d), g_next.reshape(1, d), hn_all)


def _mem_kv_body(m_ref, g_ref, w_ref, o_ref):
    h = _rms(m_ref[...], g_ref[...]).astype(BF16)
    o_ref[...] = jnp.dot(h, w_ref[...].astype(BF16), preferred_element_type=F32)


def mem_kv(mem, g, w_kv, layer):
    m, d = mem.shape
    n = w_kv.shape[2]
    bn = _pick(n, (512, 256, 128))
    return pl.pallas_call(
        _mem_kv_body,
        grid=(n // bn,),
        in_specs=[
            pl.BlockSpec((m, d), lambda j: (0, 0)),
            pl.BlockSpec((1, d), lambda j: (0, 0)),
            pl.BlockSpec((None, d, bn), lambda j: (layer, 0, j)),
        ],
        out_specs=pl.BlockSpec((m, bn), lambda j: (0, j)),
        out_shape=jax.ShapeDtypeStruct((m, n), F32),
        compiler_params=_params(("arbitrary",)),
        name="mem_kv",
    )(mem, g.reshape(1, d), w_kv)


def _xattn_prompt_body(h_ref, wq_ref, kv_ref, wo_ref, x_ref, gn_ref, o_ref, hn_ref, wqb_ref, wob_ref, kvb_ref):
    @pl.when(pl.program_id(0) == 0)
    def _():
        wqb_ref[...] = wq_ref[...].astype(BF16)
        wob_ref[...] = wo_ref[...].astype(BF16)
        kvb_ref[...] = kv_ref[...].astype(BF16)

    _xattn_rows(h_ref, x_ref, gn_ref, wqb_ref, wob_ref, kvb_ref, o_ref, hn_ref)


def _xattn_rows(h_ref, x_ref, gn_ref, wqb_ref, wob_ref, kvb_ref, o_ref, hn_ref):
    xw = X_HEADS * X_HEAD_DIM
    bm = h_ref.shape[0]
    nchunk = 2 if bm % 32 == 0 else 1
    rows = [slice(c * (bm // nchunk), (c + 1) * (bm // nchunk)) for c in range(nchunk)]
    qs = [jnp.dot(h_ref[r, :], wqb_ref[...], preferred_element_type=F32) / math.sqrt(X_HEAD_DIM) for r in rows]
    scores = [[lax.dot_general(kvb_ref[:, hd * X_HEAD_DIM:(hd + 1) * X_HEAD_DIM],
                               q[:, hd * X_HEAD_DIM:(hd + 1) * X_HEAD_DIM].astype(BF16), (((1,), (1,)), ((), ())),
                               preferred_element_type=F32) for hd in range(X_HEADS)] for q in qs]
    attn = []
    for c in range(nchunk):
        outs = []
        for hd in range(X_HEADS):
            s = scores[c][hd]
            p = jnp.exp(s - jnp.max(s, axis=0, keepdims=True))
            l = jnp.sum(p, axis=0, keepdims=True)
            vs = slice(xw + hd * X_HEAD_DIM, xw + (hd + 1) * X_HEAD_DIM)
            outs.append(lax.dot_general(kvb_ref[:, vs], p.astype(BF16), (((0,), (0,)), ((), ())),
                                        preferred_element_type=F32) / l)
        attn.append(jnp.concatenate(outs, axis=0).astype(BF16))
    for c, r in enumerate(rows):
        xn = x_ref[r, :] + lax.dot_general(attn[c], wob_ref[...], (((0,), (0,)), ((), ())),
                                           preferred_element_type=F32)
        o_ref[r, :] = xn
        hn_ref[r, :] = _rms(xn, gn_ref[...]).astype(hn_ref.dtype)


def xattn_prompt(hq_all, w_q, mkv, w_o, layer, x_all, g_next, t):
    d = x_all.shape[1]
    xw = w_q.shape[2]
    nm = mkv.shape[0]
    bm = _pick(t, (512, 256, 128, 64, 32, 16))
    return pl.pallas_call(
        _xattn_prompt_body,
        grid=(t // bm,),
        in_specs=[
            pl.BlockSpec((bm, d), lambda i: (i, 0)),
            pl.BlockSpec((None, d, xw), lambda i: (layer, 0, 0)),
            pl.BlockSpec((nm, 2 * xw), lambda i: (0, 0)),
            pl.BlockSpec((None, xw, d), lambda i: (layer, 0, 0)),
            pl.BlockSpec((bm, d), lambda i: (i, 0)),
            pl.BlockSpec((1, d), lambda i: (0, 0)),
        ],
        out_specs=[pl.BlockSpec((bm, d), lambda i: (i, 0)), pl.BlockSpec((bm, d), lambda i: (i, 0))],
        out_shape=[jax.ShapeDtypeStruct(x_all.shape, F32), jax.ShapeDtypeStruct(x_all.shape, BF16)],
        scratch_shapes=[pltpu.VMEM((d, xw), BF16), pltpu.VMEM((xw, d), BF16), pltpu.VMEM((nm, 2 * xw), BF16)],
        input_output_aliases={4: 0, 0: 1},
        compiler_params=_params(("arbitrary",)),
        name="xattn_prompt",
    )(hq_all, w_q, mkv, w_o, x_all, g_next.reshape(1, d))


def _pool_xattn_body(x_ref, g_ref, wp_ref, sc_ref, gq_ref, wq_ref, kv_ref, wo_ref, gn_ref, hn_any,
                     o_ref, hn_ref, hl_ref, wpb_ref, wqb_ref, wob_ref, kvb_ref, x1_ref, hq_ref, *, gc):
    del hn_any

    @pl.when(pl.program_id(0) == 0)
    def _():
        wpb_ref[...] = wp_ref[...].astype(BF16)
        wqb_ref[...] = wq_ref[...].astype(BF16)
        wob_ref[...] = wo_ref[...].astype(BF16)
        kvb_ref[...] = kv_ref[...].astype(BF16)
        hl_ref[...] = jnp.zeros_like(hl_ref)

    bm = x_ref.shape[0]
    hl_ref[...] = _pool_rows(x_ref, g_ref, sc_ref, gq_ref, wpb_ref, x1_ref, hq_ref, gc,
                             slice(0, bm), pl.program_id(0) * bm, hl_ref[...])
    _xattn_rows(hq_ref, x1_ref, gn_ref, wqb_ref, wob_ref, kvb_ref, o_ref, hn_ref)


def pool_xattn_prompt(x_all, hn_all, g, w_pool, pool_layer, scale, g_xq, w_q, mkv, w_o, layer, g_next, t):
    d = x_all.shape[1]
    _, ng, gc, _ = w_pool.shape
    xw = w_q.shape[2]
    nm = mkv.shape[0]
    bm = _pick(t, (512, 256, 128, 64, 32, 16))
    row = lambda i: (i, 0)
    vec = pl.BlockSpec((1, d), lambda i: (0, 0))
    once = pl.Buffered(1)
    return pl.pallas_call(
        functools.partial(_pool_xattn_body, gc=gc),
        grid=(t // bm,),
        in_specs=[
            pl.BlockSpec((bm, d), row),
            vec,
            pl.BlockSpec((None, ng, gc, gc), lambda i: (pool_layer, 0, 0, 0), pipeline_mode=once),
            vec,
            vec,
            pl.BlockSpec((None, d, xw), lambda i: (layer, 0, 0), pipeline_mode=once),
            pl.BlockSpec((nm, 2 * xw), lambda i: (0, 0), pipeline_mode=once),
            pl.BlockSpec((None, xw, d), lambda i: (layer, 0, 0), pipeline_mode=once),
            vec,
            pl.BlockSpec(memory_space=pl.ANY),
        ],
        out_specs=[pl.BlockSpec((bm, d), row), pl.BlockSpec((bm, d), row),
                   pl.BlockSpec((POOL_MAXW, d), lambda i: (0, 0))],
        out_shape=[jax.ShapeDtypeStruct(x_all.shape, F32), jax.ShapeDtypeStruct(hn_all.shape, hn_all.dtype),
                   jax.ShapeDtypeStruct((POOL_MAXW, d), F32)],
        scratch_shapes=[pltpu.VMEM((ng, gc, gc), BF16), pltpu.VMEM((d, xw), BF16), pltpu.VMEM((xw, d), BF16),
                        pltpu.VMEM((nm, 2 * xw), BF16), pltpu.VMEM((bm, d), F32), pltpu.VMEM((bm, d), BF16)],
        input_output_aliases={0: 0, 9: 1},
        compiler_params=_params(("arbitrary",)),
        name="pool_xattn_prompt",
    )(x_all, g.reshape(1, d), w_pool, scale.reshape(1, d), g_xq.reshape(1, d), w_q, mkv, w_o,
      g_next.reshape(1, d), hn_all)


def _xq_sample_body(h_ref, wq_ref, o_ref):
    o_ref[...] = jnp.dot(h_ref[...], wq_ref[...].astype(BF16), preferred_element_type=F32) / math.sqrt(X_HEAD_DIM)


def xq_sample(hq_all, w_q, layer, t, b):
    d = hq_all.shape[1]
    xw = w_q.shape[2]
    assert t % b == 0
    return pl.pallas_call(
        _xq_sample_body,
        grid=(1,),
        in_specs=[
            pl.BlockSpec((b, d), lambda i: (t // b, 0)),
            pl.BlockSpec((None, d, xw), lambda i: (layer, 0, 0)),
        ],
        out_specs=pl.BlockSpec((b, xw), lambda i: (0, 0)),
        out_shape=jax.ShapeDtypeStruct((b, xw), F32),
        compiler_params=_params(("arbitrary",)),
        name="xq_sample",
    )(hq_all, w_q)


def _xattn_core_body(q_ref, k_ref, v_ref, ones_ref, o_ref):
    bt, sub, hd = q_ref.shape
    nv = k_ref.shape[0] // (bt * sub)
    k = k_ref[...].reshape(bt, nv, sub, hd)
    prod = (k * q_ref[...][:, None]).reshape(bt * nv * sub, hd).astype(BF16)
    s = jnp.dot(prod, ones_ref[...], preferred_element_type=F32).reshape(bt, nv, sub, hd)

    def fold(x, op):
        step = X_HEADS
        while step < sub:
            x = op(x, pltpu.roll(x, step, 2))
            step *= 2
        return x

    mx = fold(jnp.max(s, axis=1, keepdims=True), jnp.maximum)
    p = jnp.exp(s - mx)
    l = fold(jnp.sum(p, axis=1, keepdims=True), jnp.add)
    acc = fold(jnp.sum(p * v_ref[...].reshape(bt, nv, sub, hd), axis=1, keepdims=True), jnp.add)
    o_ref[...] = (acc / l).reshape(bt, sub, hd)


def xattn_core_sample(q_s, mk_flat, mv_flat, layer, nm):
    b = q_s.shape[0]
    hd = X_HEAD_DIM
    sub = 8
    assert sub % X_HEADS == 0 and nm % (sub // X_HEADS) == 0
    bt = _pick(b, (8, 4, 2, 1))
    nblk = b // bt
    rows = bt * nm * X_HEADS
    q8 = jnp.tile(q_s.reshape(b, X_HEADS, hd), (1, sub // X_HEADS, 1))
    o8 = pl.pallas_call(
        _xattn_core_body,
        grid=(nblk,),
        in_specs=[
            pl.BlockSpec((bt, sub, hd), lambda i: (i, 0, 0)),
            pl.BlockSpec((rows, hd), lambda i: (layer * nblk + i, 0)),
            pl.BlockSpec((rows, hd), lambda i: (layer * nblk + i, 0)),
            pl.BlockSpec((hd, hd), lambda i: (0, 0)),
        ],
        out_specs=pl.BlockSpec((bt, sub, hd), lambda i: (i, 0, 0)),
        out_shape=jax.ShapeDtypeStruct((b, sub, hd), F32),
        compiler_params=_params(("arbitrary",)),
        name="xattn_core_sample",
    )(q8, mk_flat, mv_flat, jnp.ones((hd, hd), BF16))
    return o8[:, :X_HEADS].reshape(b, X_HEADS * hd)


def _xout_sample_body(a_ref, wo_ref, x_ref, gn_ref, hn_any, o_ref, hn_ref):
    del hn_any
    xn = x_ref[...] + jnp.dot(a_ref[...].astype(BF16), wo_ref[...].astype(BF16), preferred_element_type=F32)
    o_ref[...] = xn
    hn_ref[...] = _rms(xn, gn_ref[...]).astype(hn_ref.dtype)


def xout_sample(o_s, w_o, layer, x_all, hn_all, g_next, t):
    d = x_all.shape[1]
    b, xw = o_s.shape
    assert t % b == 0
    off = t // b
    return pl.pallas_call(
        _xout_sample_body,
        grid=(1,),
        in_specs=[
            pl.BlockSpec((b, xw), lambda i: (0, 0)),
            pl.BlockSpec((None, xw, d), lambda i: (layer, 0, 0)),
            pl.BlockSpec((b, d), lambda i: (off, 0)),
            pl.BlockSpec((1, d), lambda i: (0, 0)),
            pl.BlockSpec(memory_space=pl.ANY),
        ],
        out_specs=[pl.BlockSpec((b, d), lambda i: (off, 0)), pl.BlockSpec((b, d), lambda i: (off, 0))],
        out_shape=[jax.ShapeDtypeStruct(x_all.shape, F32), jax.ShapeDtypeStruct(hn_all.shape, hn_all.dtype)],
        input_output_aliases={2: 0, 4: 1},
        compiler_params=_params(("arbitrary",)),
        name="xout_sample",
    )(o_s, w_o, x_all, g_next.reshape(1, d), hn_all)


def kernel(x_prompt, x_sample, state_pool, cache_swa_k, cache_swa_v, cache_mem_k, cache_mem_v, mem_prompt,
           g_ffn1, w_ffn1_gu, w_ffn1_dn, g_mix, w_pool, pool_scale, w_qkv, w_o, sinks,
           g_xq, g_mem, w_xq, w_xkv, w_xo, g_ffn2, w_ffn2_gu, w_ffn2_dn, g_final):
    bp, t, d = x_prompt.shape
    b, s_len, _ = x_sample.shape
    assert bp == 1 and s_len == 1
    depth = g_ffn1.shape[0]
    xw = w_xq.shape[2]
    nm = mem_prompt.shape[1]
    n_kv = d // HEAD_DIM // GQA_GROUP
    kvw = n_kv * HEAD_DIM

    h = stacked_norm(x_prompt[0], x_sample[:, 0], g_ffn1[0])
    x = None
    cos_t, sin_t = rope_tables(t, b)
    mk_flat = cache_mem_k.reshape(-1, X_HEAD_DIM)
    mv_flat = cache_mem_v.reshape(-1, X_HEAD_DIM)
    state_sm = jnp.swapaxes(state_pool, 1, 2)

    pool_p, pool_s = [], []
    swa_kp, swa_vp, swa_ks, swa_vs = [], [], [], []
    mem_kp, mem_vp = [], []
    y_p = y_s = None
    for layer in range(depth):
        i = layer // 2
        act, wdb = gate_up(h, w_ffn1_gu, w_ffn1_dn, layer)
        if x is None:
            x, h = down_norm_stacking(act, wdb, x_prompt[0], x_sample[:, 0], g_mix[layer], BF16)
        else:
            x, h = down_norm(act, wdb, x, g_mix[layer], BF16)
        mkv = mem_kv(mem_prompt[0], g_mem[layer], w_xkv, layer)
        mem_kp.append(mkv[:, :xw].reshape(1, nm, X_HEADS, X_HEAD_DIM))
        mem_vp.append(mkv[:, xw:].reshape(1, nm, X_HEADS, X_HEAD_DIM))
        if layer % 2 == 0:
            x, hq, new_state = pool_sample(x, h, state_sm, i, g_mix[layer], w_pool, pool_scale[i], g_xq[layer], t)
            q_s = xq_sample(hq, w_xq, layer, t, b)
            x, h, h_last = pool_xattn_prompt(x, hq, g_mix[layer], w_pool, i, pool_scale[i], g_xq[layer],
                                             w_xq, mkv, w_xo, layer, g_ffn2[layer], t)
            pool_p.append(h_last[None, POOL_MAXW - POOL_BUF:])
            pool_s.append(jnp.swapaxes(new_state, 0, 1))
        else:
            qkv = qkv_rope(h, w_qkv, i, cos_t, sin_t, d + kvw, kvw)
            buf = cache_swa_k.shape[2]
            keep = min(WINDOW, t)
            swa_kp.append(qkv[t - keep:t, d:d + kvw].reshape(1, keep, n_kv, HEAD_DIM))
            swa_vp.append(qkv[t - keep:t, d + kvw:].reshape(1, keep, n_kv, HEAD_DIM))
            assert buf == WINDOW
            to_fm = lambda c: jnp.transpose(c, (0, 2, 3, 1)).reshape(b * kvw, buf)
            from_fm = lambda c: jnp.transpose(c.reshape(b, n_kv, HEAD_DIM, buf), (0, 3, 1, 2))
            o_fm, wo_bf16 = swa_prompt(qkv, sinks[i], w_o, i, t, d)
            ks_fm, vs_fm, o_fm = swa_sample(qkv[t:].T, to_fm(cache_swa_k[i]), to_fm(cache_swa_v[i]), sinks[i],
                                            o_fm, n_kv)
            swa_ks.append(from_fm(ks_fm))
            swa_vs.append(from_fm(vs_fm))
            x, hq = proj_residual_norm(o_fm, wo_bf16, x, g_xq[layer])
            q_s = xq_sample(hq, w_xq, layer, t, b)
            x, h = xattn_prompt(hq, w_xq, mkv, w_xo, layer, x, g_ffn2[layer], t)
        o_s = xattn_core_sample(q_s, mk_flat, mv_flat, layer, nm)
        x, h = xout_sample(o_s, w_xo, layer, x, h, g_ffn2[layer], t)
        act, wdb = gate_up(h, w_ffn2_gu, w_ffn2_dn, layer)
        if layer + 1 < depth:
            x, h = down_norm(act, wdb, x, g_ffn1[layer + 1], BF16)
        else:
            (y_p,) = down_norm(act, wdb, x, g_final, F32, emit_x=False, row0=0, nrows=t)
            (y_s,) = down_norm(act, wdb, x, g_final, F32, emit_x=False, row0=t, nrows=b)
    return (y_p[None], y_s[:, None], jnp.stack(pool_p), jnp.stack(pool_s), jnp.stack(swa_kp), jnp.stack(swa_vp),
            jnp.stack(swa_ks), jnp.stack(swa_vs), jnp.stack(mem_kp), jnp.stack(mem_vp))
```

```python
import functools
import math

import jax
import jax.numpy as jnp
from jax import lax
from jax.experimental import pallas as pl
from jax.experimental.pallas import tpu as pltpu

F32 = jnp.float32
BF16 = jnp.bfloat16

RMS_EPS = 1e-6
PAST_LEN = 8192
POOL_WINDOWS = (2, 4, 8, 16)
POOL_MAXW = max(POOL_WINDOWS)
POOL_BUF = POOL_MAXW - 1
HEAD_DIM = 64
GQA_GROUP = 4
WINDOW = 128
ROPE_THETA = 10000.0
X_HEADS = 4
X_HEAD_DIM = 128
LANES = 128
V7X_VMEM_LIMIT = 60 * 1024 * 1024


def _params(sem):
    return pltpu.CompilerParams(dimension_semantics=sem, vmem_limit_bytes=V7X_VMEM_LIMIT)


def _pick(n, candidates):
    for c in candidates:
        if n % c == 0:
            return c
    raise ValueError(f"no block size in {candidates} divides {n}")


def _rms(x, g):
    return x * lax.rsqrt(jnp.mean(x * x, axis=-1, keepdims=True) + RMS_EPS) * g


def _stack_norm_body(xp_ref, xs_ref, g_ref, h_ref):
    nt = pl.num_programs(0) - 1
    b = xs_ref.shape[0]

    @pl.when(pl.program_id(0) < nt)
    def _():
        h_ref[...] = _rms(xp_ref[...], g_ref[...]).astype(h_ref.dtype)

    @pl.when(pl.program_id(0) == nt)
    def _():
        h_ref[:b, :] = _rms(xs_ref[...], g_ref[...]).astype(h_ref.dtype)


def stacked_norm(x_p, x_s, g):
    t, d = x_p.shape
    b = x_s.shape[0]
    bm = _pick(t, (512, 256, 128, 64, 32, 16))
    nt = t // bm
    assert b <= bm and b % 16 == 0
    return pl.pallas_call(
        _stack_norm_body,
        grid=(nt + 1,),
        in_specs=[pl.BlockSpec((bm, d), lambda i: (jnp.minimum(i, nt - 1), 0)),
                  pl.BlockSpec((b, d), lambda i: (0, 0)),
                  pl.BlockSpec((1, d), lambda i: (0, 0))],
        out_specs=pl.BlockSpec((bm, d), lambda i: (i, 0)),
        out_shape=jax.ShapeDtypeStruct((t + b, d), BF16),
        compiler_params=_params(("arbitrary",)),
        name="stack_norm",
    )(x_p, x_s, g.reshape(1, d))


def _gate_up_body(h_ref, wg_ref, wu_ref, wd_ref, o_ref, wdb_ref, wb_ref, *, bf):
    @pl.when(pl.program_id(1) == 0)
    def _():
        wb_ref[:, :bf] = wg_ref[...].astype(BF16)
        wb_ref[:, bf:] = wu_ref[...].astype(BF16)
        wdb_ref[...] = wd_ref[...].astype(BF16)

    bm = h_ref.shape[0]
    nchunk = 8 if bm % 128 == 0 else (4 if bm % 64 == 0 else 1)
    cr = bm // nchunk
    rs = [jnp.dot(h_ref[c * cr:(c + 1) * cr, :], wb_ref[...], preferred_element_type=F32) for c in range(nchunk)]
    for c in range(nchunk):
        a = rs[c][:, :bf]
        b = rs[c][:, bf:]
        o_ref[c * cr:(c + 1) * cr, :] = (a / (1.0 + jnp.exp(-a)) * b * 0.5).astype(o_ref.dtype)


def gate_up(h, w_gu, w_dn, layer):
    m, d = h.shape
    f = w_gu.shape[2] // 2
    bf = _pick(f, (512, 256, 128))
    bm = _pick(m, (1664, 1024, 512, 256, 128, 64, 32, 16))
    nf = f // bf
    return pl.pallas_call(
        functools.partial(_gate_up_body, bf=bf),
        grid=(nf, m // bm),
        in_specs=[
            pl.BlockSpec((bm, d), lambda j, i: (i, 0)),
            pl.BlockSpec((None, d, bf), lambda j, i: (layer, 0, j)),
            pl.BlockSpec((None, d, bf), lambda j, i: (layer, 0, j + nf)),
            pl.BlockSpec((None, bf, d), lambda j, i: (layer, j, 0)),
        ],
        out_specs=[
            pl.BlockSpec((bm, bf), lambda j, i: (i, j)),
            pl.BlockSpec((bf, d), lambda j, i: (j, 0)),
        ],
        out_shape=[jax.ShapeDtypeStruct((m, f), BF16), jax.ShapeDtypeStruct((f, d), BF16)],
        scratch_shapes=[pltpu.VMEM((d, 2 * bf), BF16)],
        compiler_params=_params(("arbitrary", "arbitrary")),
        name="gate_up",
    )(h, w_gu, w_gu, w_dn)


def _down_body(a_ref, w_ref, x_ref, g_ref, *out_refs, emit_x):
    xn = x_ref[...] + jnp.dot(a_ref[...], w_ref[...], preferred_element_type=F32)
    if emit_x:
        out_refs[0][...] = xn
    h_ref = out_refs[-1]
    h_ref[...] = _rms(xn, g_ref[...]).astype(h_ref.dtype)


def down_norm(a, w_bf16, x, g_next, h_dtype, emit_x=True, row0=0, nrows=None):
    m, k = a.shape
    d = w_bf16.shape[1]
    nrows = m if nrows is None else nrows
    bm = _pick(math.gcd(nrows, row0) if row0 else nrows, (416, 320, 256, 128, 64, 32, 16))
    off = row0 // bm
    row = lambda i: (i, 0)
    src = lambda i: (i + off, 0)
    out_specs = [pl.BlockSpec((bm, d), row)]
    out_shape = [jax.ShapeDtypeStruct((nrows, d), h_dtype)]
    if emit_x:
        out_specs.insert(0, pl.BlockSpec((bm, d), src))
        out_shape.insert(0, jax.ShapeDtypeStruct((m, d), F32))
    return pl.pallas_call(
        functools.partial(_down_body, emit_x=emit_x),
        grid=(nrows // bm,),
        in_specs=[
            pl.BlockSpec((bm, k), src),
            pl.BlockSpec((k, d), lambda i: (0, 0), pipeline_mode=pl.Buffered(1)),
            pl.BlockSpec((bm, d), src),
            pl.BlockSpec((1, d), lambda i: (0, 0)),
        ],
        out_specs=out_specs,
        out_shape=out_shape,
        input_output_aliases={2: 0} if emit_x else {},
        compiler_params=_params(("arbitrary",)),
        name="down_norm",
    )(a, w_bf16, x, g_next.reshape(1, d))


def _down_first_body(a_ref, w_ref, xp_ref, xs_ref, g_ref, xo_ref, h_ref, *, split):
    bm = a_ref.shape[0]
    last = pl.num_programs(0) - 1

    def finish(rows, x):
        xn = x + jnp.dot(a_ref[rows, :], w_ref[...], preferred_element_type=F32)
        xo_ref[rows, :] = xn
        h_ref[rows, :] = _rms(xn, g_ref[...]).astype(h_ref.dtype)

    @pl.when(pl.program_id(0) < last)
    def _():
        finish(slice(0, bm), xp_ref[...])

    @pl.when(pl.program_id(0) == last)
    def _():
        finish(slice(0, split), xp_ref[:split, :])
        finish(slice(split, bm), xs_ref[...])


def down_norm_stacking(a, w_bf16, x_p, x_s, g_next, h_dtype):
    m, k = a.shape
    d = w_bf16.shape[1]
    t, b = x_p.shape[0], x_s.shape[0]
    bm = _pick(m, (416, 320, 256, 128, 64, 32, 16))
    nt = m // bm
    split = t - (nt - 1) * bm
    assert m == t + b and bm - split == b and split > 0 and split % 16 == 0
    row = lambda i: (i, 0)
    return pl.pallas_call(
        functools.partial(_down_first_body, split=split),
        grid=(nt,),
        in_specs=[
            pl.BlockSpec((bm, k), row),
            pl.BlockSpec((k, d), lambda i: (0, 0), pipeline_mode=pl.Buffered(1)),
            pl.BlockSpec((bm, d), row),
            pl.BlockSpec((b, d), lambda i: (0, 0)),
            pl.BlockSpec((1, d), lambda i: (0, 0)),
        ],
        out_specs=[pl.BlockSpec((bm, d), row), pl.BlockSpec((bm, d), row)],
        out_shape=[jax.ShapeDtypeStruct((m, d), F32), jax.ShapeDtypeStruct((m, d), h_dtype)],
        compiler_params=_params(("arbitrary",)),
        name="down_norm_stacking",
    )(a, w_bf16, x_p, x_s, g_next.reshape(1, d))


def _proj_res_body(a_ref, w_ref, x_ref, g_ref, xo_ref, h_ref):
    xn = x_ref[...] + lax.dot_general(a_ref[...], w_ref[...], (((0,), (0,)), ((), ())),
                                      preferred_element_type=F32)
    xo_ref[...] = xn
    h_ref[...] = _rms(xn, g_ref[...]).astype(h_ref.dtype)


def proj_residual_norm(a_fm, w_bf16, x, g_next):
    k, m = a_fm.shape
    d = w_bf16.shape[1]
    bm = _pick(m, (640, 512, 256, 128))
    row = lambda i: (i, 0)
    return pl.pallas_call(
        _proj_res_body,
        grid=(m // bm,),
        in_specs=[
            pl.BlockSpec((k, bm), lambda i: (0, i)),
            pl.BlockSpec((k, d), lambda i: (0, 0), pipeline_mode=pl.Buffered(1)),
            pl.BlockSpec((bm, d), row),
            pl.BlockSpec((1, d), lambda i: (0, 0)),
        ],
        out_specs=[pl.BlockSpec((bm, d), row), pl.BlockSpec((bm, d), row)],
        out_shape=[jax.ShapeDtypeStruct((m, d), F32), jax.ShapeDtypeStruct((m, d), BF16)],
        input_output_aliases={2: 0},
        compiler_params=_params(("arbitrary",)),
        name="proj_residual_norm",
    )(a_fm, w_bf16, x, g_next.reshape(1, d))


def _qkv_body(h_ref, w_ref, c_ref, s_ref, o_ref, wb_ref, *, n_rope_tiles):
    @pl.when(pl.program_id(1) == 0)
    def _():
        wb_ref[...] = w_ref[...].astype(BF16)

    bm, bn = o_ref.shape
    is_rope = pl.program_id(0) < n_rope_tiles
    nchunk = 8 if bm % 128 == 0 else (4 if bm % 64 == 0 else 1)
    cr = bm // nchunk
    rs = [jnp.dot(h_ref[c * cr:(c + 1) * cr, :], wb_ref[...], preferred_element_type=F32) for c in range(nchunk)]
    lane = lax.broadcasted_iota(jnp.int32, (cr, LANES), 1)
    first_half = (lane % HEAD_DIM) < (HEAD_DIM // 2)
    for c in range(nchunk):
        rows = slice(c * cr, (c + 1) * cr)
        cos = c_ref[rows, :]
        sin = s_ref[rows, :]
        for ci in range(bn // LANES):
            blk = rs[c][:, ci * LANES:(ci + 1) * LANES]
            partner = jnp.where(first_half,
                                pltpu.roll(blk, LANES - HEAD_DIM // 2, 1),
                                pltpu.roll(blk, HEAD_DIM // 2, 1))
            o_ref[rows, ci * LANES:(ci + 1) * LANES] = jnp.where(is_rope, blk * cos + partner * sin, blk)


def qkv_rope(h, w_qkv, layer, cos_t, sin_t, n_rope_cols, bn):
    m, d = h.shape
    n = w_qkv.shape[2]
    assert n % bn == 0 and n_rope_cols % bn == 0 and bn % LANES == 0
    bm = _pick(m, (1664, 1024, 512, 256, 128, 64, 32, 16))
    return pl.pallas_call(
        functools.partial(_qkv_body, n_rope_tiles=n_rope_cols // bn),
        grid=(n // bn, m // bm),
        in_specs=[
            pl.BlockSpec((bm, d), lambda j, i: (i, 0)),
            pl.BlockSpec((None, d, bn), lambda j, i: (layer, 0, j)),
            pl.BlockSpec((bm, LANES), lambda j, i: (i, 0)),
            pl.BlockSpec((bm, LANES), lambda j, i: (i, 0)),
        ],
        out_specs=pl.BlockSpec((bm, bn), lambda j, i: (i, j)),
        out_shape=jax.ShapeDtypeStruct((m, n), F32),
        scratch_shapes=[pltpu.VMEM((d, bn), BF16)],
        compiler_params=_params(("arbitrary", "arbitrary")),
        name="qkv_rope",
    )(h, w_qkv, cos_t, sin_t)


def rope_tables(t_prompt, n_sample):
    half = HEAD_DIM // 2
    inv = ROPE_THETA ** (-jnp.arange(half, dtype=F32) / half)
    pos = jnp.concatenate([jnp.arange(t_prompt), jnp.full((n_sample,), PAST_LEN)]).astype(F32)
    ang = pos[:, None] * inv[None, :]
    cos, sin = jnp.cos(ang), jnp.sin(ang)
    reps = LANES // HEAD_DIM
    return (jnp.tile(jnp.concatenate([cos, cos], axis=1), (1, reps)),
            jnp.tile(jnp.concatenate([-sin, sin], axis=1), (1, reps)))


def _swa_prompt_body(sink_ref, q_ref, kp_ref, kc_ref, vp_ref, vc_ref, wo_ref, o_ref, wob_ref, *, n_kv, nsub):
    n = pl.program_id(0)
    blk = WINDOW
    hd = HEAD_DIM
    assert LANES == 2 * hd
    wob_ref[...] = wo_ref[...].astype(BF16)
    nq = GQA_GROUP * blk
    keys = lax.broadcasted_iota(jnp.int32, (2 * blk, nq), 0)
    qrow = lax.broadcasted_iota(jnp.int32, (2 * blk, nq), 1) % blk
    diff = qrow + blk - keys
    in_window = (diff >= 0) & (diff < WINDOW)
    masks = [in_window & ((keys >= blk) | (n > 0))] + [in_window] * (nsub - 1)
    head_of_col = lax.broadcasted_iota(jnp.int32, (1, nq), 1) // blk
    lane = lax.broadcasted_iota(jnp.int32, (1, LANES), 1)
    log2e = math.log2(math.e)
    qscale = log2e / math.sqrt(hd)
    def mine(kv):
        return (lane >= (kv % 2) * hd) & (lane < (kv % 2 + 1) * hd)

    def window_rows(prev_ref, cur_ref, sub, ps):
        prev = prev_ref[:, ps] if sub == 0 else cur_ref[(sub - 1) * blk:sub * blk, ps]
        return jnp.concatenate([prev, cur_ref[sub * blk:(sub + 1) * blk, ps]], axis=0)

    def scores(item):
        kv, sub = item
        ps = slice((kv // 2) * LANES, (kv // 2 + 1) * LANES)
        kpair = window_rows(kp_ref, kc_ref, sub, ps).astype(BF16)
        parts = []
        for g in range(GQA_GROUP):
            c0 = (kv * GQA_GROUP + g - (g % 2)) * hd
            src = q_ref[sub * blk:(sub + 1) * blk, c0:c0 + LANES] * qscale
            if g % 2 != kv % 2:
                src = pltpu.roll(src, hd, 1)
            parts.append(jnp.where(mine(kv), src, 0.0))
        qs = jnp.concatenate(parts, axis=0).astype(BF16)
        return lax.dot_general(kpair, qs, (((1,), (1,)), ((), ())), preferred_element_type=F32)

    def softmax(item, s):
        kv, sub = item
        s = jnp.where(masks[sub], s, -jnp.inf)
        sink = jnp.zeros((1, nq), F32)
        for g in range(GQA_GROUP):
            sink = jnp.where(head_of_col == g, sink_ref[kv * GQA_GROUP + g] * log2e, sink)
        mx = jnp.maximum(jnp.max(s, axis=0, keepdims=True), sink)
        return jnp.exp2(s - mx).astype(BF16), jnp.exp2(sink - mx)

    def weighted_values(item, pb, sink_term):
        kv, sub = item
        ps = slice((kv // 2) * LANES, (kv // 2 + 1) * LANES)
        vpair = window_rows(vp_ref, vc_ref, sub, ps)
        vaug = jnp.where(mine(kv), vpair, 1.0).astype(BF16)
        ot = lax.dot_general(vaug, pb, (((0,), (0,)), ((), ())), preferred_element_type=F32)
        half = kv % 2
        other = (1 - half) * hd
        denom = ot[other:other + 1, :] + sink_term
        on = (ot[half * hd:(half + 1) * hd, :] * (1.0 / denom)).astype(o_ref.dtype)
        for g in range(GQA_GROUP):
            r0 = (kv * GQA_GROUP + g) * hd
            o_ref[r0:r0 + hd, sub * blk:(sub + 1) * blk] = on[:, g * blk:(g + 1) * blk]

    @pl.when(n < pl.num_programs(0) - 1)
    def _():
        items = [(kv, sub) for sub in range(nsub) for kv in range(n_kv)]
        ahead = 2
        pending = {i: scores(items[i]) for i in range(min(ahead, len(items)))}
        for i, item in enumerate(items):
            pb, sink_term = softmax(item, pending.pop(i))
            if i + ahead < len(items):
                pending[i + ahead] = scores(items[i + ahead])
            weighted_values(item, pb, sink_term)

    @pl.when(n == pl.num_programs(0) - 1)
    def _():
        o_ref[...] = jnp.zeros_like(o_ref)


def swa_prompt(qkv, sinks, w_o, layer, t, d):
    n_kv = d // HEAD_DIM // GQA_GROUP
    kvw = n_kv * HEAD_DIM
    assert t % WINDOW == 0
    nsub = _pick(t // WINDOW, (4, 2, 1))
    nblk = t // (nsub * WINDOW)
    assert d % kvw == 0 and n_kv % 2 == 0 and d % nblk == 0 and (d // nblk) % 16 == 0
    wrows = d // nblk
    kblk = d // kvw
    vblk = kblk + 1
    assert 0 < qkv.shape[0] - t <= nsub * WINDOW
    cur = lambda n: jnp.minimum(n, nblk - 1)
    prev = lambda n: jnp.maximum(cur(n) * nsub - 1, 0)
    return pl.pallas_call(
        functools.partial(_swa_prompt_body, n_kv=n_kv, nsub=nsub),
        grid=(nblk + 1,),
        in_specs=[
            pl.BlockSpec(memory_space=pltpu.SMEM),
            pl.BlockSpec((nsub * WINDOW, d), lambda n: (cur(n), 0)),
            pl.BlockSpec((WINDOW, kvw), lambda n: (prev(n), kblk)),
            pl.BlockSpec((nsub * WINDOW, kvw), lambda n: (cur(n), kblk)),
            pl.BlockSpec((WINDOW, kvw), lambda n: (prev(n), vblk)),
            pl.BlockSpec((nsub * WINDOW, kvw), lambda n: (cur(n), vblk)),
            pl.BlockSpec((None, wrows, d), lambda n: (layer, cur(n), 0)),
        ],
        out_specs=[pl.BlockSpec((d, nsub * WINDOW), lambda n: (0, n)),
                   pl.BlockSpec((wrows, d), lambda n: (cur(n), 0))],
        out_shape=[jax.ShapeDtypeStruct((d, qkv.shape[0]), BF16), jax.ShapeDtypeStruct((d, d), BF16)],
        compiler_params=_params(("arbitrary",)),
        name="swa_prompt",
    )(sinks, qkv, qkv, qkv, qkv, qkv, w_o)


def _swa_sample_body(sink_ref, qkvt_ref, k_ref, v_ref, o_any, ko_ref, vo_ref, o_ref, acc_ref, *, n_kv, bt):
    del o_any
    i = pl.program_id(0)
    nk = k_ref.shape[1]
    nb = qkvt_ref.shape[1]
    kvw = n_kv * HEAD_DIM
    n_heads = n_kv * GQA_GROUP
    nq = n_heads * HEAD_DIM
    sub = 8

    @pl.when(i == 0)
    def _():
        acc_ref[...] = jnp.zeros_like(acc_ref)

    key_lane = lax.broadcasted_iota(jnp.int32, (1, nk), 1)
    sublane = lax.broadcasted_iota(jnp.int32, (sub, nk), 0)
    head_row = lax.broadcasted_iota(jnp.int32, (n_heads, 1), 0)
    sink = jnp.zeros((n_heads, 1), F32)
    for hd in range(n_heads):
        sink = jnp.where(head_row == hd, sink_ref[hd], sink)
    qt = (qkvt_ref[:nq, :] * (1.0 / math.sqrt(HEAD_DIM))).astype(BF16)
    knt = qkvt_ref[nq:nq + kvw, :]
    vnt = qkvt_ref[nq + kvw:, :]
    gs = bt
    wide_row = lax.broadcasted_iota(jnp.int32, (nb, gs * nk), 0)
    wide_lane_group = lax.broadcasted_iota(jnp.int32, (nb, gs * nk), 1) // nk
    tall_lane_group = lax.broadcasted_iota(jnp.int32, (gs * nk, nb), 0) // nk
    tall_lane = lax.broadcasted_iota(jnp.int32, (gs * nk, nb), 1)
    qcols_of = [jnp.dot(qt, (wide_row == wide_lane_group + (i * bt + g0)).astype(BF16),
                        preferred_element_type=F32) for g0 in range(0, bt, gs)]
    kfulls, vfulls = [], []
    assert nb == nk
    for j in range(bt):
        to_last = nk - 1 - (i * bt + j)
        kn = pltpu.roll(knt, to_last, 1)
        vn = pltpu.roll(vnt, to_last, 1)
        rows = slice(j * kvw, (j + 1) * kvw)
        kfulls.append(jnp.where(key_lane == nk - 1, kn, pltpu.roll(k_ref[rows, :], nk - 1, 1)))
        vfulls.append(jnp.where(key_lane == nk - 1, vn, pltpu.roll(v_ref[rows, :], nk - 1, 1)))
        ko_ref[rows, :] = kfulls[j]
        vo_ref[rows, :] = vfulls[j]
    all_scores = []
    for j in range(bt):
        qcols = qcols_of[j // gs][:, (j % gs) * nk:(j % gs + 1) * nk]
        kfull = kfulls[j]
        tiles = []
        for tile in range(n_heads // sub):
            st = jnp.zeros((sub, nk), F32)
            for r in range(sub):
                hidx = tile * sub + r
                kv = hidx // GQA_GROUP
                prod = kfull[kv * HEAD_DIM:(kv + 1) * HEAD_DIM] * qcols[hidx * HEAD_DIM:(hidx + 1) * HEAD_DIM]
                red = jnp.sum(prod.reshape(HEAD_DIM // sub, sub, nk), axis=0)
                step = sub // 2
                while step >= 1:
                    red = red + pltpu.roll(red, step, 0)
                    step //= 2
                st = jnp.where(sublane == r, red, st)
            tiles.append(st)
        all_scores.append(jnp.concatenate(tiles, axis=0))
    probs = []
    for s in all_scores:
        mx = jnp.maximum(jnp.max(s, axis=1, keepdims=True), sink)
        p = jnp.exp(s - mx)
        probs.append(p / (jnp.sum(p, axis=1, keepdims=True) + jnp.exp(sink - mx)))
    pvs = []
    for j in range(bt):
        pn, vfull = probs[j], vfulls[j]
        pvs.append(jnp.concatenate(
            [vfull[(hidx // GQA_GROUP) * HEAD_DIM:(hidx // GQA_GROUP + 1) * HEAD_DIM] * pn[hidx:hidx + 1, :]
             for hidx in range(n_heads)], axis=0).astype(BF16))
        if len(pvs) == gs:
            first = i * bt + j + 1 - gs
            acc_ref[...] += jnp.dot(jnp.concatenate(pvs, axis=1),
                                    (tall_lane_group + first == tall_lane).astype(BF16),
                                    preferred_element_type=F32)
            pvs = []

    @pl.when(i == pl.num_programs(0) - 1)
    def _():
        o_ref[...] = acc_ref[...].astype(o_ref.dtype)


def swa_sample(qkvt, kt, vt, sinks, o_fm, n_kv):
    nrow, b = qkvt.shape
    kvw = n_kv * HEAD_DIM
    d = nrow - 2 * kvw
    nk = kt.shape[1]
    t = o_fm.shape[1] - b
    assert d == n_kv * GQA_GROUP * HEAD_DIM and (n_kv * GQA_GROUP) % 8 == 0 and t % b == 0
    bt = _pick(b, (8, 4, 2, 1))
    cache_spec = pl.BlockSpec((bt * kvw, nk), lambda i: (i, 0))
    return pl.pallas_call(
        functools.partial(_swa_sample_body, n_kv=n_kv, bt=bt),
        grid=(b // bt,),
        in_specs=[
            pl.BlockSpec(memory_space=pltpu.SMEM),
            pl.BlockSpec((nrow, b), lambda i: (0, 0)),
            cache_spec,
            cache_spec,
            pl.BlockSpec(memory_space=pl.ANY),
        ],
        out_specs=[cache_spec, cache_spec, pl.BlockSpec((d, b), lambda i: (0, t // b))],
        out_shape=[jax.ShapeDtypeStruct(kt.shape, F32), jax.ShapeDtypeStruct(vt.shape, F32),
                   jax.ShapeDtypeStruct(o_fm.shape, o_fm.dtype)],
        scratch_shapes=[pltpu.VMEM((d, b), F32)],
        input_output_aliases={4: 2},
        compiler_params=_params(("arbitrary",)),
        name="swa_sample",
    )(sinks, qkvt, kt, vt, o_fm)


def _pool_rows(x_ref, g_ref, sc_ref, gn_ref, wb_ref, hl_ref, o_ref, hn_ref, gc):
    i = pl.program_id(0)
    bm = x_ref.shape[0]
    x = x_ref[...]
    h = _rms(x, g_ref[...])
    hprev = hl_ref[...]
    hl_ref[...] = h[bm - POOL_MAXW:, :]
    pos1 = (lax.broadcasted_iota(jnp.int32, (bm, 1), 0) + i * bm + 1).astype(F32)
    for gi, w in enumerate(POOL_WINDOWS):
        cs = slice(gi * gc, (gi + 1) * gc)
        hg = h[:, cs]
        acc = jnp.concatenate([hprev[:, cs], hg], axis=0)
        step = 1
        while step < w:
            acc = acc + pltpu.roll(acc, step, 0)
            step *= 2
        inv_cnt = 1.0 / jnp.minimum(jnp.float32(w), pos1)
        pooled = (acc[POOL_MAXW:, :] * inv_cnt - hg).astype(BF16)
        y = jnp.dot(pooled, wb_ref[gi], preferred_element_type=F32)
        o_ref[:, cs] = x[:, cs] + y * sc_ref[:, cs]
    hn_ref[...] = _rms(o_ref[...], gn_ref[...]).astype(hn_ref.dtype)


def _pool_sample_body(x_ref, st_ref, g_ref, w_ref, sc_ref, gn_ref, hn_any, o_ref, hn_ref, ns_ref, *, gc):
    del hn_any
    x = x_ref[...]
    h = _rms(x, g_ref[...])
    ns_ref[:POOL_BUF - 1] = st_ref[1:]
    ns_ref[POOL_BUF - 1] = h
    for gi, w in enumerate(POOL_WINDOWS):
        cs = slice(gi * gc, (gi + 1) * gc)
        hg = h[:, cs]
        tot = hg + jnp.sum(st_ref[POOL_BUF - (w - 1):, :, cs], axis=0)
        cnt = float(min(w, PAST_LEN + 1))
        pooled = (tot / cnt - hg).astype(BF16)
        y = jnp.dot(pooled, w_ref[gi].astype(BF16), preferred_element_type=F32)
        o_ref[:, cs] = x[:, cs] + y * sc_ref[:, cs]
    hn_ref[...] = _rms(o_ref[...], gn_ref[...]).astype(hn_ref.dtype)


def pool_sample(x_all, hn_all, state, layer, g, w_pool, scale, g_next, t):
    d = x_all.shape[1]
    b = state.shape[2]
    _, ng, gc, _ = w_pool.shape
    bt = _pick(b, (32, 16, 8))
    assert t % bt == 0
    off = t // bt
    return pl.pallas_call(
        functools.partial(_pool_sample_body, gc=gc),
        grid=(b // bt,),
        in_specs=[
            pl.BlockSpec((bt, d), lambda i: (off + i, 0)),
            pl.BlockSpec((None, POOL_BUF, bt, d), lambda i: (layer, 0, i, 0)),
            pl.BlockSpec((1, d), lambda i: (0, 0)),
            pl.BlockSpec((None, ng, gc, gc), lambda i: (layer, 0, 0, 0)),
            pl.BlockSpec((1, d), lambda i: (0, 0)),
            pl.BlockSpec((1, d), lambda i: (0, 0)),
            pl.BlockSpec(memory_space=pl.ANY),
        ],
        out_specs=[
            pl.BlockSpec((bt, d), lambda i: (off + i, 0)),
            pl.BlockSpec((bt, d), lambda i: (off + i, 0)),
            pl.BlockSpec((POOL_BUF, bt, d), lambda i: (0, i, 0)),
        ],
        out_shape=[jax.ShapeDtypeStruct(x_all.shape, F32), jax.ShapeDtypeStruct(hn_all.shape, hn_all.dtype),
                   jax.ShapeDtypeStruct((POOL_BUF, b, d), F32)],
        input_output_aliases={0: 0, 6: 1},
        compiler_params=_params(("arbitrary",)),
        name="pool_sample",
    )(x_all, state, g.reshape(1, d), w_pool, scale.reshape(1, d), g_next.reshape(1, d), hn_all)


def _mem_kv_body(m_ref, g_ref, w_ref, o_ref):
    h = _rms(m_ref[...], g_ref[...]).astype(BF16)
    o_ref[...] = jnp.dot(h, w_ref[...].astype(BF16), preferred_element_type=F32)


def mem_kv(mem, g, w_kv, layer):
    m, d = mem.shape
    n = w_kv.shape[2]
    bn = _pick(n, (512, 256, 128))
    return pl.pallas_call(
        _mem_kv_body,
        grid=(n // bn,),
        in_specs=[
            pl.BlockSpec((m, d), lambda j: (0, 0)),
            pl.BlockSpec((1, d), lambda j: (0, 0)),
            pl.BlockSpec((None, d, bn), lambda j: (layer, 0, j)),
        ],
        out_specs=pl.BlockSpec((m, bn), lambda j: (0, j)),
        out_shape=jax.ShapeDtypeStruct((m, n), F32),
        compiler_params=_params(("arbitrary",)),
        name="mem_kv",
    )(mem, g.reshape(1, d), w_kv)


def _xattn_prompt_body(h_ref, wq_ref, kv_ref, wo_ref, x_ref, gn_ref, o_ref, hn_ref, wqb_ref, wob_ref, kvb_ref):
    @pl.when(pl.program_id(0) == 0)
    def _():
        wqb_ref[...] = wq_ref[...].astype(BF16)
        wob_ref[...] = wo_ref[...].astype(BF16)
        kvb_ref[...] = kv_ref[...].astype(BF16)

    _xattn_rows(h_ref, x_ref, gn_ref, wqb_ref, wob_ref, kvb_ref, o_ref, hn_ref)


def _xattn_rows(h_ref, x_ref, gn_ref, wqb_ref, wob_ref, kvb_ref, o_ref, hn_ref):
    xw = X_HEADS * X_HEAD_DIM
    bm = h_ref.shape[0]
    nchunk = 2 if bm % 32 == 0 else 1
    rows = [slice(c * (bm // nchunk), (c + 1) * (bm // nchunk)) for c in range(nchunk)]
    qs = [jnp.dot(h_ref[r, :], wqb_ref[...], preferred_element_type=F32) / math.sqrt(X_HEAD_DIM) for r in rows]
    scores = [[lax.dot_general(kvb_ref[:, hd * X_HEAD_DIM:(hd + 1) * X_HEAD_DIM],
                               q[:, hd * X_HEAD_DIM:(hd + 1) * X_HEAD_DIM].astype(BF16), (((1,), (1,)), ((), ())),
                               preferred_element_type=F32) for hd in range(X_HEADS)] for q in qs]
    attn = []
    for c in range(nchunk):
        outs = []
        for hd in range(X_HEADS):
            s = scores[c][hd]
            p = jnp.exp(s - jnp.max(s, axis=0, keepdims=True))
            l = jnp.sum(p, axis=0, keepdims=True)
            vs = slice(xw + hd * X_HEAD_DIM, xw + (hd + 1) * X_HEAD_DIM)
            outs.append(lax.dot_general(kvb_ref[:, vs], p.astype(BF16), (((0,), (0,)), ((), ())),
                                        preferred_element_type=F32) / l)
        attn.append(jnp.concatenate(outs, axis=0).astype(BF16))
    for c, r in enumerate(rows):
        xn = x_ref[r, :] + lax.dot_general(attn[c], wob_ref[...], (((0,), (0,)), ((), ())),
                                           preferred_element_type=F32)
        o_ref[r, :] = xn
        hn_ref[r, :] = _rms(xn, gn_ref[...]).astype(hn_ref.dtype)


def xattn_prompt(hq_all, w_q, mkv, w_o, layer, x_all, g_next, t):
    d = x_all.shape[1]
    xw = w_q.shape[2]
    nm = mkv.shape[0]
    bm = _pick(t, (512, 256, 128, 64, 32, 16))
    return pl.pallas_call(
        _xattn_prompt_body,
        grid=(t // bm,),
        in_specs=[
            pl.BlockSpec((bm, d), lambda i: (i, 0)),
            pl.BlockSpec((None, d, xw), lambda i: (layer, 0, 0)),
            pl.BlockSpec((nm, 2 * xw), lambda i: (0, 0)),
            pl.BlockSpec((None, xw, d), lambda i: (layer, 0, 0)),
            pl.BlockSpec((bm, d), lambda i: (i, 0)),
            pl.BlockSpec((1, d), lambda i: (0, 0)),
        ],
        out_specs=[pl.BlockSpec((bm, d), lambda i: (i, 0)), pl.BlockSpec((bm, d), lambda i: (i, 0))],
        out_shape=[jax.ShapeDtypeStruct(x_all.shape, F32), jax.ShapeDtypeStruct(x_all.shape, BF16)],
        scratch_shapes=[pltpu.VMEM((d, xw), BF16), pltpu.VMEM((xw, d), BF16), pltpu.VMEM((nm, 2 * xw), BF16)],
        input_output_aliases={4: 0, 0: 1},
        compiler_params=_params(("arbitrary",)),
        name="xattn_prompt",
    )(hq_all, w_q, mkv, w_o, x_all, g_next.reshape(1, d))


def _pool_xattn_body(x_ref, g_ref, wp_ref, sc_ref, gq_ref, wq_ref, kv_ref, wo_ref, gn_ref, hn_any,
                     o_ref, hn_ref, hl_ref, wpb_ref, wqb_ref, wob_ref, kvb_ref, x1_ref, hq_ref, *, gc):
    del hn_any

    @pl.when(pl.program_id(0) == 0)
    def _():
        wpb_ref[...] = wp_ref[...].astype(BF16)
        wqb_ref[...] = wq_ref[...].astype(BF16)
        wob_ref[...] = wo_ref[...].astype(BF16)
        kvb_ref[...] = kv_ref[...].astype(BF16)
        hl_ref[...] = jnp.zeros_like(hl_ref)

    _pool_rows(x_ref, g_ref, sc_ref, gq_ref, wpb_ref, hl_ref, x1_ref, hq_ref, gc)
    _xattn_rows(hq_ref, x1_ref, gn_ref, wqb_ref, wob_ref, kvb_ref, o_ref, hn_ref)


def pool_xattn_prompt(x_all, hn_all, g, w_pool, pool_layer, scale, g_xq, w_q, mkv, w_o, layer, g_next, t):
    d = x_all.shape[1]
    _, ng, gc, _ = w_pool.shape
    xw = w_q.shape[2]
    nm = mkv.shape[0]
    bm = _pick(t, (512, 256, 128, 64, 32, 16))
    row = lambda i: (i, 0)
    vec = pl.BlockSpec((1, d), lambda i: (0, 0))
    once = pl.Buffered(1)
    return pl.pallas_call(
        functools.partial(_pool_xattn_body, gc=gc),
        grid=(t // bm,),
        in_specs=[
            pl.BlockSpec((bm, d), row),
            vec,
            pl.BlockSpec((None, ng, gc, gc), lambda i: (pool_layer, 0, 0, 0), pipeline_mode=once),
            vec,
            vec,
            pl.BlockSpec((None, d, xw), lambda i: (layer, 0, 0), pipeline_mode=once),
            pl.BlockSpec((nm, 2 * xw), lambda i: (0, 0), pipeline_mode=once),
            pl.BlockSpec((None, xw, d), lambda i: (layer, 0, 0), pipeline_mode=once),
            vec,
            pl.BlockSpec(memory_space=pl.ANY),
        ],
        out_specs=[pl.BlockSpec((bm, d), row), pl.BlockSpec((bm, d), row),
                   pl.BlockSpec((POOL_MAXW, d), lambda i: (0, 0))],
        out_shape=[jax.ShapeDtypeStruct(x_all.shape, F32), jax.ShapeDtypeStruct(hn_all.shape, hn_all.dtype),
                   jax.ShapeDtypeStruct((POOL_MAXW, d), F32)],
        scratch_shapes=[pltpu.VMEM((ng, gc, gc), BF16), pltpu.VMEM((d, xw), BF16), pltpu.VMEM((xw, d), BF16),
                        pltpu.VMEM((nm, 2 * xw), BF16), pltpu.VMEM((bm, d), F32), pltpu.VMEM((bm, d), BF16)],
        input_output_aliases={0: 0, 9: 1},
        compiler_params=_params(("arbitrary",)),
        name="pool_xattn_prompt",
    )(x_all, g.reshape(1, d), w_pool, scale.reshape(1, d), g_xq.reshape(1, d), w_q, mkv, w_o,
      g_next.reshape(1, d), hn_all)


def _xq_sample_body(h_ref, wq_ref, o_ref):
    o_ref[...] = jnp.dot(h_ref[...], wq_ref[...].astype(BF16), preferred_element_type=F32) / math.sqrt(X_HEAD_DIM)


def xq_sample(hq_all, w_q, layer, t, b):
    d = hq_all.shape[1]
    xw = w_q.shape[2]
    assert t % b == 0
    return pl.pallas_call(
        _xq_sample_body,
        grid=(1,),
        in_specs=[
            pl.BlockSpec((b, d), lambda i: (t // b, 0)),
            pl.BlockSpec((None, d, xw), lambda i: (layer, 0, 0)),
        ],
        out_specs=pl.BlockSpec((b, xw), lambda i: (0, 0)),
        out_shape=jax.ShapeDtypeStruct((b, xw), F32),
        compiler_params=_params(("arbitrary",)),
        name="xq_sample",
    )(hq_all, w_q)


XATTN_RING_SLOTS = 3


def _xattn_core_body(q_ref, k_hbm, v_hbm, ones_ref, o_ref, kbuf, vbuf, sem, *, first_block, nblk):
    step = pl.program_id(0)
    rows = kbuf.shape[1]

    def copies(s):
        slot = lax.rem(s, XATTN_RING_SLOTS)
        src = pl.ds((first_block + s) * rows, rows)
        return (pltpu.make_async_copy(k_hbm.at[src, :], kbuf.at[slot], sem.at[0, slot]),
                pltpu.make_async_copy(v_hbm.at[src, :], vbuf.at[slot], sem.at[1, slot]))

    def start(s):
        for c in copies(s):
            c.start()

    @pl.when(step == 0)
    def _():
        for s in range(min(XATTN_RING_SLOTS - 1, nblk)):
            start(s)

    @pl.when(step + (XATTN_RING_SLOTS - 1) < nblk)
    def _():
        start(step + (XATTN_RING_SLOTS - 1))

    for c in copies(step):
        c.wait()
    slot = lax.rem(step, XATTN_RING_SLOTS)
    k_ref = kbuf.at[slot]
    v_ref = vbuf.at[slot]
    bt, sub, hd = q_ref.shape
    nv = k_ref.shape[0] // (bt * sub)
    k = k_ref[...].reshape(bt, nv, sub, hd)
    prod = (k * q_ref[...][:, None]).reshape(bt * nv * sub, hd).astype(BF16)
    s = jnp.dot(prod, ones_ref[...], preferred_element_type=F32).reshape(bt, nv, sub, hd)

    def fold(x, op):
        step = X_HEADS
        while step < sub:
            x = op(x, pltpu.roll(x, step, 2))
            step *= 2
        return x

    mx = fold(jnp.max(s, axis=1, keepdims=True), jnp.maximum)
    p = jnp.exp(s - mx)
    l = fold(jnp.sum(p, axis=1, keepdims=True), jnp.add)
    acc = fold(jnp.sum(p * v_ref[...].reshape(bt, nv, sub, hd), axis=1, keepdims=True), jnp.add)
    o_ref[...] = (acc / l).reshape(bt, sub, hd)


def xattn_core_sample(q_s, mk_flat, mv_flat, layer, nm):
    b = q_s.shape[0]
    hd = X_HEAD_DIM
    sub = 8
    assert sub % X_HEADS == 0 and nm % (sub // X_HEADS) == 0
    bt = _pick(b, (8, 4, 2, 1))
    nblk = b // bt
    rows = bt * nm * X_HEADS
    q8 = jnp.tile(q_s.reshape(b, X_HEADS, hd), (1, sub // X_HEADS, 1))
    o8 = pl.pallas_call(
        functools.partial(_xattn_core_body, first_block=layer * nblk, nblk=nblk),
        grid=(nblk,),
        in_specs=[
            pl.BlockSpec((bt, sub, hd), lambda i: (i, 0, 0)),
            pl.BlockSpec(memory_space=pl.ANY),
            pl.BlockSpec(memory_space=pl.ANY),
            pl.BlockSpec((hd, hd), lambda i: (0, 0)),
        ],
        out_specs=pl.BlockSpec((bt, sub, hd), lambda i: (i, 0, 0)),
        out_shape=jax.ShapeDtypeStruct((b, sub, hd), F32),
        scratch_shapes=[pltpu.VMEM((XATTN_RING_SLOTS, rows, hd), F32), pltpu.VMEM((XATTN_RING_SLOTS, rows, hd), F32),
                        pltpu.SemaphoreType.DMA((2, XATTN_RING_SLOTS))],
        compiler_params=_params(("arbitrary",)),
        name="xattn_core_sample",
    )(q8, mk_flat, mv_flat, jnp.ones((hd, hd), BF16))
    return o8[:, :X_HEADS].reshape(b, X_HEADS * hd)


def _xout_sample_body(a_ref, wo_ref, x_ref, gn_ref, hn_any, o_ref, hn_ref):
    del hn_any
    xn = x_ref[...] + jnp.dot(a_ref[...].astype(BF16), wo_ref[...].astype(BF16), preferred_element_type=F32)
    o_ref[...] = xn
    hn_ref[...] = _rms(xn, gn_ref[...]).astype(hn_ref.dtype)


def xout_sample(o_s, w_o, layer, x_all, hn_all, g_next, t):
    d = x_all.shape[1]
    b, xw = o_s.shape
    assert t % b == 0
    off = t // b
    return pl.pallas_call(
        _xout_sample_body,
        grid=(1,),
        in_specs=[
            pl.BlockSpec((b, xw), lambda i: (0, 0)),
            pl.BlockSpec((None, xw, d), lambda i: (layer, 0, 0)),
            pl.BlockSpec((b, d), lambda i: (off, 0)),
            pl.BlockSpec((1, d), lambda i: (0, 0)),
            pl.BlockSpec(memory_space=pl.ANY),
        ],
        out_specs=[pl.BlockSpec((b, d), lambda i: (off, 0)), pl.BlockSpec((b, d), lambda i: (off, 0))],
        out_shape=[jax.ShapeDtypeStruct(x_all.shape, F32), jax.ShapeDtypeStruct(hn_all.shape, hn_all.dtype)],
        input_output_aliases={2: 0, 4: 1},
        compiler_params=_params(("arbitrary",)),
        name="xout_sample",
    )(o_s, w_o, x_all, g_next.reshape(1, d), hn_all)


def kernel(x_prompt, x_sample, state_pool, cache_swa_k, cache_swa_v, cache_mem_k, cache_mem_v, mem_prompt,
           g_ffn1, w_ffn1_gu, w_ffn1_dn, g_mix, w_pool, pool_scale, w_qkv, w_o, sinks,
           g_xq, g_mem, w_xq, w_xkv, w_xo, g_ffn2, w_ffn2_gu, w_ffn2_dn, g_final):
    bp, t, d = x_prompt.shape
    b, s_len, _ = x_sample.shape
    assert bp == 1 and s_len == 1
    depth = g_ffn1.shape[0]
    xw = w_xq.shape[2]
    nm = mem_prompt.shape[1]
    n_kv = d // HEAD_DIM // GQA_GROUP
    kvw = n_kv * HEAD_DIM

    h = stacked_norm(x_prompt[0], x_sample[:, 0], g_ffn1[0])
    x = None
    cos_t, sin_t = rope_tables(t, b)
    mk_flat = cache_mem_k.reshape(-1, X_HEAD_DIM)
    mv_flat = cache_mem_v.reshape(-1, X_HEAD_DIM)
    state_sm = jnp.swapaxes(state_pool, 1, 2)

    pool_p, pool_s = [], []
    swa_kp, swa_vp, swa_ks, swa_vs = [], [], [], []
    mem_kp, mem_vp = [], []
    y_p = y_s = None
    for layer in range(depth):
        i = layer // 2
        act, wdb = gate_up(h, w_ffn1_gu, w_ffn1_dn, layer)
        if x is None:
            x, h = down_norm_stacking(act, wdb, x_prompt[0], x_sample[:, 0], g_mix[layer], BF16)
        else:
            x, h = down_norm(act, wdb, x, g_mix[layer], BF16)
        mkv = mem_kv(mem_prompt[0], g_mem[layer], w_xkv, layer)
        mem_kp.append(mkv[:, :xw].reshape(1, nm, X_HEADS, X_HEAD_DIM))
        mem_vp.append(mkv[:, xw:].reshape(1, nm, X_HEADS, X_HEAD_DIM))
        if layer % 2 == 0:
            x, hq, new_state = pool_sample(x, h, state_sm, i, g_mix[layer], w_pool, pool_scale[i], g_xq[layer], t)
            q_s = xq_sample(hq, w_xq, layer, t, b)
            x, h, h_last = pool_xattn_prompt(x, hq, g_mix[layer], w_pool, i, pool_scale[i], g_xq[layer],
                                             w_xq, mkv, w_xo, layer, g_ffn2[layer], t)
            pool_p.append(h_last[None, POOL_MAXW - POOL_BUF:])
            pool_s.append(jnp.swapaxes(new_state, 0, 1))
        else:
            qkv = qkv_rope(h, w_qkv, i, cos_t, sin_t, d + kvw, kvw)
            buf = cache_swa_k.shape[2]
            keep = min(WINDOW, t)
            swa_kp.append(qkv[t - keep:t, d:d + kvw].reshape(1, keep, n_kv, HEAD_DIM))
            swa_vp.append(qkv[t - keep:t, d + kvw:].reshape(1, keep, n_kv, HEAD_DIM))
            assert buf == WINDOW
            to_fm = lambda c: jnp.transpose(c, (0, 2, 3, 1)).reshape(b * kvw, buf)
            from_fm = lambda c: jnp.transpose(c.reshape(b, n_kv, HEAD_DIM, buf), (0, 3, 1, 2))
            o_fm, wo_bf16 = swa_prompt(qkv, sinks[i], w_o, i, t, d)
            ks_fm, vs_fm, o_fm = swa_sample(qkv[t:].T, to_fm(cache_swa_k[i]), to_fm(cache_swa_v[i]), sinks[i],
                                            o_fm, n_kv)
            swa_ks.append(from_fm(ks_fm))
            swa_vs.append(from_fm(vs_fm))
            x, hq = proj_residual_norm(o_fm, wo_bf16, x, g_xq[layer])
            q_s = xq_sample(hq, w_xq, layer, t, b)
            x, h = xattn_prompt(hq, w_xq, mkv, w_xo, layer, x, g_ffn2[layer], t)
        o_s = xattn_core_sample(q_s, mk_flat, mv_flat, layer, nm)
        x, h = xout_sample(o_s, w_xo, layer, x, h, g_ffn2[layer], t)
        act, wdb = gate_up(h, w_ffn2_gu, w_ffn2_dn, layer)
        if layer + 1 < depth:
            x, h = down_norm(act, wdb, x, g_ffn1[layer + 1], BF16)
        else:
            (y_p,) = down_norm(act, wdb, x, g_final, F32, emit_x=False, row0=0, nrows=t)
            (y_s,) = down_norm(act, wdb, x, g_final, F32, emit_x=False, row0=t, nrows=b)
    return (y_p[None], y_s[:, None], jnp.stack(pool_p), jnp.stack(pool_s), jnp.stack(swa_kp), jnp.stack(swa_vp),
            jnp.stack(swa_ks), jnp.stack(swa_vs), jnp.stack(mem_kp), jnp.stack(mem_vp))
```
